```python
import math
import jax, jax.numpy as jnp
from jax import lax
import numpy as np

D_MODEL = 1024
BATCH = 8
SEQ = 2048
DEPTH = 4
DEC_BATCH = 128
DEC_SEQ = 8
PAST_LEN = 16384
PAGE_SIZE = 128

D_MIX = 1536
GROUP_W = D_MIX // 3
H_A = 4
DH_A = GROUP_W // H_A
GATE_SOFTCAP = 15.0
P_B = 64
H_B = GROUP_W // P_B
N_B = 128
G_B = 2
CONV_W = 4
CONV_DIM = GROUP_W + 2 * G_B * N_B
H_C = 4
DH_C = GROUP_W // H_C
ROPE_BASE = 10000.0
CHUNK = 128
D_FF = 2816
EPS = 1e-6
A_COLS = 4 * GROUP_W + 2 * H_A
B_COLS = GROUP_W + CONV_DIM + H_B
C_COLS = 4 * GROUP_W
D_IN = A_COLS + B_COLS + C_COLS

kernel_name = 'hymba_mlstm_ssd_retention_macaron_step'

F32 = jnp.float32


def rmsnorm(x, g):
    xf = x.astype(F32)
    y = xf * lax.rsqrt(jnp.mean(xf * xf, axis=-1, keepdims=True) + EPS)
    return (y * g.astype(F32)).astype(x.dtype)


def group_rmsnorm(h, g, n_groups):
    shp = h.shape
    hf = h.astype(F32).reshape(shp[:-1] + (n_groups, shp[-1] // n_groups))
    hf = hf * lax.rsqrt(jnp.mean(hf * hf, axis=-1, keepdims=True) + EPS)
    return hf.reshape(shp) * g.astype(F32)


def swiglu(x, w1, w3, w2):
    return (jax.nn.silu(x @ w1) * (x @ w3)) @ w2


def soft_cap(t):
    return GATE_SOFTCAP * jnp.tanh(t / GATE_SOFTCAP)


def to_chunks(t, lc):
    b, l = t.shape[:2]
    return jnp.moveaxis(t.reshape((b, l // lc, lc) + t.shape[2:]), 1, 0)


def from_chunks(t):
    nc, b, lc = t.shape[:3]
    return jnp.moveaxis(t, 0, 1).reshape((b, nc * lc) + t.shape[3:])


def rotary(t, pos):
    half = t.shape[-1] // 2
    freqs = ROPE_BASE ** (-jnp.arange(half, dtype=F32) / half)
    ang = pos.astype(F32)[:, None] * freqs
    cos, sin = jnp.cos(ang)[None, :, None, :], jnp.sin(ang)[None, :, None, :]
    t1, t2 = t[..., :half].astype(F32), t[..., half:].astype(F32)
    return jnp.concatenate([t1 * cos - t2 * sin, t1 * sin + t2 * cos], axis=-1).astype(t.dtype)


def mlstm(q, k, v, i_pre, log_f, C0, n0, m0):
    lc = math.gcd(q.shape[1], CHUNK)
    causal = jnp.tril(jnp.ones((lc, lc), dtype=bool))[None, :, :, None]

    def step(carry, inp):
        C, n, m = carry
        qc, kc, vc, ic, fc = inp
        b = jnp.cumsum(fc, axis=1)
        d = jnp.where(causal, b[:, :, None] - b[:, None] + ic[:, None], -jnp.inf)
        inter = b + m[:, None]
        m_t = jnp.maximum(inter, jnp.max(d, axis=2))
        s = jnp.einsum('bthd,bshd->btsh', qc, kc) * jnp.exp(d - m_t[:, :, None])
        g = jnp.exp(inter - m_t)
        num = jnp.einsum('btsh,bshd->bthd', s, vc) + g[..., None] * jnp.einsum('bthd,bhde->bthe', qc, C)
        den = jnp.sum(s, axis=2) + g * jnp.einsum('bthd,bhd->bth', qc, n)
        h = num / jnp.maximum(jnp.abs(den), jnp.exp(-m_t))[..., None]
        m_end = m_t[:, -1]
        w_end = jnp.exp(b[:, -1:] - b + ic - m_end[:, None])
        g_end = jnp.exp(inter[:, -1] - m_end)
        kw = kc * w_end[..., None]
        C_new = g_end[..., None, None] * C + jnp.einsum('bshd,bshe->bhde', kw, vc)
        n_new = g_end[..., None] * n + jnp.sum(kw, axis=1)
        return (C_new, n_new, m_end), h

    (C1, n1, m1), h = lax.scan(step, (C0.astype(F32), n0.astype(F32), m0.astype(F32)),
                               tuple(to_chunks(t, lc) for t in (q, k, v, i_pre, log_f)))
    return from_chunks(h), C1, n1, m1


def causal_conv(xbc, buf, w, b):
    xp = jnp.concatenate([buf.astype(xbc.dtype), xbc], axis=1)
    y = lax.conv_general_dilated(xp, w.astype(xp.dtype)[:, None, :], window_strides=(1,), padding='VALID',
                                 dimension_numbers=('NWC', 'WIO', 'NWC'), feature_group_count=CONV_DIM)
    return jax.nn.silu(y + b.astype(y.dtype)), xp[:, -(CONV_W - 1):]


def ssd(x, dt, a, Bm, Cm, h0):
    bsz, L = x.shape[:2]
    R = H_B // G_B
    lc = math.gcd(L, CHUNK)
    causal = jnp.tril(jnp.ones((lc, lc), dtype=bool))[None, :, :, None, None]
    xg = x.reshape(bsz, L, G_B, R, P_B)
    dtg = dt.reshape(bsz, L, G_B, R)
    ag = a.reshape(G_B, R)

    def step(h, inp):
        xc, dtc, Bc, Cc = inp
        acum = jnp.cumsum(dtc * ag, axis=1)
        decay = jnp.exp(jnp.where(causal, acum[:, :, None] - acum[:, None], -jnp.inf))
        xdt = xc * dtc[..., None]
        cb = jnp.einsum('btgn,bsgn->btsg', Cc, Bc)
        y = (jnp.einsum('btsgr,bsgrp->btgrp', cb[..., None] * decay, xdt)
             + jnp.exp(acum)[..., None] * jnp.einsum('btgn,bgrpn->btgrp', Cc, h))
        xw = xdt * jnp.exp(acum[:, -1:] - acum)[..., None]
        h_new = jnp.exp(acum[:, -1])[..., None, None] * h + jnp.einsum('bsgn,bsgrp->bgrpn', Bc, xw)
        return h_new, y

    h1, y = lax.scan(step, h0.astype(F32).reshape(bsz, G_B, R, P_B, N_B),
                     tuple(to_chunks(t, lc) for t in (xg, dtg, Bm, Cm)))
    return from_chunks(y).reshape(bsz, L, H_B, P_B), h1.reshape(bsz, H_B, P_B, N_B)


def retention(q, k, v, S0):
    lc = math.gcd(q.shape[1], CHUNK)
    log_gamma = jnp.log(1.0 - 2.0 ** (-5.0 - jnp.arange(H_C, dtype=F32)))
    idx = jnp.arange(lc, dtype=F32)
    causal = (idx[:, None] >= idx[None, :])[..., None]
    decay = jnp.where(causal, jnp.exp((idx[:, None] - idx[None, :])[..., None] * log_gamma), 0.0)
    inner = jnp.exp((idx + 1.0)[:, None] * log_gamma)
    end = jnp.exp((lc - 1.0 - idx)[:, None] * log_gamma)
    g_len = jnp.exp(lc * log_gamma)

    def step(S, inp):
        qc, kc, vc = inp
        s = jnp.einsum('bthd,bshd->btsh', qc, kc) * decay
        o = jnp.einsum('btsh,bshe->bthe', s, vc) + inner[None, :, :, None] * jnp.einsum('bthd,bhde->bthe', qc, S)
        S_new = g_len[None, :, None, None] * S + jnp.einsum('bshd,bshe->bhde', kc * end[None, :, :, None], vc)
        return S_new, o

    S1, o = lax.scan(step, S0.astype(F32), tuple(to_chunks(t, lc) for t in (q, k, v)))
    return from_chunks(o), S1


def decoder_layer(x, pos0, C0, n0, m0, h0, buf0, S0,
                  ffn1_norm, ffn1_w1, ffn1_w3, ffn1_w2, mix_norm, w_in, b_igate, b_fgate, mlstm_norm,
                  conv_w, conv_b, dt_bias, a_log, d_skip, ssd_norm, ret_norm, w_out,
                  ffn2_norm, ffn2_w1, ffn2_w3, ffn2_w2):
    x = x + 0.5 * swiglu(rmsnorm(x, ffn1_norm), ffn1_w1, ffn1_w3, ffn1_w2)
    h = rmsnorm(x, mix_norm)
    bsz, L, _ = h.shape
    proj = h @ w_in
    a_part, b_part, c_part = jnp.split(proj, [A_COLS, A_COLS + B_COLS], axis=-1)

    qa, ka, va, oa, ia, fa = jnp.split(a_part, [GROUP_W, 2 * GROUP_W, 3 * GROUP_W, 4 * GROUP_W, 4 * GROUP_W + H_A], axis=-1)
    heads_a = lambda t: t.reshape(bsz, L, H_A, DH_A)
    i_pre = soft_cap(ia.astype(F32) + b_igate.astype(F32))
    log_f = jax.nn.log_sigmoid(soft_cap(fa.astype(F32) + b_fgate.astype(F32)))
    ha, C1, n1, m1 = mlstm(heads_a(qa) * (DH_A ** -0.5), heads_a(ka), heads_a(va), i_pre, log_f, C0, n0, m0)
    out_a = group_rmsnorm(ha.reshape(bsz, L, GROUP_W), mlstm_norm, H_A) * jax.nn.sigmoid(oa.astype(F32))

    z, xbc, dt_raw = jnp.split(b_part, [GROUP_W, GROUP_W + CONV_DIM], axis=-1)
    xbc_c, buf1 = causal_conv(xbc, buf0, conv_w, conv_b)
    xb, Bm, Cm = jnp.split(xbc_c, [GROUP_W, GROUP_W + G_B * N_B], axis=-1)
    xb = xb.reshape(bsz, L, H_B, P_B)
    dt = jax.nn.softplus(dt_raw.astype(F32) + dt_bias.astype(F32))
    a = -jnp.exp(a_log.astype(F32))
    yb, h1 = ssd(xb, dt, a, Bm.reshape(bsz, L, G_B, N_B), Cm.reshape(bsz, L, G_B, N_B), h0)
    yb = yb + d_skip.astype(F32)[:, None] * xb.astype(F32)
    out_b = group_rmsnorm(yb.reshape(bsz, L, GROUP_W) * jax.nn.silu(z.astype(F32)), ssd_norm, G_B)

    qc, kc, vc, gc = jnp.split(c_part, [GROUP_W, 2 * GROUP_W, 3 * GROUP_W], axis=-1)
    pos = pos0 + jnp.arange(L)
    heads_c = lambda t: t.reshape(bsz, L, H_C, DH_C)
    oc, S1 = retention(rotary(heads_c(qc), pos), rotary(heads_c(kc), pos) * (DH_C ** -0.5), heads_c(vc), S0)
    out_c = group_rmsnorm(oc.reshape(bsz, L, GROUP_W), ret_norm, H_C) * jax.nn.silu(gc.astype(F32))

    mixed = jnp.concatenate([out_a, out_b, out_c], axis=-1).astype(x.dtype) @ w_out
    x = x + mixed
    x = x + 0.5 * swiglu(rmsnorm(x, ffn2_norm), ffn2_w1, ffn2_w3, ffn2_w2)
    return x, C1, n1, m1, h1, buf1, S1


def run_trunk(x, pos0, st_C, st_n, st_m, st_h, st_buf, st_S, layer_params, final_norm):
    news = [[] for _ in range(6)]
    for l in range(DEPTH):
        x, *st = decoder_layer(x, pos0, st_C[l], st_n[l], st_m[l], st_h[l], st_buf[l], st_S[l],
                               *[p[l] for p in layer_params])
        for acc, s in zip(news, st):
            acc.append(s)
    return rmsnorm(x, final_norm), [jnp.stack(acc) for acc in news]


def setup_inputs(seed: int = 0) -> dict:
    key = jax.random.key(seed)
    ks = iter(jax.random.split(key, 40))
    nrm = lambda shape, scale: scale * jax.random.normal(next(ks), shape, F32)
    gain = lambda shape: 1.0 + nrm(shape, 0.02)
    x_prompt = nrm((BATCH, SEQ, D_MODEL), 1.0)
    x_sample = nrm((DEC_BATCH, DEC_SEQ, D_MODEL), 1.0)
    state_mlstm_C = nrm((DEPTH, DEC_BATCH, H_A, DH_A, DH_A), 0.1)
    state_mlstm_n = nrm((DEPTH, DEC_BATCH, H_A, DH_A), 0.1)
    state_mlstm_m = 1.0 + nrm((DEPTH, DEC_BATCH, H_A), 0.5)
    state_ssd = nrm((DEPTH, DEC_BATCH, H_B, P_B, N_B), 0.5)
    state_conv = nrm((DEPTH, DEC_BATCH, CONV_W - 1, CONV_DIM), 1.0)
    state_ret = nrm((DEPTH, DEC_BATCH, H_C, DH_C, DH_C), 0.5)
    ffn1_norm = gain((DEPTH, D_MODEL))
    ffn1_w1 = nrm((DEPTH, D_MODEL, D_FF), D_MODEL ** -0.5)
    ffn1_w3 = nrm((DEPTH, D_MODEL, D_FF), D_MODEL ** -0.5)
    ffn1_w2 = nrm((DEPTH, D_FF, D_MODEL), D_FF ** -0.5)
    mix_norm = gain((DEPTH, D_MODEL))
    w_in = nrm((DEPTH, D_MODEL, D_IN), D_MODEL ** -0.5)
    b_igate = nrm((DEPTH, H_A), 0.1)
    b_fgate = jnp.linspace(3.0, 6.0, H_A, dtype=F32)[None, :] + nrm((DEPTH, H_A), 0.1)
    mlstm_norm = gain((DEPTH, GROUP_W))
    conv_w = nrm((DEPTH, CONV_W, CONV_DIM), CONV_W ** -0.5)
    conv_b = nrm((DEPTH, CONV_DIM), 0.01)
    dt0 = jnp.exp(jax.random.uniform(next(ks), (DEPTH, H_B), F32, math.log(1e-3), math.log(1e-1)))
    dt_bias = dt0 + jnp.log(-jnp.expm1(-dt0))
    a_log = jnp.log(jax.random.uniform(next(ks), (DEPTH, H_B), F32, 1.0, 16.0))
    d_skip = 1.0 + nrm((DEPTH, H_B), 0.1)
    ssd_norm = gain((DEPTH, GROUP_W))
    ret_norm = gain((DEPTH, GROUP_W))
    w_out = nrm((DEPTH, D_MIX, D_MODEL), D_MIX ** -0.5)
    ffn2_norm = gain((DEPTH, D_MODEL))
    ffn2_w1 = nrm((DEPTH, D_MODEL, D_FF), D_MODEL ** -0.5)
    ffn2_w3 = nrm((DEPTH, D_MODEL, D_FF), D_MODEL ** -0.5)
    ffn2_w2 = nrm((DEPTH, D_FF, D_MODEL), D_FF ** -0.5)
    final_norm = gain((D_MODEL,))
    return {'x_prompt': x_prompt, 'x_sample': x_sample,
            'state_mlstm_C': state_mlstm_C, 'state_mlstm_n': state_mlstm_n, 'state_mlstm_m': state_mlstm_m,
            'state_ssd': state_ssd, 'state_conv': state_conv, 'state_ret': state_ret,
            'ffn1_norm': ffn1_norm, 'ffn1_w1': ffn1_w1, 'ffn1_w3': ffn1_w3, 'ffn1_w2': ffn1_w2,
            'mix_norm': mix_norm, 'w_in': w_in, 'b_igate': b_igate, 'b_fgate': b_fgate, 'mlstm_norm': mlstm_norm,
            'conv_w': conv_w, 'conv_b': conv_b, 'dt_bias': dt_bias, 'a_log': a_log, 'd_skip': d_skip,
            'ssd_norm': ssd_norm, 'ret_norm': ret_norm, 'w_out': w_out,
            'ffn2_norm': ffn2_norm, 'ffn2_w1': ffn2_w1, 'ffn2_w3': ffn2_w3, 'ffn2_w2': ffn2_w2,
            'final_norm': final_norm}


def reference(x_prompt, x_sample, state_mlstm_C, state_mlstm_n, state_mlstm_m, state_ssd, state_conv, state_ret,
              ffn1_norm, ffn1_w1, ffn1_w3, ffn1_w2, mix_norm, w_in, b_igate, b_fgate, mlstm_norm,
              conv_w, conv_b, dt_bias, a_log, d_skip, ssd_norm, ret_norm, w_out,
              ffn2_norm, ffn2_w1, ffn2_w3, ffn2_w2, final_norm):
    layer_params = (ffn1_norm, ffn1_w1, ffn1_w3, ffn1_w2, mix_norm, w_in, b_igate, b_fgate, mlstm_norm,
                    conv_w, conv_b, dt_bias, a_log, d_skip, ssd_norm, ret_norm, w_out,
                    ffn2_norm, ffn2_w1, ffn2_w3, ffn2_w2)
    y_prompt, (p_C, p_n, p_m, p_ssd, p_conv, p_ret) = run_trunk(
        x_prompt, 0,
        jnp.zeros((DEPTH, BATCH, H_A, DH_A, DH_A), F32),
        jnp.zeros((DEPTH, BATCH, H_A, DH_A), F32),
        jnp.zeros((DEPTH, BATCH, H_A), F32),
        jnp.zeros((DEPTH, BATCH, H_B, P_B, N_B), F32),
        jnp.zeros((DEPTH, BATCH, CONV_W - 1, CONV_DIM), x_prompt.dtype),
        jnp.zeros((DEPTH, BATCH, H_C, DH_C, DH_C), F32),
        layer_params, final_norm)
    y_sample, (s_C, s_n, s_m, s_ssd, s_conv, s_ret) = run_trunk(
        x_sample, PAST_LEN, state_mlstm_C, state_mlstm_n, state_mlstm_m, state_ssd, state_conv, state_ret,
        layer_params, final_norm)
    return (y_prompt, y_sample, p_C, p_n, p_m, p_ssd, p_conv, p_ret, s_C, s_n, s_m, s_ssd, s_conv, s_ret)
```

```python
import functools
import math

import jax
import jax.numpy as jnp
from jax import lax
from jax.experimental import pallas as pl
from jax.experimental.pallas import tpu as pltpu

F32 = jnp.float32
BF16 = jnp.bfloat16
HIGHEST = lax.Precision.HIGHEST

D_MODEL = 1024
D_FF = 2816
GROUP_W = 512
H_A = 4
DH = 128
H_B = 8
P_B = 64
N_B = 128
G_B = 2
CONV_W = 4
CONV_DIM = GROUP_W + 2 * G_B * N_B
H_C = 4
CHUNK = 128
PAST_LEN = 16384
GATE_SOFTCAP = 15.0
ROPE_BASE = 10000.0
EPS = 1e-6
QK_SCALE = DH ** -0.5

LANES = 128
SUBLANES = 8

ROWS = 128
TM = 512
FC = 256
VMEM_LIMIT = 56 * 1024 * 1024

GATE_I = 0
GATE_F = 4
GATE_DT = 8


def _dot(a, b):
    return jnp.dot(a, b, preferred_element_type=F32)


def _dot_nt(a, b):
    return lax.dot_general(a, b, (((1,), (1,)), ((), ())), preferred_element_type=F32)


def _dot_tn(a, b):
    return lax.dot_general(a, b, (((0,), (0,)), ((), ())), preferred_element_type=F32)


def _dot_exact(a, b):
    return jnp.dot(a, b, precision=HIGHEST, preferred_element_type=F32)


def _rms(x, g):
    return x * lax.rsqrt(jnp.mean(x * x, axis=-1, keepdims=True) + EPS) * g


def _silu(x):
    return x * jax.nn.sigmoid(x)


def _log1p_exp_neg_abs(x):
    return jnp.log1p(jnp.exp(-jnp.abs(x)))


def _causal_mask(rows, lc):
    r = lax.broadcasted_iota(jnp.int32, (rows, rows), 0)
    c = lax.broadcasted_iota(jnp.int32, (rows, rows), 1)
    m = c <= r
    if lc != rows:
        shift = lc.bit_length() - 1
        m = m & ((r >> shift) == (c >> shift))
    return m, r, c


def _group_last(x, lc):
    rows, w = x.shape
    if lc == rows:
        return jnp.broadcast_to(x[rows - 1:rows, :], (rows, w))
    x3 = x.reshape(rows // lc, lc, w)
    return jnp.broadcast_to(x3[:, lc - 1:lc, :], x3.shape).reshape(rows, w)


def _group_first(x, lc):
    rows, w = x.shape
    if lc == rows:
        return x[0:1, :]
    return x.reshape(rows // lc, lc, w)[:, 0, :]


def _group_sum(x, lc):
    rows, w = x.shape
    if lc == rows:
        return jnp.sum(x, axis=0, keepdims=True)
    return jnp.sum(x.reshape(rows // lc, lc, w), axis=1)


def _group_bcast(x, lc, rows):
    n, w = x.shape
    if n == 1:
        return jnp.broadcast_to(x, (rows, w))
    return jnp.broadcast_to(x[:, None, :], (n, lc, w)).reshape(rows, w)


def _ffn_core(x, g_ref, w1_ref, w3_ref, w2_ref, act_ref):
    h = _rms(x, g_ref[...]).astype(BF16)
    for c in range(D_FF // FC):
        cols = slice(c * FC, (c + 1) * FC)
        a = _dot(h, w1_ref[:, cols])
        b = _dot(h, w3_ref[:, cols])
        act_ref[:, cols] = (_silu(a) * b).astype(BF16)
    return x + 0.5 * _dot(act_ref[...], w2_ref[...])


def _ffn_body(x_ref, g_ref, w1_ref, w3_ref, w2_ref, o_ref, act_ref):
    o_ref[...] = _ffn_core(x_ref[...], g_ref, w1_ref, w3_ref, w2_ref, act_ref)


def _mix_ffn_body(x_ref, oa_ref, ob_ref, oc_ref, wo_ref, g_ref, w1_ref, w3_ref, w2_ref, fin_ref,
                  o_ref, act_ref, *, final):
    x = x_ref[...]
    x = x + (_dot(oa_ref[...], wo_ref[0:GROUP_W, :])
             + _dot(ob_ref[...], wo_ref[GROUP_W:2 * GROUP_W, :])
             + _dot(oc_ref[...], wo_ref[2 * GROUP_W:3 * GROUP_W, :]))
    y = _ffn_core(x, g_ref, w1_ref, w3_ref, w2_ref, act_ref)
    if final:
        y = _rms(y, fin_ref[...])
    o_ref[...] = y


def _inproj_body(x_ref, g_ref, wa_ref, wb_ref, wc_ref, wg_ref, pa_ref, pb_ref, pc_ref, pg_ref):
    h = _rms(x_ref[...], g_ref[...]).astype(BF16)
    pa_ref[...] = _dot(h, wa_ref[...])
    pb_ref[...] = _dot(h, wb_ref[...])
    pc_ref[...] = _dot(h, wc_ref[...])
    pg_ref[...] = _dot(h, wg_ref[...])


def _const_spec(shape):
    nd = len(shape)
    return pl.BlockSpec(shape, lambda *_: (0,) * nd, pipeline_mode=pl.Buffered(1))


def _row_spec(rows, cols):
    return pl.BlockSpec((rows, cols), lambda i: (i, 0))


def _dense_params():
    return pltpu.CompilerParams(dimension_semantics=("arbitrary",), vmem_limit_bytes=VMEM_LIMIT)


def _ffn_call(x, g, w1, w3, w2):
    t = x.shape[0]
    return pl.pallas_call(
        _ffn_body,
        grid=(t // TM,),
        in_specs=[_row_spec(TM, D_MODEL), _const_spec((1, D_MODEL)),
                  _const_spec((D_MODEL, D_FF)), _const_spec((D_MODEL, D_FF)), _const_spec((D_FF, D_MODEL))],
        out_specs=_row_spec(TM, D_MODEL),
        out_shape=jax.ShapeDtypeStruct((t, D_MODEL), F32),
        scratch_shapes=[pltpu.VMEM((TM, D_FF), BF16)],
        compiler_params=_dense_params(),
        name="ffn",
    )(x, g, w1, w3, w2)


def _mix_ffn_call(x, oa, ob, oc, wo, g, w1, w3, w2, fin, final):
    t = x.shape[0]
    return pl.pallas_call(
        functools.partial(_mix_ffn_body, final=final),
        grid=(t // TM,),
        in_specs=[_row_spec(TM, D_MODEL), _row_spec(TM, GROUP_W), _row_spec(TM, GROUP_W), _row_spec(TM, GROUP_W),
                  _const_spec((3 * GROUP_W, D_MODEL)), _const_spec((1, D_MODEL)),
                  _const_spec((D_MODEL, D_FF)), _const_spec((D_MODEL, D_FF)), _const_spec((D_FF, D_MODEL)),
                  _const_spec((1, D_MODEL))],
        out_specs=_row_spec(TM, D_MODEL),
        out_shape=jax.ShapeDtypeStruct((t, D_MODEL), F32),
        scratch_shapes=[pltpu.VMEM((TM, D_FF), BF16)],
        compiler_params=_dense_params(),
        name="mix_ffn",
    )(x, oa, ob, oc, wo, g, w1, w3, w2, fin)


def _inproj_call(x, g, wa, wb, wc, wg):
    t = x.shape[0]
    widths = (wa.shape[1], wb.shape[1], wc.shape[1], wg.shape[1])
    return pl.pallas_call(
        _inproj_body,
        grid=(t // TM,),
        in_specs=[_row_spec(TM, D_MODEL), _const_spec((1, D_MODEL))] + [_const_spec((D_MODEL, w)) for w in widths],
        out_specs=[_row_spec(TM, w) for w in widths],
        out_shape=[jax.ShapeDtypeStruct((t, w), F32) for w in widths],
        compiler_params=_dense_params(),
        name="inproj",
    )(x, g, wa, wb, wc, wg)


def _mlstm_body(*refs, lc, nseq):
    rows = lc * nseq
    if nseq == 1:
        pa_ref, pg_ref, gb_ref, norm_ref, out_ref, c_ref, n_ref, m_ref, qc_scr = refs
        c0_ref, n0_ref = c_ref, n_ref

        @pl.when(pl.program_id(1) == 0)
        def _init():
            c_ref[...] = jnp.zeros_like(c_ref)
            n_ref[...] = jnp.zeros_like(n_ref)
            m_ref[...] = jnp.zeros_like(m_ref)

        mprev = jnp.broadcast_to(m_ref[0], (rows, LANES))
    else:
        (pa_ref, pg_ref, gb_ref, norm_ref, c0_ref, n0_ref, m0_ref,
         out_ref, c_ref, n_ref, m_ref, qc_scr, kw_scr, g_scr) = refs
        mprev = m0_ref[...]

    causal, _, _ = _causal_mask(rows, lc)
    lane = lax.broadcasted_iota(jnp.int32, (rows, LANES), 1)
    fcols = (lane >= GATE_F) & (lane < GATE_F + H_A)

    pre = pg_ref[...] + gb_ref[...]
    cap = GATE_SOFTCAP * jnp.tanh(pre / GATE_SOFTCAP)
    logf = jnp.minimum(cap, 0.0) - _log1p_exp_neg_abs(cap)
    bcum = _dot_exact(jnp.where(causal, 1.0, 0.0), logf)
    inter = bcum + mprev
    u = pltpu.roll(cap, GATE_F - GATE_I, axis=1) - bcum
    ut = u.T
    mt_all = jnp.zeros((rows, LANES), F32)
    for h in range(H_A):
        c = GATE_F + h
        d = jnp.where(causal, bcum[:, c:c + 1] + ut[c:c + 1, :], -jnp.inf)
        mt = jnp.maximum(inter[:, c:c + 1], jnp.max(d, axis=1, keepdims=True))
        mt_all = jnp.where(lane == c, mt, mt_all)
    blast = _group_last(bcum, lc)
    mend = _group_last(mt_all, lc)
    wend = jnp.where(fcols, jnp.exp(blast + u - mend), 0.0)
    gend = jnp.where(fcols, jnp.exp(blast + mprev - mend), 0.0)
    gint = jnp.where(fcols, jnp.exp(inter - mt_all), 0.0)
    emt = jnp.exp(-mt_all)

    def head_cols(group, h):
        return slice(group * GROUP_W + h * DH, group * GROUP_W + (h + 1) * DH)

    if nseq == 1:
        for h in range(H_A):
            c = GATE_F + h
            q = (pa_ref[:, head_cols(0, h)] * QK_SCALE).astype(BF16)
            kw = pa_ref[:, head_cols(1, h)] * wend[:, c:c + 1]
            v = pa_ref[:, head_cols(2, h)].astype(BF16)
            c_old = c_ref[0, h]
            qc_scr[h] = _dot(q, c_old.astype(BF16))
            g1 = gend[0:1, c:c + 1]
            c_ref[0, h] = g1 * c_old + _dot_tn(kw.astype(BF16), v)
        n_rows = [jnp.broadcast_to(n_ref[0, h:h + 1, :], (rows, DH)) for h in range(H_A)]
        for h in range(H_A):
            c = GATE_F + h
            kw = pa_ref[:, head_cols(1, h)] * wend[:, c:c + 1]
            n_ref[0, h:h + 1, :] = gend[0:1, c:c + 1] * n_ref[0, h:h + 1, :] + jnp.sum(kw, axis=0, keepdims=True)
        m_ref[0] = mend[0:1, :]
    else:
        g_scr[...] = gend
        for h in range(H_A):
            c = GATE_F + h
            kw_scr[h] = pa_ref[:, head_cols(1, h)] * wend[:, c:c + 1]

        def seq_body(j, carry):
            r0 = pl.multiple_of(j * lc, lc)
            grow = g_scr[pl.ds(r0, 1), :]
            for h in range(H_A):
                c = GATE_F + h
                qj = (pa_ref[pl.ds(r0, lc), head_cols(0, h)] * QK_SCALE).astype(BF16)
                vj = pa_ref[pl.ds(r0, lc), head_cols(2, h)].astype(BF16)
                kwj = kw_scr[h, pl.ds(r0, lc), :].astype(BF16)
                c_old = c0_ref[j, h]
                qc_scr[h, pl.ds(r0, lc), :] = _dot(qj, c_old.astype(BF16))
                c_ref[j, h] = grow[:, c:c + 1] * c_old + _dot_tn(kwj, vj)
            return carry

        lax.fori_loop(0, nseq, seq_body, 0)
        n_rows = []
        gfirst = _group_first(gend, lc)
        for h in range(H_A):
            c = GATE_F + h
            n_old = n0_ref[:, h, :]
            n_rows.append(_group_bcast(n_old, lc, rows))
            n_ref[:, h, :] = gfirst[:, c:c + 1] * n_old + _group_sum(kw_scr[h], lc)
        m_ref[...] = _group_first(mend, lc)

    for h in range(H_A):
        c = GATE_F + h
        qf = pa_ref[:, head_cols(0, h)] * QK_SCALE
        k = pa_ref[:, head_cols(1, h)].astype(BF16)
        v = pa_ref[:, head_cols(2, h)].astype(BF16)
        og = pa_ref[:, head_cols(3, h)]
        mt = mt_all[:, c:c + 1]
        d = bcum[:, c:c + 1] + ut[c:c + 1, :]
        p = jnp.where(causal, jnp.exp(d - mt), 0.0)
        s = _dot_nt(qf.astype(BF16), k) * p
        g = gint[:, c:c + 1]
        num = _dot(s.astype(BF16), v) + g * qc_scr[h]
        qn = jnp.sum(qf * n_rows[h], axis=1, keepdims=True)
        den = jnp.sum(s, axis=1, keepdims=True) + g * qn
        hh = num / jnp.maximum(jnp.abs(den), emt[:, c:c + 1])
        hn = hh * lax.rsqrt(jnp.mean(hh * hh, axis=1, keepdims=True) + EPS) * norm_ref[:, h * DH:(h + 1) * DH]
        out_ref[:, h * DH:(h + 1) * DH] = (hn * jax.nn.sigmoid(og)).astype(out_ref.dtype)


def _mixer_params(ndims):
    return pltpu.CompilerParams(dimension_semantics=("arbitrary",) * ndims, vmem_limit_bytes=VMEM_LIMIT)


def _mlstm_prompt(pa, pg, gbias, norm, bsz, nchunk):
    t = pa.shape[0]
    row = lambda b, c: (b * nchunk + c, 0)
    const = lambda b, c: (0, 0)
    return pl.pallas_call(
        functools.partial(_mlstm_body, lc=ROWS, nseq=1),
        grid=(bsz, nchunk),
        in_specs=[pl.BlockSpec((ROWS, 4 * GROUP_W), row), pl.BlockSpec((ROWS, LANES), row),
                  pl.BlockSpec((1, LANES), const), pl.BlockSpec((1, GROUP_W), const)],
        out_specs=[pl.BlockSpec((ROWS, GROUP_W), row),
                   pl.BlockSpec((1, H_A, DH, DH), lambda b, c: (b, 0, 0, 0)),
                   pl.BlockSpec((1, H_A, DH), lambda b, c: (b, 0, 0)),
                   pl.BlockSpec((1, 1, LANES), lambda b, c: (b, 0, 0))],
        out_shape=[jax.ShapeDtypeStruct((t, GROUP_W), BF16),
                   jax.ShapeDtypeStruct((bsz, H_A, DH, DH), F32),
                   jax.ShapeDtypeStruct((bsz, H_A, DH), F32),
                   jax.ShapeDtypeStruct((bsz, 1, LANES), F32)],
        scratch_shapes=[pltpu.VMEM((H_A, ROWS, DH), F32)],
        compiler_params=_mixer_params(2),
        name="mlstm_prompt",
    )(pa, pg, gbias, norm)


def _mlstm_sample(pa, pg, gbias, norm, c0, n0, m0rows, lc):
    t = pa.shape[0]
    nseq = ROWS // lc
    bsz = c0.shape[0]
    row = lambda i: (i, 0)
    const = lambda i: (0, 0)
    return pl.pallas_call(
        functools.partial(_mlstm_body, lc=lc, nseq=nseq),
        grid=(t // ROWS,),
        in_specs=[pl.BlockSpec((ROWS, 4 * GROUP_W), row), pl.BlockSpec((ROWS, LANES), row),
                  pl.BlockSpec((1, LANES), const), pl.BlockSpec((1, GROUP_W), const),
                  pl.BlockSpec((nseq, H_A, DH, DH), lambda i: (i, 0, 0, 0)),
                  pl.BlockSpec((nseq, H_A, DH), lambda i: (i, 0, 0)),
                  pl.BlockSpec((ROWS, LANES), row)],
        out_specs=[pl.BlockSpec((ROWS, GROUP_W), row),
                   pl.BlockSpec((nseq, H_A, DH, DH), lambda i: (i, 0, 0, 0)),
                   pl.BlockSpec((nseq, H_A, DH), lambda i: (i, 0, 0)),
                   pl.BlockSpec((nseq, LANES), row)],
        out_shape=[jax.ShapeDtypeStruct((t, GROUP_W), BF16),
                   jax.ShapeDtypeStruct((bsz, H_A, DH, DH), F32),
                   jax.ShapeDtypeStruct((bsz, H_A, DH), F32),
                   jax.ShapeDtypeStruct((bsz, LANES), F32)],
        scratch_shapes=[pltpu.VMEM((H_A, ROWS, DH), F32), pltpu.VMEM((H_A, ROWS, DH), F32),
                        pltpu.VMEM((ROWS, LANES), F32)],
        compiler_params=_mixer_params(1),
        name="mlstm_sample",
    )(pa, pg, gbias, norm, c0, n0, m0rows)


def _ret_body(*refs, lc, nseq):
    rows = lc * nseq
    if nseq == 1:
        pc_ref, cos_ref, sin_ref, norm_ref, out_ref, s_ref, qs_scr = refs

        @pl.when(pl.program_id(1) == 0)
        def _init():
            s_ref[...] = jnp.zeros_like(s_ref)
    else:
        pc_ref, cos_ref, sin_ref, norm_ref, s0_ref, out_ref, s_ref, qs_scr, qr_scr, ke_scr = refs

    causal, r, c = _causal_mask(rows, lc)
    diff = (r - c).astype(F32)
    tin = (lax.broadcasted_iota(jnp.int32, (rows, 1), 0) & (lc - 1)).astype(F32)
    cos = cos_ref[...]
    sin = sin_ref[...]

    def head_cols(group, h):
        return slice(group * GROUP_W + h * DH, group * GROUP_W + (h + 1) * DH)

    def rot(x):
        return x * cos + pltpu.roll(x, DH // 2, axis=1) * sin

    log_gamma = [math.log(1.0 - 2.0 ** (-5.0 - h)) for h in range(H_C)]

    for h in range(H_C):
        lg = log_gamma[h]
        qr = rot(pc_ref[:, head_cols(0, h)])
        ke = rot(pc_ref[:, head_cols(1, h)]) * QK_SCALE * jnp.exp((lc - 1.0 - tin) * lg)
        if nseq == 1:
            v = pc_ref[:, head_cols(2, h)].astype(BF16)
            s_old = s_ref[0, h]
            qs_scr[h] = _dot(qr.astype(BF16), s_old.astype(BF16))
            s_ref[0, h] = math.exp(lc * lg) * s_old + _dot_tn(ke.astype(BF16), v)
        else:
            qr_scr[h] = qr
            ke_scr[h] = ke

    if nseq > 1:
        def seq_body(j, carry):
            r0 = pl.multiple_of(j * lc, lc)
            for h in range(H_C):
                qj = qr_scr[h, pl.ds(r0, lc), :].astype(BF16)
                kj = ke_scr[h, pl.ds(r0, lc), :].astype(BF16)
                vj = pc_ref[pl.ds(r0, lc), head_cols(2, h)].astype(BF16)
                s_old = s0_ref[j, h]
                qs_scr[h, pl.ds(r0, lc), :] = _dot(qj, s_old.astype(BF16))
                s_ref[j, h] = math.exp(lc * log_gamma[h]) * s_old + _dot_tn(kj, vj)
            return carry

        lax.fori_loop(0, nseq, seq_body, 0)

    for h in range(H_C):
        lg = log_gamma[h]
        decay = jnp.where(causal, jnp.exp(diff * lg), 0.0)
        inner = jnp.exp((tin + 1.0) * lg)
        qr = rot(pc_ref[:, head_cols(0, h)]).astype(BF16)
        kr = (rot(pc_ref[:, head_cols(1, h)]) * QK_SCALE).astype(BF16)
        v = pc_ref[:, head_cols(2, h)].astype(BF16)
        gate = pc_ref[:, head_cols(3, h)]
        s = _dot_nt(qr, kr) * decay
        o = _dot(s.astype(BF16), v) + inner * qs_scr[h]
        on = o * lax.rsqrt(jnp.mean(o * o, axis=1, keepdims=True) + EPS) * norm_ref[:, h * DH:(h + 1) * DH]
        out_ref[:, h * DH:(h + 1) * DH] = (on * _silu(gate)).astype(out_ref.dtype)


def _ret_prompt(pc, cos, sin, norm, bsz, nchunk):
    t = pc.shape[0]
    row = lambda b, c: (b * nchunk + c, 0)
    chunk = lambda b, c: (c, 0)
    const = lambda b, c: (0, 0)
    return pl.pallas_call(
        functools.partial(_ret_body, lc=ROWS, nseq=1),
        grid=(bsz, nchunk),
        in_specs=[pl.BlockSpec((ROWS, 4 * GROUP_W), row), pl.BlockSpec((ROWS, DH), chunk),
                  pl.BlockSpec((ROWS, DH), chunk), pl.BlockSpec((1, GROUP_W), const)],
        out_specs=[pl.BlockSpec((ROWS, GROUP_W), row),
                   pl.BlockSpec((1, H_C, DH, DH), lambda b, c: (b, 0, 0, 0))],
        out_shape=[jax.ShapeDtypeStruct((t, GROUP_W), BF16),
                   jax.ShapeDtypeStruct((bsz, H_C, DH, DH), F32)],
        scratch_shapes=[pltpu.VMEM((H_C, ROWS, DH), F32)],
        compiler_params=_mixer_params(2),
        name="ret_prompt",
    )(pc, cos, sin, norm)


def _ret_sample(pc, cos, sin, norm, s0, lc):
    t = pc.shape[0]
    nseq = ROWS // lc
    bsz = s0.shape[0]
    row = lambda i: (i, 0)
    const = lambda i: (0, 0)
    return pl.pallas_call(
        functools.partial(_ret_body, lc=lc, nseq=nseq),
        grid=(t // ROWS,),
        in_specs=[pl.BlockSpec((ROWS, 4 * GROUP_W), row), pl.BlockSpec((ROWS, DH), const),
                  pl.BlockSpec((ROWS, DH), const), pl.BlockSpec((1, GROUP_W), const),
                  pl.BlockSpec((nseq, H_C, DH, DH), lambda i: (i, 0, 0, 0))],
        out_specs=[pl.BlockSpec((ROWS, GROUP_W), row),
                   pl.BlockSpec((nseq, H_C, DH, DH), lambda i: (i, 0, 0, 0))],
        out_shape=[jax.ShapeDtypeStruct((t, GROUP_W), BF16),
                   jax.ShapeDtypeStruct((bsz, H_C, DH, DH), F32)],
        scratch_shapes=[pltpu.VMEM((H_C, ROWS, DH), F32)] * 3,
        compiler_params=_mixer_params(1),
        name="ret_sample",
    )(pc, cos, sin, norm, s0)


HEADS_PER_GROUP = H_B // G_B
PAIR_W = 2 * P_B
GROUP_CH = HEADS_PER_GROUP * P_B


def _ssd_body(*refs, lc, nseq):
    rows = lc * nseq
    if nseq == 1:
        (pb_ref, pg_ref, gb_ref, alog_ref, dskip_ref, cw_ref, cb_ref, norm_ref,
         out_ref, h_ref, buf_ref, ch_scr, tail_scr) = refs
        h0_ref = h_ref

        @pl.when(pl.program_id(1) == 0)
        def _init():
            h_ref[...] = jnp.zeros_like(h_ref)
            tail_scr[...] = jnp.zeros_like(tail_scr)
    else:
        (pb_ref, pg_ref, gb_ref, alog_ref, dskip_ref, cw_ref, cb_ref, norm_ref, h0_ref, hist_ref,
         out_ref, h_ref, buf_ref, ch_scr, xc_scr, xw_scr, g_scr) = refs

    causal, _, _ = _causal_mask(rows, lc)
    lane = lax.broadcasted_iota(jnp.int32, (rows, LANES), 1)
    dcols = (lane >= GATE_DT) & (lane < GATE_DT + H_B)
    low = lax.broadcasted_iota(jnp.int32, (rows, PAIR_W), 1) < P_B

    new = pb_ref[:, GROUP_W:GROUP_W + CONV_DIM]
    tin = lax.broadcasted_iota(jnp.int32, (rows, 1), 0) & (lc - 1)
    acc = new * cw_ref[CONV_W - 1:CONV_W, :]
    for k in range(1, CONV_W):
        rolled = pltpu.roll(new, k, axis=0)
        if nseq == 1:
            tail = pltpu.roll(tail_scr[...], k, axis=0)
            first = jnp.where(tin[0:SUBLANES] >= k, rolled[0:SUBLANES], tail)
            shifted = jnp.concatenate([first, rolled[SUBLANES:]], axis=0)
        else:
            hist = pltpu.roll(hist_ref[...], (rows + k - lc) % rows, axis=0)
            shifted = jnp.where(tin >= k, rolled, hist)
        acc = acc + shifted * cw_ref[CONV_W - 1 - k:CONV_W - k, :]
    xc = _silu(acc + cb_ref[...])
    if nseq == 1:
        tail_scr[...] = new[rows - SUBLANES:rows, :]
        buf_ref[0] = new[rows - (CONV_W - 1):rows, :]
    else:
        buf_ref[...] = new.reshape(nseq, lc, CONV_DIM)[:, lc - (CONV_W - 1):lc, :]

    dpre = pg_ref[...] + gb_ref[...]
    dt = jnp.maximum(dpre, 0.0) + _log1p_exp_neg_abs(dpre)
    adt = dt * (-jnp.exp(alog_ref[...]))
    acum = jnp.where(dcols, _dot_exact(jnp.where(causal, 1.0, 0.0), adt), 0.0)
    at = acum.T
    alast = _group_last(acum, lc)
    exp_a = jnp.exp(acum)
    wx = jnp.exp(alast - acum)
    g_a = jnp.exp(alast)

    def pair_bcast(slab, c0):
        return jnp.where(low, slab[:, c0:c0 + 1], slab[:, c0 + 1:c0 + 2])

    def pair_cols(g, p):
        start = g * GROUP_CH + p * PAIR_W
        return slice(start, start + PAIR_W)

    def b_cols(g):
        return slice(GROUP_W + g * N_B, GROUP_W + (g + 1) * N_B)

    def c_cols(g):
        return slice(GROUP_W + G_B * N_B + g * N_B, GROUP_W + G_B * N_B + (g + 1) * N_B)

    xw_pairs = {}
    for g in range(G_B):
        for p in range(HEADS_PER_GROUP // 2):
            c0 = GATE_DT + g * HEADS_PER_GROUP + 2 * p
            xdt = xc[:, pair_cols(g, p)] * pair_bcast(dt, c0)
            xw_pairs[g, p] = xdt * pair_bcast(wx, c0)
    if nseq == 1:
        for g in range(G_B):
            bg = xc[:, b_cols(g)].astype(BF16)
            cg = xc[:, c_cols(g)].astype(BF16)
            hs = [h_ref[0, g * HEADS_PER_GROUP + r] for r in range(HEADS_PER_GROUP)]
            h_old = jnp.concatenate(hs, axis=0)
            ch_scr[g] = _dot_nt(cg, h_old.astype(BF16))
            xw_g = jnp.concatenate([xw_pairs[g, p] for p in range(HEADS_PER_GROUP // 2)], axis=1)
            upd = _dot_tn(xw_g.astype(BF16), bg)
            for r in range(HEADS_PER_GROUP):
                hd = g * HEADS_PER_GROUP + r
                c = GATE_DT + hd
                h_ref[0, hd] = g_a[0:1, c:c + 1] * hs[r] + upd[r * P_B:(r + 1) * P_B, :]
    else:
        xc_scr[...] = xc
        g_scr[...] = g_a
        for g in range(G_B):
            for p in range(HEADS_PER_GROUP // 2):
                xw_scr[:, pair_cols(g, p)] = xw_pairs[g, p]

        def seq_body(j, carry):
            r0 = pl.multiple_of(j * lc, lc)
            grow = g_scr[pl.ds(r0, 1), :]
            for g in range(G_B):
                bj = xc_scr[pl.ds(r0, lc), b_cols(g)].astype(BF16)
                cj = xc_scr[pl.ds(r0, lc), c_cols(g)].astype(BF16)
                xwj = xw_scr[pl.ds(r0, lc), g * GROUP_CH:(g + 1) * GROUP_CH].astype(BF16)
                hs = [h0_ref[j, g * HEADS_PER_GROUP + r] for r in range(HEADS_PER_GROUP)]
                h_old = jnp.concatenate(hs, axis=0)
                ch_scr[g, pl.ds(r0, lc), :] = _dot_nt(cj, h_old.astype(BF16))
                upd = _dot_tn(xwj, bj)
                for r in range(HEADS_PER_GROUP):
                    hd = g * HEADS_PER_GROUP + r
                    c = GATE_DT + hd
                    h_ref[j, hd] = grow[:, c:c + 1] * hs[r] + upd[r * P_B:(r + 1) * P_B, :]
            return carry

        lax.fori_loop(0, nseq, seq_body, 0)

    for g in range(G_B):
        bg = xc[:, b_cols(g)].astype(BF16)
        cg = xc[:, c_cols(g)].astype(BF16)
        cb = _dot_nt(cg, bg)
        ys = []
        for p in range(HEADS_PER_GROUP // 2):
            c0 = GATE_DT + g * HEADS_PER_GROUP + 2 * p
            xpair = xc[:, pair_cols(g, p)]
            xdt = (xpair * pair_bcast(dt, c0)).astype(BF16)
            halves = []
            for c in (c0, c0 + 1):
                dec = jnp.where(causal, jnp.exp(acum[:, c:c + 1] - at[c:c + 1, :]), 0.0)
                halves.append(_dot((cb * dec).astype(BF16), xdt))
            y = (jnp.where(low, halves[0], halves[1])
                 + pair_bcast(exp_a, c0) * ch_scr[g, :, p * PAIR_W:(p + 1) * PAIR_W])
            ys.append(y + dskip_ref[:, pair_cols(g, p)] * xpair)
        yg = jnp.concatenate(ys, axis=1)
        gcols = slice(g * GROUP_CH, (g + 1) * GROUP_CH)
        yz = yg * _silu(pb_ref[:, gcols])
        yn = yz * lax.rsqrt(jnp.mean(yz * yz, axis=1, keepdims=True) + EPS) * norm_ref[:, gcols]
        out_ref[:, gcols] = yn.astype(out_ref.dtype)


def _ssd_prompt(pb, pg, gbias, alog, dskip, cw, cb, norm, bsz, nchunk):
    t = pb.shape[0]
    row = lambda b, c: (b * nchunk + c, 0)
    const = lambda b, c: (0, 0)
    return pl.pallas_call(
        functools.partial(_ssd_body, lc=ROWS, nseq=1),
        grid=(bsz, nchunk),
        in_specs=[pl.BlockSpec((ROWS, GROUP_W + CONV_DIM), row), pl.BlockSpec((ROWS, LANES), row),
                  pl.BlockSpec((1, LANES), const), pl.BlockSpec((1, LANES), const),
                  pl.BlockSpec((1, GROUP_W), const), pl.BlockSpec((CONV_W, CONV_DIM), const),
                  pl.BlockSpec((1, CONV_DIM), const), pl.BlockSpec((1, GROUP_W), const)],
        out_specs=[pl.BlockSpec((ROWS, GROUP_W), row),
                   pl.BlockSpec((1, H_B, P_B, N_B), lambda b, c: (b, 0, 0, 0)),
                   pl.BlockSpec((1, CONV_W - 1, CONV_DIM), lambda b, c: (b, 0, 0))],
        out_shape=[jax.ShapeDtypeStruct((t, GROUP_W), BF16),
                   jax.ShapeDtypeStruct((bsz, H_B, P_B, N_B), F32),
                   jax.ShapeDtypeStruct((bsz, CONV_W - 1, CONV_DIM), F32)],
        scratch_shapes=[pltpu.VMEM((G_B, ROWS, GROUP_CH), F32), pltpu.VMEM((SUBLANES, CONV_DIM), F32)],
        compiler_params=_mixer_params(2),
        name="ssd_prompt",
    )(pb, pg, gbias, alog, dskip, cw, cb, norm)


def _ssd_sample(pb, pg, gbias, alog, dskip, cw, cb, norm, h0, hist, lc):
    t = pb.shape[0]
    nseq = ROWS // lc
    bsz = h0.shape[0]
    row = lambda i: (i, 0)
    const = lambda i: (0, 0)
    return pl.pallas_call(
        functools.partial(_ssd_body, lc=lc, nseq=nseq),
        grid=(t // ROWS,),
        in_specs=[pl.BlockSpec((ROWS, GROUP_W + CONV_DIM), row), pl.BlockSpec((ROWS, LANES), row),
                  pl.BlockSpec((1, LANES), const), pl.BlockSpec((1, LANES), const),
                  pl.BlockSpec((1, GROUP_W), const), pl.BlockSpec((CONV_W, CONV_DIM), const),
                  pl.BlockSpec((1, CONV_DIM), const), pl.BlockSpec((1, GROUP_W), const),
                  pl.BlockSpec((nseq, H_B, P_B, N_B), lambda i: (i, 0, 0, 0)),
                  pl.BlockSpec((ROWS, CONV_DIM), row)],
        out_specs=[pl.BlockSpec((ROWS, GROUP_W), row),
                   pl.BlockSpec((nseq, H_B, P_B, N_B), lambda i: (i, 0, 0, 0)),
                   pl.BlockSpec((nseq, CONV_W - 1, CONV_DIM), lambda i: (i, 0, 0))],
        out_shape=[jax.ShapeDtypeStruct((t, GROUP_W), BF16),
                   jax.ShapeDtypeStruct((bsz, H_B, P_B, N_B), F32),
                   jax.ShapeDtypeStruct((bsz, CONV_W - 1, CONV_DIM), F32)],
        scratch_shapes=[pltpu.VMEM((G_B, ROWS, GROUP_CH), F32), pltpu.VMEM((ROWS, CONV_DIM), F32),
                        pltpu.VMEM((ROWS, GROUP_W), F32), pltpu.VMEM((ROWS, LANES), F32)],
        compiler_params=_mixer_params(1),
        name="ssd_sample",
    )(pb, pg, gbias, alog, dskip, cw, cb, norm, h0, hist)


def _rope_tables(pos):
    half = DH // 2
    freqs = ROPE_BASE ** (-jnp.arange(half, dtype=F32) / half)
    ang = pos.astype(F32)[:, None] * freqs
    cos, sin = jnp.cos(ang), jnp.sin(ang)
    return jnp.concatenate([cos, cos], axis=-1), jnp.concatenate([-sin, sin], axis=-1)


def _pad_lanes(parts):
    row = jnp.concatenate([p.astype(F32) for p in parts])
    return jnp.pad(row, (0, LANES - row.shape[0]))[None, :]


def kernel(x_prompt, x_sample, state_mlstm_C, state_mlstm_n, state_mlstm_m, state_ssd, state_conv, state_ret,
           ffn1_norm, ffn1_w1, ffn1_w3, ffn1_w2, mix_norm, w_in, b_igate, b_fgate, mlstm_norm,
           conv_w, conv_b, dt_bias, a_log, d_skip, ssd_norm, ret_norm, w_out,
           ffn2_norm, ffn2_w1, ffn2_w3, ffn2_w2, final_norm):
    depth = w_in.shape[0]
    bsz, seq, _ = x_prompt.shape
    dbsz, dseq, _ = x_sample.shape
    assert seq % CHUNK == 0 and ROWS % dseq == 0 and (dbsz * dseq) % ROWS == 0
    assert dseq >= SUBLANES and dseq & (dseq - 1) == 0
    nchunk = seq // CHUNK
    seq_per_block = ROWS // dseq

    xp = x_prompt.reshape(bsz * seq, D_MODEL)
    xs = x_sample.reshape(dbsz * dseq, D_MODEL)

    cos_p, sin_p = _rope_tables(jnp.arange(seq))
    cos_s, sin_s = _rope_tables(PAST_LEN + jnp.arange(dseq))
    cos_s, sin_s = jnp.tile(cos_s, (seq_per_block, 1)), jnp.tile(sin_s, (seq_per_block, 1))

    a0 = 0
    a_gate = a0 + 4 * GROUP_W
    b0 = a_gate + 2 * H_A
    b_dt = b0 + GROUP_W + CONV_DIM
    c0 = b_dt + H_B

    outs_p = [[] for _ in range(6)]
    outs_s = [[] for _ in range(6)]
    for l in range(depth):
        w1a, w3a, w2a = ffn1_w1[l].astype(BF16), ffn1_w3[l].astype(BF16), ffn1_w2[l].astype(BF16)
        w1b, w3b, w2b = ffn2_w1[l].astype(BF16), ffn2_w3[l].astype(BF16), ffn2_w2[l].astype(BF16)
        wi = w_in[l]
        wa = wi[:, a0:a_gate].astype(BF16)
        wb = wi[:, b0:b_dt].astype(BF16)
        wc = wi[:, c0:c0 + 4 * GROUP_W].astype(BF16)
        wg = jnp.concatenate([wi[:, a_gate:b0], wi[:, b_dt:c0]], axis=1)
        wg = jnp.pad(wg, ((0, 0), (0, LANES - wg.shape[1]))).astype(BF16)
        wo = w_out[l].astype(BF16)
        g1, gm, g2 = ffn1_norm[l][None, :], mix_norm[l][None, :], ffn2_norm[l][None, :]
        gbias = _pad_lanes([b_igate[l], b_fgate[l], dt_bias[l]])
        alog = _pad_lanes([jnp.zeros((GATE_DT,), F32), a_log[l]])
        dskip = jnp.repeat(d_skip[l].astype(F32), P_B)[None, :]
        na, nb, nc = mlstm_norm[l][None, :], ssd_norm[l][None, :], ret_norm[l][None, :]
        cw, cb = conv_w[l], conv_b[l][None, :]
        final = l == depth - 1
        fin = final_norm[None, :]

        xp = _ffn_call(xp, g1, w1a, w3a, w2a)
        pa, pb, pc, pg = _inproj_call(xp, gm, wa, wb, wc, wg)
        oa, c1, n1, m1 = _mlstm_prompt(pa, pg, gbias, na, bsz, nchunk)
        ob, h1, buf1 = _ssd_prompt(pb, pg, gbias, alog, dskip, cw, cb, nb, bsz, nchunk)
        oc, s1 = _ret_prompt(pc, cos_p, sin_p, nc, bsz, nchunk)
        xp = _mix_ffn_call(xp, oa, ob, oc, wo, g2, w1b, w3b, w2b, fin, final)
        for acc, v in zip(outs_p, (c1, n1, m1[:, 0, GATE_F:GATE_F + H_A], h1, buf1, s1)):
            acc.append(v)

        m0rows = jnp.pad(jnp.repeat(state_mlstm_m[l].astype(F32), dseq, axis=0),
                         ((0, 0), (GATE_F, LANES - GATE_F - H_A)))
        hist = jnp.pad(state_conv[l].astype(F32), ((0, 0), (dseq - (CONV_W - 1), 0), (0, 0)))
        hist = hist.reshape(dbsz * dseq, CONV_DIM)
        xs = _ffn_call(xs, g1, w1a, w3a, w2a)
        pa, pb, pc, pg = _inproj_call(xs, gm, wa, wb, wc, wg)
        oa, c1, n1, m1 = _mlstm_sample(pa, pg, gbias, na, state_mlstm_C[l], state_mlstm_n[l], m0rows, dseq)
        ob, h1, buf1 = _ssd_sample(pb, pg, gbias, alog, dskip, cw, cb, nb, state_ssd[l], hist, dseq)
        oc, s1 = _ret_sample(pc, cos_s, sin_s, nc, state_ret[l], dseq)
        xs = _mix_ffn_call(xs, oa, ob, oc, wo, g2, w1b, w3b, w2b, fin, final)
        for acc, v in zip(outs_s, (c1, n1, m1[:, GATE_F:GATE_F + H_A], h1, buf1, s1)):
            acc.append(v)

    y_prompt = xp.reshape(bsz, seq, D_MODEL)
    y_sample = xs.reshape(dbsz, dseq, D_MODEL)
    return (y_prompt, y_sample, *[jnp.stack(a) for a in outs_p], *[jnp.stack(a) for a in outs_s])
```

```python
import functools
import math

import jax
import jax.numpy as jnp
from jax import lax
from jax.experimental import pallas as pl
from jax.experimental.pallas import tpu as pltpu

F32 = jnp.float32
BF16 = jnp.bfloat16
HIGHEST = lax.Precision.HIGHEST

D_MODEL = 1024
D_FF = 2816
GROUP_W = 512
H_A = 4
DH = 128
H_B = 8
P_B = 64
N_B = 128
G_B = 2
CONV_W = 4
CONV_DIM = GROUP_W + 2 * G_B * N_B
H_C = 4
CHUNK = 128
PAST_LEN = 16384
GATE_SOFTCAP = 15.0
ROPE_BASE = 10000.0
EPS = 1e-6
QK_SCALE = DH ** -0.5

LANES = 128
SUBLANES = 8

ROWS = 128
SEQ_PER_STEP = 2
TM = 512
FC = 256
VMEM_LIMIT = 56 * 1024 * 1024

GATE_I = 0
GATE_F = 4
GATE_DT = 8


def _dot(a, b):
    return jnp.dot(a, b, preferred_element_type=F32)


def _dot_nt(a, b):
    return lax.dot_general(a, b, (((1,), (1,)), ((), ())), preferred_element_type=F32)


def _dot_tn(a, b):
    return lax.dot_general(a, b, (((0,), (0,)), ((), ())), preferred_element_type=F32)


def _dot_exact(a, b):
    return jnp.dot(a, b, precision=HIGHEST, preferred_element_type=F32)


def _rms(x, g):
    return x * lax.rsqrt(jnp.mean(x * x, axis=-1, keepdims=True) + EPS) * g


def _silu(x):
    return x * jax.nn.sigmoid(x)


def _log1p_exp_neg_abs(x):
    return jnp.log1p(jnp.exp(-jnp.abs(x)))


def _causal_mask(rows, lc):
    r = lax.broadcasted_iota(jnp.int32, (rows, rows), 0)
    c = lax.broadcasted_iota(jnp.int32, (rows, rows), 1)
    m = c <= r
    if lc != rows:
        shift = lc.bit_length() - 1
        m = m & ((r >> shift) == (c >> shift))
    return m, r, c


def _group_last(x, lc):
    rows, w = x.shape
    if lc == rows:
        return jnp.broadcast_to(x[rows - 1:rows, :], (rows, w))
    x3 = x.reshape(rows // lc, lc, w)
    return jnp.broadcast_to(x3[:, lc - 1:lc, :], x3.shape).reshape(rows, w)


def _group_first(x, lc):
    rows, w = x.shape
    if lc == rows:
        return x[0:1, :]
    return x.reshape(rows // lc, lc, w)[:, 0, :]


def _group_sum(x, lc):
    rows, w = x.shape
    if lc == rows:
        return jnp.sum(x, axis=0, keepdims=True)
    return jnp.sum(x.reshape(rows // lc, lc, w), axis=1)


def _group_bcast(x, lc, rows):
    n, w = x.shape
    if n == 1:
        return jnp.broadcast_to(x, (rows, w))
    return jnp.broadcast_to(x[:, None, :], (n, lc, w)).reshape(rows, w)


def _ffn_core(x, g_ref, w1_ref, w3_ref, w2_ref, act_ref):
    h = _rms(x, g_ref[...]).astype(BF16)
    for c in range(D_FF // FC):
        cols = slice(c * FC, (c + 1) * FC)
        a = _dot(h, w1_ref[:, cols])
        b = _dot(h, w3_ref[:, cols])
        act_ref[:, cols] = (_silu(a) * b).astype(BF16)
    return x + 0.5 * _dot(act_ref[...], w2_ref[...])


def _ffn_body(x_ref, g_ref, w1_ref, w3_ref, w2_ref, o_ref, act_ref):
    o_ref[...] = _ffn_core(x_ref[...], g_ref, w1_ref, w3_ref, w2_ref, act_ref)


def _mix_ffn_body(x_ref, oa_ref, ob_ref, oc_ref, wo_ref, g_ref, w1_ref, w3_ref, w2_ref, fin_ref,
                  o_ref, act_ref, *, final):
    x = x_ref[...]
    x = x + (_dot(oa_ref[...], wo_ref[0:GROUP_W, :])
             + _dot(ob_ref[...], wo_ref[GROUP_W:2 * GROUP_W, :])
             + _dot(oc_ref[...], wo_ref[2 * GROUP_W:3 * GROUP_W, :]))
    y = _ffn_core(x, g_ref, w1_ref, w3_ref, w2_ref, act_ref)
    if final:
        y = _rms(y, fin_ref[...])
    o_ref[...] = y


def _inproj_body(x_ref, g_ref, wa_ref, wb_ref, wc_ref, wg_ref, pa_ref, pb_ref, pc_ref, pg_ref):
    h = _rms(x_ref[...], g_ref[...]).astype(BF16)
    pa_ref[...] = _dot(h, wa_ref[...])
    pb_ref[...] = _dot(h, wb_ref[...])
    pc_ref[...] = _dot(h, wc_ref[...])
    pg_ref[...] = _dot(h, wg_ref[...])


def _const_spec(shape):
    nd = len(shape)
    return pl.BlockSpec(shape, lambda *_: (0,) * nd, pipeline_mode=pl.Buffered(1))


def _row_spec(rows, cols):
    return pl.BlockSpec((rows, cols), lambda i: (i, 0))


def _dense_params():
    return pltpu.CompilerParams(dimension_semantics=("arbitrary",), vmem_limit_bytes=VMEM_LIMIT)


def _ffn_call(x, g, w1, w3, w2):
    t = x.shape[0]
    return pl.pallas_call(
        _ffn_body,
        grid=(t // TM,),
        in_specs=[_row_spec(TM, D_MODEL), _const_spec((1, D_MODEL)),
                  _const_spec((D_MODEL, D_FF)), _const_spec((D_MODEL, D_FF)), _const_spec((D_FF, D_MODEL))],
        out_specs=_row_spec(TM, D_MODEL),
        out_shape=jax.ShapeDtypeStruct((t, D_MODEL), F32),
        scratch_shapes=[pltpu.VMEM((TM, D_FF), BF16)],
        compiler_params=_dense_params(),
        name="ffn",
    )(x, g, w1, w3, w2)


def _mix_ffn_call(x, oa, ob, oc, wo, g, w1, w3, w2, fin, final):
    t = x.shape[0]
    return pl.pallas_call(
        functools.partial(_mix_ffn_body, final=final),
        grid=(t // TM,),
        in_specs=[_row_spec(TM, D_MODEL), _row_spec(TM, GROUP_W), _row_spec(TM, GROUP_W), _row_spec(TM, GROUP_W),
                  _const_spec((3 * GROUP_W, D_MODEL)), _const_spec((1, D_MODEL)),
                  _const_spec((D_MODEL, D_FF)), _const_spec((D_MODEL, D_FF)), _const_spec((D_FF, D_MODEL)),
                  _const_spec((1, D_MODEL))],
        out_specs=_row_spec(TM, D_MODEL),
        out_shape=jax.ShapeDtypeStruct((t, D_MODEL), F32),
        scratch_shapes=[pltpu.VMEM((TM, D_FF), BF16)],
        compiler_params=_dense_params(),
        name="mix_ffn",
    )(x, oa, ob, oc, wo, g, w1, w3, w2, fin)


def _inproj_call(x, g, wa, wb, wc, wg):
    t = x.shape[0]
    widths = (wa.shape[1], wb.shape[1], wc.shape[1], wg.shape[1])
    return pl.pallas_call(
        _inproj_body,
        grid=(t // TM,),
        in_specs=[_row_spec(TM, D_MODEL), _const_spec((1, D_MODEL))] + [_const_spec((D_MODEL, w)) for w in widths],
        out_specs=[_row_spec(TM, w) for w in widths],
        out_shape=[jax.ShapeDtypeStruct((t, w), F32) for w in widths],
        compiler_params=_dense_params(),
        name="inproj",
    )(x, g, wa, wb, wc, wg)


def _mlstm_prompt_body(pa_ref, pg_ref, gb_ref, norm_ref, out_ref, c_ref, n_ref, m_ref, qc_scr, *, nb):
    @pl.when(pl.program_id(1) == 0)
    def _init():
        c_ref[...] = jnp.zeros_like(c_ref)
        n_ref[...] = jnp.zeros_like(n_ref)
        m_ref[...] = jnp.zeros_like(m_ref)

    for sq in range(nb):
        _mlstm_block(pa_ref.at[sq], pg_ref.at[sq], gb_ref, norm_ref, out_ref.at[sq],
                     c_ref.at[sq], n_ref.at[sq], m_ref.at[sq], qc_scr.at[sq], lc=ROWS, nseq=1)


def _mlstm_sample_body(*refs, lc, nseq, has_prev):
    _mlstm_block(*refs[:7], *refs[7 + has_prev:], lc=lc, nseq=nseq)


def _mlstm_block(*refs, lc, nseq):
    rows = lc * nseq
    if nseq == 1:
        pa_ref, pg_ref, gb_ref, norm_ref, out_ref, c_ref, n_ref, m_ref, qc_scr = refs
        mprev = jnp.broadcast_to(m_ref[...], (rows, LANES))
    else:
        (pa_ref, pg_ref, gb_ref, norm_ref, c0_ref, n0_ref, m0_ref,
         out_ref, c_ref, n_ref, m_ref, qc_scr, kw_scr, g_scr) = refs
        mprev = m0_ref[...]

    causal, _, _ = _causal_mask(rows, lc)
    lane = lax.broadcasted_iota(jnp.int32, (rows, LANES), 1)
    fcols = (lane >= GATE_F) & (lane < GATE_F + H_A)

    pre = pg_ref[...] + gb_ref[...]
    cap = GATE_SOFTCAP * jnp.tanh(pre / GATE_SOFTCAP)
    logf = jnp.minimum(cap, 0.0) - _log1p_exp_neg_abs(cap)
    bcum = _dot_exact(jnp.where(causal, 1.0, 0.0), logf)
    inter = bcum + mprev
    u = pltpu.roll(cap, GATE_F - GATE_I, axis=1) - bcum
    ut = u.T
    mt_all = jnp.zeros((rows, LANES), F32)
    for h in range(H_A):
        c = GATE_F + h
        d = jnp.where(causal, bcum[:, c:c + 1] + ut[c:c + 1, :], -jnp.inf)
        mt = jnp.maximum(inter[:, c:c + 1], jnp.max(d, axis=1, keepdims=True))
        mt_all = jnp.where(lane == c, mt, mt_all)
    blast = _group_last(bcum, lc)
    mend = _group_last(mt_all, lc)
    wend = jnp.where(fcols, jnp.exp(blast + u - mend), 0.0)
    gend = jnp.where(fcols, jnp.exp(blast + mprev - mend), 0.0)
    gint = jnp.where(fcols, jnp.exp(inter - mt_all), 0.0)
    emt = jnp.exp(-mt_all)
    bm = bcum - mt_all

    def head_cols(group, h):
        return slice(group * GROUP_W + h * DH, group * GROUP_W + (h + 1) * DH)

    if nseq == 1:
        n_rows = []
        for h in range(H_A):
            c = GATE_F + h
            q = (pa_ref[:, head_cols(0, h)] * QK_SCALE).astype(BF16)
            kw = pa_ref[:, head_cols(1, h)] * wend[:, c:c + 1]
            v = pa_ref[:, head_cols(2, h)].astype(BF16)
            c_old = c_ref[h]
            n_old = n_ref[h:h + 1, :]
            n_rows.append(n_old)
            qc_scr[h] = _dot(q, c_old.astype(BF16))
            g1 = gend[0:1, c:c + 1]
            c_ref[h] = g1 * c_old + _dot_tn(kw.astype(BF16), v)
            n_ref[h:h + 1, :] = g1 * n_old + jnp.sum(kw, axis=0, keepdims=True)
        m_ref[...] = mend[0:1, :]
    else:
        g_scr[...] = gend
        for h in range(H_A):
            c = GATE_F + h
            kw_scr[h] = pa_ref[:, head_cols(1, h)] * wend[:, c:c + 1]

        def seq_body(j, carry):
            r0 = pl.multiple_of(j * lc, lc)
            grow = g_scr[pl.ds(r0, 1), :]
            for h in range(H_A):
                c = GATE_F + h
                qj = (pa_ref[pl.ds(r0, lc), head_cols(0, h)] * QK_SCALE).astype(BF16)
                vj = pa_ref[pl.ds(r0, lc), head_cols(2, h)].astype(BF16)
                kwj = kw_scr[h, pl.ds(r0, lc), :].astype(BF16)
                c_old = c0_ref[j, h]
                qc_scr[h, pl.ds(r0, lc), :] = _dot(qj, c_old.astype(BF16))
                c_ref[j, h] = grow[:, c:c + 1] * c_old + _dot_tn(kwj, vj)
            return carry

        lax.fori_loop(0, nseq, seq_body, 0)
        n_rows = []
        gfirst = _group_first(gend, lc)
        for h in range(H_A):
            c = GATE_F + h
            n_old = n0_ref[:, h, :]
            n_rows.append(_group_bcast(n_old, lc, rows))
            n_ref[:, h, :] = gfirst[:, c:c + 1] * n_old + _group_sum(kw_scr[h], lc)
        m_ref[...] = _group_first(mend, lc)

    for h in range(H_A):
        c = GATE_F + h
        qf = pa_ref[:, head_cols(0, h)] * QK_SCALE
        k = pa_ref[:, head_cols(1, h)].astype(BF16)
        v = pa_ref[:, head_cols(2, h)].astype(BF16)
        og = pa_ref[:, head_cols(3, h)]
        p = jnp.where(causal, jnp.exp(bm[:, c:c + 1] + ut[c:c + 1, :]), 0.0)
        s = _dot_nt(qf.astype(BF16), k) * p
        g = gint[:, c:c + 1]
        num = _dot(s.astype(BF16), v) + g * qc_scr[h]
        qn = jnp.sum(qf * n_rows[h], axis=1, keepdims=True)
        den = jnp.sum(s, axis=1, keepdims=True) + g * qn
        hh = num / jnp.maximum(jnp.abs(den), emt[:, c:c + 1])
        hn = hh * lax.rsqrt(jnp.mean(hh * hh, axis=1, keepdims=True) + EPS) * norm_ref[:, h * DH:(h + 1) * DH]
        out_ref[:, h * DH:(h + 1) * DH] = (hn * jax.nn.sigmoid(og)).astype(out_ref.dtype)


def _mixer_params(ndims):
    return pltpu.CompilerParams(dimension_semantics=("arbitrary",) * ndims, vmem_limit_bytes=VMEM_LIMIT)


def _seq_spec(cols):
    return pl.BlockSpec((SEQ_PER_STEP, ROWS, cols), lambda b, c: (b, c, 0))


def _seq_state_spec(shape):
    nd = len(shape)
    return pl.BlockSpec((SEQ_PER_STEP,) + shape, lambda b, c: (b,) + (0,) * nd)


def _mlstm_prompt(pa, pg, gbias, norm, bsz, nchunk):
    seq = nchunk * ROWS
    const = lambda b, c: (0, 0)
    out, c1, n1, m1 = pl.pallas_call(
        functools.partial(_mlstm_prompt_body, nb=SEQ_PER_STEP),
        grid=(bsz // SEQ_PER_STEP, nchunk),
        in_specs=[_seq_spec(4 * GROUP_W), _seq_spec(LANES),
                  pl.BlockSpec((1, LANES), const), pl.BlockSpec((1, GROUP_W), const)],
        out_specs=[_seq_spec(GROUP_W), _seq_state_spec((H_A, DH, DH)), _seq_state_spec((H_A, DH)),
                   _seq_state_spec((1, LANES))],
        out_shape=[jax.ShapeDtypeStruct((bsz, seq, GROUP_W), BF16),
                   jax.ShapeDtypeStruct((bsz, H_A, DH, DH), F32),
                   jax.ShapeDtypeStruct((bsz, H_A, DH), F32),
                   jax.ShapeDtypeStruct((bsz, 1, LANES), F32)],
        scratch_shapes=[pltpu.VMEM((SEQ_PER_STEP, H_A, ROWS, DH), F32)],
        compiler_params=_mixer_params(2),
        name="mlstm_prompt",
    )(pa.reshape(bsz, seq, -1), pg.reshape(bsz, seq, -1), gbias, norm)
    return out.reshape(bsz * seq, GROUP_W), c1, n1, m1


def _layer_state_spec(shape, layer):
    nd = len(shape)
    return pl.BlockSpec((None,) + shape, lambda i: (layer, i) + (0,) * (nd - 1))


def _prev_args(prev):
    if prev is None:
        return [], [], False
    return [prev], [pl.BlockSpec(memory_space=pl.ANY)], True


def _mlstm_sample(pa, pg, gbias, norm, c_all, n_all, m0rows, lc, layer, c_prev):
    t = pa.shape[0]
    nseq = ROWS // lc
    bsz = c_all.shape[1]
    row = lambda i: (i, 0)
    const = lambda i: (0, 0)
    prev_ops, prev_specs, has_prev = _prev_args(c_prev)
    return pl.pallas_call(
        functools.partial(_mlstm_sample_body, lc=lc, nseq=nseq, has_prev=has_prev),
        grid=(t // ROWS,),
        in_specs=[pl.BlockSpec((ROWS, 4 * GROUP_W), row), pl.BlockSpec((ROWS, LANES), row),
                  pl.BlockSpec((1, LANES), const), pl.BlockSpec((1, GROUP_W), const),
                  _layer_state_spec((nseq, H_A, DH, DH), layer),
                  _layer_state_spec((nseq, H_A, DH), layer),
                  pl.BlockSpec((ROWS, LANES), row)] + prev_specs,
        out_specs=[pl.BlockSpec((ROWS, GROUP_W), row),
                   _layer_state_spec((nseq, H_A, DH, DH), layer),
                   pl.BlockSpec((nseq, H_A, DH), lambda i: (i, 0, 0)),
                   pl.BlockSpec((nseq, LANES), row)],
        out_shape=[jax.ShapeDtypeStruct((t, GROUP_W), BF16),
                   jax.ShapeDtypeStruct(c_all.shape, F32),
                   jax.ShapeDtypeStruct((bsz, H_A, DH), F32),
                   jax.ShapeDtypeStruct((bsz, LANES), F32)],
        input_output_aliases={7: 1} if has_prev else {},
        scratch_shapes=[pltpu.VMEM((H_A, ROWS, DH), F32), pltpu.VMEM((H_A, ROWS, DH), F32),
                        pltpu.VMEM((ROWS, LANES), F32)],
        compiler_params=_mixer_params(1),
        name="mlstm_sample",
    )(pa, pg, gbias, norm, c_all, n_all, m0rows, *prev_ops)


def _ret_prompt_body(pc_ref, cos_ref, sin_ref, norm_ref, out_ref, s_ref, *, nb):
    @pl.when(pl.program_id(1) == 0)
    def _init():
        s_ref[...] = jnp.zeros_like(s_ref)

    for sq in range(nb):
        _ret_block(pc_ref.at[sq], cos_ref, sin_ref, norm_ref, out_ref.at[sq], s_ref.at[sq], lc=ROWS, nseq=1)


def _ret_sample_body(*refs, lc, nseq, has_prev):
    _ret_block(*refs[:5], *refs[5 + has_prev:], lc=lc, nseq=nseq)


def _ret_block(*refs, lc, nseq):
    rows = lc * nseq
    if nseq == 1:
        pc_ref, cos_ref, sin_ref, norm_ref, out_ref, s_ref = refs
    else:
        pc_ref, cos_ref, sin_ref, norm_ref, s0_ref, out_ref, s_ref, qs_scr, qr_scr, ke_scr = refs

    causal, r, c = _causal_mask(rows, lc)
    diff = (r - c).astype(F32)
    tin = (lax.broadcasted_iota(jnp.int32, (rows, 1), 0) & (lc - 1)).astype(F32)
    cos = cos_ref[...]
    sin = sin_ref[...]

    def head_cols(group, h):
        return slice(group * GROUP_W + h * DH, group * GROUP_W + (h + 1) * DH)

    def rot(x):
        return x * cos + pltpu.roll(x, DH // 2, axis=1) * sin

    log_gamma = [math.log(1.0 - 2.0 ** (-5.0 - h)) for h in range(H_C)]

    def finish(h, o):
        gate = pc_ref[:, head_cols(3, h)]
        on = o * lax.rsqrt(jnp.mean(o * o, axis=1, keepdims=True) + EPS) * norm_ref[:, h * DH:(h + 1) * DH]
        out_ref[:, h * DH:(h + 1) * DH] = (on * _silu(gate)).astype(out_ref.dtype)

    if nseq == 1:
        for h in range(H_C):
            lg = log_gamma[h]
            decay = jnp.where(causal, jnp.exp(diff * lg), 0.0)
            qr = rot(pc_ref[:, head_cols(0, h)]).astype(BF16)
            kr = rot(pc_ref[:, head_cols(1, h)]) * QK_SCALE
            ke = (kr * jnp.exp((lc - 1.0 - tin) * lg)).astype(BF16)
            v = pc_ref[:, head_cols(2, h)].astype(BF16)
            s_old = s_ref[h]
            s = _dot_nt(qr, kr.astype(BF16)) * decay
            o = _dot(s.astype(BF16), v) + jnp.exp((tin + 1.0) * lg) * _dot(qr, s_old.astype(BF16))
            s_ref[h] = math.exp(lc * lg) * s_old + _dot_tn(ke, v)
            finish(h, o)
        return

    for h in range(H_C):
        lg = log_gamma[h]
        qr_scr[h] = rot(pc_ref[:, head_cols(0, h)])
        ke_scr[h] = rot(pc_ref[:, head_cols(1, h)]) * QK_SCALE * jnp.exp((lc - 1.0 - tin) * lg)

    def seq_body(j, carry):
        r0 = pl.multiple_of(j * lc, lc)
        for h in range(H_C):
            qj = qr_scr[h, pl.ds(r0, lc), :].astype(BF16)
            kj = ke_scr[h, pl.ds(r0, lc), :].astype(BF16)
            vj = pc_ref[pl.ds(r0, lc), head_cols(2, h)].astype(BF16)
            s_old = s0_ref[j, h]
            qs_scr[h, pl.ds(r0, lc), :] = _dot(qj, s_old.astype(BF16))
            s_ref[j, h] = math.exp(lc * log_gamma[h]) * s_old + _dot_tn(kj, vj)
        return carry

    lax.fori_loop(0, nseq, seq_body, 0)

    for h in range(H_C):
        lg = log_gamma[h]
        decay = jnp.where(causal, jnp.exp(diff * lg), 0.0)
        inner = jnp.exp((tin + 1.0) * lg)
        qr = qr_scr[h].astype(BF16)
        kr = (rot(pc_ref[:, head_cols(1, h)]) * QK_SCALE).astype(BF16)
        v = pc_ref[:, head_cols(2, h)].astype(BF16)
        s = _dot_nt(qr, kr) * decay
        finish(h, _dot(s.astype(BF16), v) + inner * qs_scr[h])


def _ret_prompt(pc, cos, sin, norm, bsz, nchunk):
    seq = nchunk * ROWS
    chunk = lambda b, c: (c, 0)
    const = lambda b, c: (0, 0)
    out, s1 = pl.pallas_call(
        functools.partial(_ret_prompt_body, nb=SEQ_PER_STEP),
        grid=(bsz // SEQ_PER_STEP, nchunk),
        in_specs=[_seq_spec(4 * GROUP_W), pl.BlockSpec((ROWS, DH), chunk),
                  pl.BlockSpec((ROWS, DH), chunk), pl.BlockSpec((1, GROUP_W), const)],
        out_specs=[_seq_spec(GROUP_W), _seq_state_spec((H_C, DH, DH))],
        out_shape=[jax.ShapeDtypeStruct((bsz, seq, GROUP_W), BF16),
                   jax.ShapeDtypeStruct((bsz, H_C, DH, DH), F32)],
        compiler_params=_mixer_params(2),
        name="ret_prompt",
    )(pc.reshape(bsz, seq, -1), cos, sin, norm)
    return out.reshape(bsz * seq, GROUP_W), s1


def _ret_sample(pc, cos, sin, norm, s_all, lc, layer, s_prev):
    t = pc.shape[0]
    nseq = ROWS // lc
    row = lambda i: (i, 0)
    const = lambda i: (0, 0)
    prev_ops, prev_specs, has_prev = _prev_args(s_prev)
    return pl.pallas_call(
        functools.partial(_ret_sample_body, lc=lc, nseq=nseq, has_prev=has_prev),
        grid=(t // ROWS,),
        in_specs=[pl.BlockSpec((ROWS, 4 * GROUP_W), row), pl.BlockSpec((ROWS, DH), const),
                  pl.BlockSpec((ROWS, DH), const), pl.BlockSpec((1, GROUP_W), const),
                  _layer_state_spec((nseq, H_C, DH, DH), layer)] + prev_specs,
        out_specs=[pl.BlockSpec((ROWS, GROUP_W), row),
                   _layer_state_spec((nseq, H_C, DH, DH), layer)],
        out_shape=[jax.ShapeDtypeStruct((t, GROUP_W), BF16),
                   jax.ShapeDtypeStruct(s_all.shape, F32)],
        input_output_aliases={5: 1} if has_prev else {},
        scratch_shapes=[pltpu.VMEM((H_C, ROWS, DH), F32)] * 3,
        compiler_params=_mixer_params(1),
        name="ret_sample",
    )(pc, cos, sin, norm, s_all, *prev_ops)


HEADS_PER_GROUP = H_B // G_B
PAIR_W = 2 * P_B
GROUP_CH = HEADS_PER_GROUP * P_B


def _ssd_prompt_body(pb_ref, pg_ref, gb_ref, alog_ref, dskip_ref, cw_ref, cb_ref, norm_ref,
                     out_ref, h_ref, buf_ref, ch_scr, tail_scr, *, nb):
    @pl.when(pl.program_id(1) == 0)
    def _init():
        h_ref[...] = jnp.zeros_like(h_ref)
        tail_scr[...] = jnp.zeros_like(tail_scr)

    for sq in range(nb):
        _ssd_block(pb_ref.at[sq], pg_ref.at[sq], gb_ref, alog_ref, dskip_ref, cw_ref, cb_ref, norm_ref,
                   out_ref.at[sq], h_ref.at[sq], buf_ref.at[sq], ch_scr.at[sq], tail_scr.at[sq], lc=ROWS, nseq=1)


def _ssd_sample_body(*refs, lc, nseq, has_prev):
    _ssd_block(*refs[:10], *refs[10 + has_prev:], lc=lc, nseq=nseq)


def _ssd_block(*refs, lc, nseq):
    rows = lc * nseq
    if nseq == 1:
        (pb_ref, pg_ref, gb_ref, alog_ref, dskip_ref, cw_ref, cb_ref, norm_ref,
         out_ref, h_ref, buf_ref, ch_scr, tail_scr) = refs
    else:
        (pb_ref, pg_ref, gb_ref, alog_ref, dskip_ref, cw_ref, cb_ref, norm_ref, h0_ref, hist_ref,
         out_ref, h_ref, buf_ref, ch_scr, xc_scr, xw_scr, g_scr) = refs

    causal, _, _ = _causal_mask(rows, lc)
    lane = lax.broadcasted_iota(jnp.int32, (rows, LANES), 1)
    dcols = (lane >= GATE_DT) & (lane < GATE_DT + H_B)
    low = lax.broadcasted_iota(jnp.int32, (rows, PAIR_W), 1) < P_B

    new = pb_ref[:, GROUP_W:GROUP_W + CONV_DIM]
    tin = lax.broadcasted_iota(jnp.int32, (rows, 1), 0) & (lc - 1)
    acc = new * cw_ref[CONV_W - 1:CONV_W, :]
    for k in range(1, CONV_W):
        rolled = pltpu.roll(new, k, axis=0)
        if nseq == 1:
            tail = pltpu.roll(tail_scr[...], k, axis=0)
            first = jnp.where(tin[0:SUBLANES] >= k, rolled[0:SUBLANES], tail)
            shifted = jnp.concatenate([first, rolled[SUBLANES:]], axis=0)
        else:
            hist = pltpu.roll(hist_ref[...], (rows + k - lc) % rows, axis=0)
            shifted = jnp.where(tin >= k, rolled, hist)
        acc = acc + shifted * cw_ref[CONV_W - 1 - k:CONV_W - k, :]
    xc = _silu(acc + cb_ref[...])
    if nseq == 1:
        tail_scr[...] = new[rows - SUBLANES:rows, :]
        buf_ref[...] = new[rows - (CONV_W - 1):rows, :]
    else:
        buf_ref[...] = new.reshape(nseq, lc, CONV_DIM)[:, lc - (CONV_W - 1):lc, :]

    dpre = pg_ref[...] + gb_ref[...]
    dt = jnp.maximum(dpre, 0.0) + _log1p_exp_neg_abs(dpre)
    adt = dt * (-jnp.exp(alog_ref[...]))
    acum = jnp.where(dcols, _dot_exact(jnp.where(causal, 1.0, 0.0), adt), 0.0)
    at = acum.T
    alast = _group_last(acum, lc)
    exp_a = jnp.exp(acum)
    wx = jnp.exp(alast - acum)
    g_a = jnp.exp(alast)

    def pair_bcast(slab, c0):
        return jnp.where(low, slab[:, c0:c0 + 1], slab[:, c0 + 1:c0 + 2])

    def pair_cols(g, p):
        start = g * GROUP_CH + p * PAIR_W
        return slice(start, start + PAIR_W)

    def b_cols(g):
        return slice(GROUP_W + g * N_B, GROUP_W + (g + 1) * N_B)

    def c_cols(g):
        return slice(GROUP_W + G_B * N_B + g * N_B, GROUP_W + G_B * N_B + (g + 1) * N_B)

    xw_pairs = {}
    xdt_pairs = {}
    for g in range(G_B):
        for p in range(HEADS_PER_GROUP // 2):
            c0 = GATE_DT + g * HEADS_PER_GROUP + 2 * p
            xdt = xc[:, pair_cols(g, p)] * pair_bcast(dt, c0)
            xdt_pairs[g, p] = xdt.astype(BF16)
            xw_pairs[g, p] = xdt * pair_bcast(wx, c0)
    if nseq == 1:
        for g in range(G_B):
            bg = xc[:, b_cols(g)].astype(BF16)
            cg = xc[:, c_cols(g)].astype(BF16)
            hs = [h_ref[g * HEADS_PER_GROUP + r] for r in range(HEADS_PER_GROUP)]
            h_old = jnp.concatenate(hs, axis=0)
            ch_scr[g] = _dot_nt(cg, h_old.astype(BF16))
            xw_g = jnp.concatenate([xw_pairs[g, p] for p in range(HEADS_PER_GROUP // 2)], axis=1)
            upd = _dot_tn(xw_g.astype(BF16), bg)
            for r in range(HEADS_PER_GROUP):
                hd = g * HEADS_PER_GROUP + r
                c = GATE_DT + hd
                h_ref[hd] = g_a[0:1, c:c + 1] * hs[r] + upd[r * P_B:(r + 1) * P_B, :]
    else:
        xc_scr[...] = xc
        g_scr[...] = g_a
        for g in range(G_B):
            for p in range(HEADS_PER_GROUP // 2):
                xw_scr[:, pair_cols(g, p)] = xw_pairs[g, p]

        def seq_body(j, carry):
            r0 = pl.multiple_of(j * lc, lc)
            grow = g_scr[pl.ds(r0, 1), :]
            for g in range(G_B):
                bj = xc_scr[pl.ds(r0, lc), b_cols(g)].astype(BF16)
                cj = xc_scr[pl.ds(r0, lc), c_cols(g)].astype(BF16)
                xwj = xw_scr[pl.ds(r0, lc), g * GROUP_CH:(g + 1) * GROUP_CH].astype(BF16)
                hs = [h0_ref[j, g * HEADS_PER_GROUP + r] for r in range(HEADS_PER_GROUP)]
                h_old = jnp.concatenate(hs, axis=0)
                ch_scr[g, pl.ds(r0, lc), :] = _dot_nt(cj, h_old.astype(BF16))
                upd = _dot_tn(xwj, bj)
                for r in range(HEADS_PER_GROUP):
                    hd = g * HEADS_PER_GROUP + r
                    c = GATE_DT + hd
                    h_ref[j, hd] = grow[:, c:c + 1] * hs[r] + upd[r * P_B:(r + 1) * P_B, :]
            return carry

        lax.fori_loop(0, nseq, seq_body, 0)

    for g in range(G_B):
        bg = xc[:, b_cols(g)].astype(BF16)
        cg = xc[:, c_cols(g)].astype(BF16)
        cb = _dot_nt(cg, bg)
        ys = []
        for p in range(HEADS_PER_GROUP // 2):
            c0 = GATE_DT + g * HEADS_PER_GROUP + 2 * p
            xpair = xc[:, pair_cols(g, p)]
            xdt = xdt_pairs[g, p]
            halves = []
            for c in (c0, c0 + 1):
                dec = jnp.where(causal, jnp.exp(acum[:, c:c + 1] - at[c:c + 1, :]), 0.0)
                halves.append(_dot((cb * dec).astype(BF16), xdt))
            y = (jnp.where(low, halves[0], halves[1])
                 + pair_bcast(exp_a, c0) * ch_scr[g, :, p * PAIR_W:(p + 1) * PAIR_W])
            ys.append(y + dskip_ref[:, pair_cols(g, p)] * xpair)
        yg = jnp.concatenate(ys, axis=1)
        gcols = slice(g * GROUP_CH, (g + 1) * GROUP_CH)
        yz = yg * _silu(pb_ref[:, gcols])
        yn = yz * lax.rsqrt(jnp.mean(yz * yz, axis=1, keepdims=True) + EPS) * norm_ref[:, gcols]
        out_ref[:, gcols] = yn.astype(out_ref.dtype)


def _ssd_prompt(pb, pg, gbias, alog, dskip, cw, cb, norm, bsz, nchunk):
    seq = nchunk * ROWS
    const = lambda b, c: (0, 0)
    out, h1, buf1 = pl.pallas_call(
        functools.partial(_ssd_prompt_body, nb=SEQ_PER_STEP),
        grid=(bsz // SEQ_PER_STEP, nchunk),
        in_specs=[_seq_spec(GROUP_W + CONV_DIM), _seq_spec(LANES),
                  pl.BlockSpec((1, LANES), const), pl.BlockSpec((1, LANES), const),
                  pl.BlockSpec((1, GROUP_W), const), pl.BlockSpec((CONV_W, CONV_DIM), const),
                  pl.BlockSpec((1, CONV_DIM), const), pl.BlockSpec((1, GROUP_W), const)],
        out_specs=[_seq_spec(GROUP_W), _seq_state_spec((H_B, P_B, N_B)),
                   _seq_state_spec((CONV_W - 1, CONV_DIM))],
        out_shape=[jax.ShapeDtypeStruct((bsz, seq, GROUP_W), BF16),
                   jax.ShapeDtypeStruct((bsz, H_B, P_B, N_B), F32),
                   jax.ShapeDtypeStruct((bsz, CONV_W - 1, CONV_DIM), F32)],
        scratch_shapes=[pltpu.VMEM((SEQ_PER_STEP, G_B, ROWS, GROUP_CH), F32),
                        pltpu.VMEM((SEQ_PER_STEP, SUBLANES, CONV_DIM), F32)],
        compiler_params=_mixer_params(2),
        name="ssd_prompt",
    )(pb.reshape(bsz, seq, -1), pg.reshape(bsz, seq, -1), gbias, alog, dskip, cw, cb, norm)
    return out.reshape(bsz * seq, GROUP_W), h1, buf1


def _ssd_sample(pb, pg, gbias, alog, dskip, cw, cb, norm, h_all, hist, lc, layer, h_prev):
    t = pb.shape[0]
    nseq = ROWS // lc
    bsz = h_all.shape[1]
    row = lambda i: (i, 0)
    const = lambda i: (0, 0)
    prev_ops, prev_specs, has_prev = _prev_args(h_prev)
    return pl.pallas_call(
        functools.partial(_ssd_sample_body, lc=lc, nseq=nseq, has_prev=has_prev),
        grid=(t // ROWS,),
        in_specs=[pl.BlockSpec((ROWS, GROUP_W + CONV_DIM), row), pl.BlockSpec((ROWS, LANES), row),
                  pl.BlockSpec((1, LANES), const), pl.BlockSpec((1, LANES), const),
                  pl.BlockSpec((1, GROUP_W), const), pl.BlockSpec((CONV_W, CONV_DIM), const),
                  pl.BlockSpec((1, CONV_DIM), const), pl.BlockSpec((1, GROUP_W), const),
                  _layer_state_spec((nseq, H_B, P_B, N_B), layer),
                  pl.BlockSpec((ROWS, CONV_DIM), row)] + prev_specs,
        out_specs=[pl.BlockSpec((ROWS, GROUP_W), row),
                   _layer_state_spec((nseq, H_B, P_B, N_B), layer),
                   pl.BlockSpec((nseq, CONV_W - 1, CONV_DIM), lambda i: (i, 0, 0))],
        out_shape=[jax.ShapeDtypeStruct((t, GROUP_W), BF16),
                   jax.ShapeDtypeStruct(h_all.shape, F32),
                   jax.ShapeDtypeStruct((bsz, CONV_W - 1, CONV_DIM), F32)],
        input_output_aliases={10: 1} if has_prev else {},
        scratch_shapes=[pltpu.VMEM((G_B, ROWS, GROUP_CH), F32), pltpu.VMEM((ROWS, CONV_DIM), F32),
                        pltpu.VMEM((ROWS, GROUP_W), F32), pltpu.VMEM((ROWS, LANES), F32)],
        compiler_params=_mixer_params(1),
        name="ssd_sample",
    )(pb, pg, gbias, alog, dskip, cw, cb, norm, h_all, hist, *prev_ops)


def _rope_tables(pos):
    half = DH // 2
    freqs = ROPE_BASE ** (-jnp.arange(half, dtype=F32) / half)
    ang = pos.astype(F32)[:, None] * freqs
    cos, sin = jnp.cos(ang), jnp.sin(ang)
    return jnp.concatenate([cos, cos], axis=-1), jnp.concatenate([-sin, sin], axis=-1)


def _pad_lanes(parts):
    row = jnp.concatenate([p.astype(F32) for p in parts])
    return jnp.pad(row, (0, LANES - row.shape[0]))[None, :]


def kernel(x_prompt, x_sample, state_mlstm_C, state_mlstm_n, state_mlstm_m, state_ssd, state_conv, state_ret,
           ffn1_norm, ffn1_w1, ffn1_w3, ffn1_w2, mix_norm, w_in, b_igate, b_fgate, mlstm_norm,
           conv_w, conv_b, dt_bias, a_log, d_skip, ssd_norm, ret_norm, w_out,
           ffn2_norm, ffn2_w1, ffn2_w3, ffn2_w2, final_norm):
    depth = w_in.shape[0]
    bsz, seq, _ = x_prompt.shape
    dbsz, dseq, _ = x_sample.shape
    assert seq % CHUNK == 0 and ROWS % dseq == 0 and (dbsz * dseq) % ROWS == 0
    assert dseq >= SUBLANES and dseq & (dseq - 1) == 0
    nchunk = seq // CHUNK
    seq_per_block = ROWS // dseq

    xp = x_prompt.reshape(bsz * seq, D_MODEL)
    xs = x_sample.reshape(dbsz * dseq, D_MODEL)

    cos_p, sin_p = _rope_tables(jnp.arange(seq))
    cos_s, sin_s = _rope_tables(PAST_LEN + jnp.arange(dseq))
    cos_s, sin_s = jnp.tile(cos_s, (seq_per_block, 1)), jnp.tile(sin_s, (seq_per_block, 1))

    a0 = 0
    a_gate = a0 + 4 * GROUP_W
    b0 = a_gate + 2 * H_A
    b_dt = b0 + GROUP_W + CONV_DIM
    c0 = b_dt + H_B

    outs_p = [[] for _ in range(6)]
    outs_s = [[] for _ in range(3)]
    c_stack = h_stack = s_stack = None
    for l in range(depth):
        w1a, w3a, w2a = ffn1_w1[l].astype(BF16), ffn1_w3[l].astype(BF16), ffn1_w2[l].astype(BF16)
        w1b, w3b, w2b = ffn2_w1[l].astype(BF16), ffn2_w3[l].astype(BF16), ffn2_w2[l].astype(BF16)
        wi = w_in[l]
        wa = wi[:, a0:a_gate].astype(BF16)
        wb = wi[:, b0:b_dt].astype(BF16)
        wc = wi[:, c0:c0 + 4 * GROUP_W].astype(BF16)
        wg = jnp.concatenate([wi[:, a_gate:b0], wi[:, b_dt:c0]], axis=1)
        wg = jnp.pad(wg, ((0, 0), (0, LANES - wg.shape[1]))).astype(BF16)
        wo = w_out[l].astype(BF16)
        g1, gm, g2 = ffn1_norm[l][None, :], mix_norm[l][None, :], ffn2_norm[l][None, :]
        gbias = _pad_lanes([b_igate[l], b_fgate[l], dt_bias[l]])
        alog = _pad_lanes([jnp.zeros((GATE_DT,), F32), a_log[l]])
        dskip = jnp.repeat(d_skip[l].astype(F32), P_B)[None, :]
        na, nb, nc = mlstm_norm[l][None, :], ssd_norm[l][None, :], ret_norm[l][None, :]
        cw, cb = conv_w[l], conv_b[l][None, :]
        final = l == depth - 1
        fin = final_norm[None, :]

        xp = _ffn_call(xp, g1, w1a, w3a, w2a)
        pa, pb, pc, pg = _inproj_call(xp, gm, wa, wb, wc, wg)
        oa, c1, n1, m1 = _mlstm_prompt(pa, pg, gbias, na, bsz, nchunk)
        ob, h1, buf1 = _ssd_prompt(pb, pg, gbias, alog, dskip, cw, cb, nb, bsz, nchunk)
        oc, s1 = _ret_prompt(pc, cos_p, sin_p, nc, bsz, nchunk)
        xp = _mix_ffn_call(xp, oa, ob, oc, wo, g2, w1b, w3b, w2b, fin, final)
        for acc, v in zip(outs_p, (c1, n1, m1[:, 0, GATE_F:GATE_F + H_A], h1, buf1, s1)):
            acc.append(v)

        m0rows = jnp.pad(jnp.repeat(state_mlstm_m[l].astype(F32), dseq, axis=0),
                         ((0, 0), (GATE_F, LANES - GATE_F - H_A)))
        hist = jnp.pad(state_conv[l].astype(F32), ((0, 0), (dseq - (CONV_W - 1), 0), (0, 0)))
        hist = hist.reshape(dbsz * dseq, CONV_DIM)
        xs = _ffn_call(xs, g1, w1a, w3a, w2a)
        pa, pb, pc, pg = _inproj_call(xs, gm, wa, wb, wc, wg)
        oa, c_stack, n1, m1 = _mlstm_sample(pa, pg, gbias, na, state_mlstm_C, state_mlstm_n, m0rows, dseq,
                                            l, c_stack)
        ob, h_stack, buf1 = _ssd_sample(pb, pg, gbias, alog, dskip, cw, cb, nb, state_ssd, hist, dseq, l, h_stack)
        oc, s_stack = _ret_sample(pc, cos_s, sin_s, nc, state_ret, dseq, l, s_stack)
        xs = _mix_ffn_call(xs, oa, ob, oc, wo, g2, w1b, w3b, w2b, fin, final)
        for acc, v in zip(outs_s, (n1, m1[:, GATE_F:GATE_F + H_A], buf1)):
            acc.append(v)

    y_prompt = xp.reshape(bsz, seq, D_MODEL)
    y_sample = xs.reshape(dbsz, dseq, D_MODEL)
    s_n, s_m, s_buf = [jnp.stack(a) for a in outs_s]
    return (y_prompt, y_sample, *[jnp.stack(a) for a in outs_p], c_stack, s_n, s_m, h_stack, s_buf, s_stack)
```

```python
import functools
import math

import jax
import jax.numpy as jnp
from jax import lax
from jax.experimental import pallas as pl
from jax.experimental.pallas import tpu as pltpu

F32 = jnp.float32
BF16 = jnp.bfloat16
HIGHEST = lax.Precision.HIGHEST

D_MODEL = 1024
D_FF = 2816
GROUP_W = 512
H_A = 4
DH = 128
H_B = 8
P_B = 64
N_B = 128
G_B = 2
CONV_W = 4
CONV_DIM = GROUP_W + 2 * G_B * N_B
H_C = 4
CHUNK = 128
PAST_LEN = 16384
GATE_SOFTCAP = 15.0
ROPE_BASE = 10000.0
EPS = 1e-6
QK_SCALE = DH ** -0.5

LANES = 128
SUBLANES = 8

ROWS = 128
SEQ_PER_STEP = 4
TM = 512
FC = 256
VMEM_LIMIT = 56 * 1024 * 1024

GATE_I = 0
GATE_F = 4
GATE_DT = 8
GATE_ROWS = 16


def _dot(a, b):
    return jnp.dot(a, b, preferred_element_type=F32)


def _dot_nt(a, b):
    return lax.dot_general(a, b, (((1,), (1,)), ((), ())), preferred_element_type=F32)


def _dot_tn(a, b):
    return lax.dot_general(a, b, (((0,), (0,)), ((), ())), preferred_element_type=F32)


def _dot_exact(a, b):
    return jnp.dot(a, b, precision=HIGHEST, preferred_element_type=F32)


def _rms(x, g):
    return x * lax.rsqrt(jnp.mean(x * x, axis=-1, keepdims=True) + EPS) * g


def _silu(x):
    return x * jax.nn.sigmoid(x)


def _log1p_exp_neg_abs(x):
    return jnp.log1p(jnp.exp(-jnp.abs(x)))


def _causal_mask(rows, lc):
    r = lax.broadcasted_iota(jnp.int32, (rows, rows), 0)
    c = lax.broadcasted_iota(jnp.int32, (rows, rows), 1)
    m = c <= r
    if lc != rows:
        shift = lc.bit_length() - 1
        m = m & ((r >> shift) == (c >> shift))
    return m, r, c


def _group_last(x, lc):
    rows, w = x.shape
    x3 = x.reshape(rows // lc, lc, w)
    return jnp.broadcast_to(x3[:, lc - 1:lc, :], x3.shape).reshape(rows, w)


def _group_first(x, lc):
    rows, w = x.shape
    return x.reshape(rows // lc, lc, w)[:, 0, :]


def _group_sum(x, lc):
    rows, w = x.shape
    return jnp.sum(x.reshape(rows // lc, lc, w), axis=1)


def _group_bcast(x, lc, rows):
    n, w = x.shape
    return jnp.broadcast_to(x[:, None, :], (n, lc, w)).reshape(rows, w)


def _ffn_core(x, g_ref, w1_ref, w3_ref, w2_ref, act_ref):
    h = _rms(x, g_ref[...]).astype(BF16)
    for c in range(D_FF // FC):
        cols = slice(c * FC, (c + 1) * FC)
        a = _dot(h, w1_ref[:, cols])
        b = _dot(h, w3_ref[:, cols])
        act_ref[:, cols] = (_silu(a) * b).astype(BF16)
    return x + 0.5 * _dot(act_ref[...], w2_ref[...])


def _ffn_body(x_ref, g_ref, w1_ref, w3_ref, w2_ref, o_ref, act_ref):
    o_ref[...] = _ffn_core(x_ref[...], g_ref, w1_ref, w3_ref, w2_ref, act_ref)


def _mix_ffn_body(x_ref, oa_ref, ob_ref, oc_ref, wo_ref, g_ref, w1_ref, w3_ref, w2_ref, fin_ref,
                  o_ref, act_ref, *, final):
    x = x_ref[...]
    x = x + (_dot(oa_ref[...], wo_ref[0:GROUP_W, :])
             + _dot(ob_ref[...], wo_ref[GROUP_W:2 * GROUP_W, :])
             + _dot(oc_ref[...], wo_ref[2 * GROUP_W:3 * GROUP_W, :]))
    y = _ffn_core(x, g_ref, w1_ref, w3_ref, w2_ref, act_ref)
    if final:
        y = _rms(y, fin_ref[...])
    o_ref[...] = y


def _inproj_body(x_ref, g_ref, wa_ref, wb_ref, wc_ref, wg_ref, pa_ref, pb_ref, pc_ref, pg_ref):
    h = _rms(x_ref[...], g_ref[...]).astype(BF16)
    pa_ref[...] = _dot(h, wa_ref[...])
    pb_ref[...] = _dot(h, wb_ref[...])
    pc_ref[...] = _dot(h, wc_ref[...])
    pg_ref[...] = _dot(h, wg_ref[...])


def _inproj_prompt_body(x_ref, g_ref, wta_ref, wtc_ref, wtg_ref, wb_ref, wk_ref, wg_ref,
                        pta_ref, ptc_ref, gt_ref, pb_ref, pk_ref, pg_ref):
    h = _rms(x_ref[...], g_ref[...]).astype(BF16)
    for w_ref, o_ref in ((wta_ref, pta_ref), (wtc_ref, ptc_ref), (wtg_ref, gt_ref)):
        pt = _dot_nt(w_ref[...], h)
        for j in range(TM // ROWS):
            o_ref[j] = pt[:, j * ROWS:(j + 1) * ROWS]
    pb_ref[...] = _dot(h, wb_ref[...])
    pk_ref[...] = _dot(h, wk_ref[...])
    pg_ref[...] = _dot(h, wg_ref[...])


def _const_spec(shape):
    nd = len(shape)
    return pl.BlockSpec(shape, lambda *_: (0,) * nd, pipeline_mode=pl.Buffered(1))


def _row_spec(rows, cols):
    return pl.BlockSpec((rows, cols), lambda i: (i, 0))


def _dense_params():
    return pltpu.CompilerParams(dimension_semantics=("arbitrary",), vmem_limit_bytes=VMEM_LIMIT)


def _ffn_call(x, g, w1, w3, w2):
    t = x.shape[0]
    return pl.pallas_call(
        _ffn_body,
        grid=(t // TM,),
        in_specs=[_row_spec(TM, D_MODEL), _const_spec((1, D_MODEL)),
                  _const_spec((D_MODEL, D_FF)), _const_spec((D_MODEL, D_FF)), _const_spec((D_FF, D_MODEL))],
        out_specs=_row_spec(TM, D_MODEL),
        out_shape=jax.ShapeDtypeStruct((t, D_MODEL), F32),
        scratch_shapes=[pltpu.VMEM((TM, D_FF), BF16)],
        compiler_params=_dense_params(),
        name="ffn",
    )(x, g, w1, w3, w2)


def _mix_ffn_call(x, oa, ob, oc, wo, g, w1, w3, w2, fin, final):
    t = x.shape[0]
    return pl.pallas_call(
        functools.partial(_mix_ffn_body, final=final),
        grid=(t // TM,),
        in_specs=[_row_spec(TM, D_MODEL), _row_spec(TM, GROUP_W), _row_spec(TM, GROUP_W), _row_spec(TM, GROUP_W),
                  _const_spec((3 * GROUP_W, D_MODEL)), _const_spec((1, D_MODEL)),
                  _const_spec((D_MODEL, D_FF)), _const_spec((D_MODEL, D_FF)), _const_spec((D_FF, D_MODEL)),
                  _const_spec((1, D_MODEL))],
        out_specs=_row_spec(TM, D_MODEL),
        out_shape=jax.ShapeDtypeStruct((t, D_MODEL), F32),
        scratch_shapes=[pltpu.VMEM((TM, D_FF), BF16)],
        compiler_params=_dense_params(),
        name="mix_ffn",
    )(x, oa, ob, oc, wo, g, w1, w3, w2, fin)


def _inproj_call(x, g, wa, wb, wc, wg):
    t = x.shape[0]
    widths = (wa.shape[1], wb.shape[1], wc.shape[1], wg.shape[1])
    return pl.pallas_call(
        _inproj_body,
        grid=(t // TM,),
        in_specs=[_row_spec(TM, D_MODEL), _const_spec((1, D_MODEL))] + [_const_spec((D_MODEL, w)) for w in widths],
        out_specs=[_row_spec(TM, w) for w in widths],
        out_shape=[jax.ShapeDtypeStruct((t, w), F32) for w in widths],
        compiler_params=_dense_params(),
        name="inproj",
    )(x, g, wa, wb, wc, wg)


def _inproj_prompt_call(x, g, wta, wtc, wtg, wb, wk, wg):
    t = x.shape[0]
    t_rows = (wta.shape[0], wtc.shape[0], wtg.shape[0])
    n_cols = (wb.shape[1], wk.shape[1], wg.shape[1])
    chunk_spec = lambda r: pl.BlockSpec((TM // ROWS, r, ROWS), lambda i: (i, 0, 0))
    return pl.pallas_call(
        _inproj_prompt_body,
        grid=(t // TM,),
        in_specs=([_row_spec(TM, D_MODEL), _const_spec((1, D_MODEL))]
                  + [_const_spec((r, D_MODEL)) for r in t_rows] + [_const_spec((D_MODEL, w)) for w in n_cols]),
        out_specs=[chunk_spec(r) for r in t_rows] + [_row_spec(TM, w) for w in n_cols],
        out_shape=([jax.ShapeDtypeStruct((t // ROWS, r, ROWS), F32) for r in t_rows]
                   + [jax.ShapeDtypeStruct((t, w), F32) for w in n_cols]),
        compiler_params=_dense_params(),
        name="inproj_prompt",
    )(x, g, wta, wtc, wtg, wb, wk, wg)


def _source_target_mask():
    src = lax.broadcasted_iota(jnp.int32, (ROWS, ROWS), 0)
    tgt = lax.broadcasted_iota(jnp.int32, (ROWS, ROWS), 1)
    return src <= tgt, src, tgt


def _last_lane(x):
    return jnp.broadcast_to(x[:, LANES - 1:LANES], x.shape)


def _mlstm_prompt_body(pt_ref, k_ref, gt_ref, gb_ref, normb_ref, out_ref, c_ref, n_ref, m_ref, ct_scr, *, nb):
    @pl.when(pl.program_id(1) == 0)
    def _init():
        ct_scr[...] = jnp.zeros_like(ct_scr)
        n_ref[...] = jnp.zeros_like(n_ref)
        m_ref[...] = jnp.zeros_like(m_ref)

    causal_t, _, _ = _source_target_mask()
    gates = [_mlstm_prompt_gates(gt_ref.at[sq], gb_ref, m_ref.at[sq], causal_t) for sq in range(nb)]
    units = [(sq, h) for sq in range(nb) for h in range(H_A)]

    st_raw, cq, qn, c_upd, n_upd, p_all = {}, {}, {}, {}, {}, {}
    for sq, h in units:
        c = GATE_F + h
        gs = gates[sq]
        qt = (pt_ref[sq, h * DH:(h + 1) * DH, :] * QK_SCALE).astype(BF16)
        vt = pt_ref[sq, GROUP_W + h * DH:GROUP_W + (h + 1) * DH, :]
        kf = k_ref[sq, :, h * DH:(h + 1) * DH]
        kb = kf.astype(BF16)
        n8 = jnp.concatenate([n_ref[sq, h:h + 1, :], jnp.zeros((SUBLANES - 1, DH), F32)], axis=0)
        st_raw[sq, h] = _dot(kb, qt)
        cq[sq, h] = _dot(ct_scr[sq, h].astype(BF16), qt)
        qn[sq, h] = _dot(n8.astype(BF16), qt)[0:1, :]
        c_upd[sq, h] = _dot((vt * gs["wend"][c:c + 1, :]).astype(BF16), kb)
        n_upd[sq, h] = _dot_exact(gs["wend"], kf)[c:c + 1, :]
        p_all[sq, h] = jnp.where(causal_t, jnp.exp(gs["ut"][:, c:c + 1] + gs["bm"][c:c + 1, :]), 0.0)

    num, den = {}, {}
    for sq, h in units:
        c = GATE_F + h
        gs = gates[sq]
        st = st_raw[sq, h] * p_all[sq, h]
        vt = pt_ref[sq, GROUP_W + h * DH:GROUP_W + (h + 1) * DH, :].astype(BF16)
        g = gs["gint"][c:c + 1, :]
        num[sq, h] = _dot(vt, st.astype(BF16)) + g * cq[sq, h]
        den[sq, h] = jnp.sum(st, axis=0, keepdims=True) + g * qn[sq, h]
        g1 = gs["gend"][c:c + 1, 0:1]
        ct_scr[sq, h] = g1 * ct_scr[sq, h] + c_upd[sq, h]
        n_ref[sq, h:h + 1, :] = g1 * n_ref[sq, h:h + 1, :] + n_upd[sq, h]

    for sq, h in units:
        c = GATE_F + h
        hs = slice(h * DH, (h + 1) * DH)
        ot = pt_ref[sq, 2 * GROUP_W + h * DH:2 * GROUP_W + (h + 1) * DH, :]
        hh = num[sq, h] * (1.0 / jnp.maximum(jnp.abs(den[sq, h]), gates[sq]["emt"][c:c + 1, :]))
        rs = lax.rsqrt(jnp.mean(hh * hh, axis=0, keepdims=True) + EPS)
        o = hh * rs * normb_ref[hs, :] * jax.nn.sigmoid(ot)
        out_ref[sq, :, hs] = o.T.astype(out_ref.dtype)

    @pl.when(pl.program_id(1) == pl.num_programs(1) - 1)
    def _finish():
        for sq in range(nb):
            for h in range(H_A):
                c_ref[sq, h] = ct_scr[sq, h].T


def _mlstm_prompt_gates(gt_ref, gb_ref, m_ref, causal_t):
    rowid = lax.broadcasted_iota(jnp.int32, (SUBLANES, LANES), 0)
    frows = rowid >= GATE_F

    pre = gt_ref[0:SUBLANES, :] + gb_ref[...]
    cap = GATE_SOFTCAP * jnp.tanh(pre / GATE_SOFTCAP)
    logf = jnp.minimum(cap, 0.0) - _log1p_exp_neg_abs(cap)
    bcum = _dot_exact(logf, jnp.where(causal_t, 1.0, 0.0))
    mprev = m_ref[...]
    inter = bcum + mprev
    u = pltpu.roll(cap, GATE_F - GATE_I, axis=0) - bcum
    ut = jnp.concatenate([u, jnp.zeros((ROWS - SUBLANES, LANES), F32)], axis=0).T
    mt_all = jnp.zeros((SUBLANES, LANES), F32)
    for h in range(H_A):
        c = GATE_F + h
        d = jnp.where(causal_t, ut[:, c:c + 1] + bcum[c:c + 1, :], -jnp.inf)
        mt = jnp.maximum(inter[c:c + 1, :], jnp.max(d, axis=0, keepdims=True))
        mt_all = jnp.where(rowid == c, mt, mt_all)
    blast = _last_lane(bcum)
    mend = _last_lane(mt_all)
    wend = jnp.where(frows, jnp.exp(blast + u - mend), 0.0)
    gend = jnp.where(frows, jnp.exp(blast + mprev - mend), 0.0)
    gint = jnp.where(frows, jnp.exp(inter - mt_all), 0.0)
    emt = jnp.exp(-mt_all)
    bm = bcum - mt_all
    m_ref[...] = jnp.where(frows, mend, 0.0)
    return dict(ut=ut, bm=bm, wend=wend, gend=gend, gint=gint, emt=emt)


def _mlstm_sample_body(*refs, lc, nseq, has_prev):
    _mlstm_block(*refs[:7], *refs[7 + has_prev:], lc=lc, nseq=nseq)


def _mlstm_block(pa_ref, pg_ref, gb_ref, norm_ref, c0_ref, n0_ref, m0_ref,
                 out_ref, c_ref, n_ref, m_ref, qc_scr, kw_scr, g_scr, *, lc, nseq):
    rows = lc * nseq
    mprev = m0_ref[...]
    causal, _, _ = _causal_mask(rows, lc)
    lane = lax.broadcasted_iota(jnp.int32, (rows, LANES), 1)
    fcols = (lane >= GATE_F) & (lane < GATE_F + H_A)

    pre = pg_ref[...] + gb_ref[...]
    cap = GATE_SOFTCAP * jnp.tanh(pre / GATE_SOFTCAP)
    logf = jnp.minimum(cap, 0.0) - _log1p_exp_neg_abs(cap)
    bcum = _dot_exact(jnp.where(causal, 1.0, 0.0), logf)
    inter = bcum + mprev
    u = pltpu.roll(cap, GATE_F - GATE_I, axis=1) - bcum
    ut = u.T
    mt_all = jnp.zeros((rows, LANES), F32)
    for h in range(H_A):
        c = GATE_F + h
        d = jnp.where(causal, bcum[:, c:c + 1] + ut[c:c + 1, :], -jnp.inf)
        mt = jnp.maximum(inter[:, c:c + 1], jnp.max(d, axis=1, keepdims=True))
        mt_all = jnp.where(lane == c, mt, mt_all)
    blast = _group_last(bcum, lc)
    mend = _group_last(mt_all, lc)
    wend = jnp.where(fcols, jnp.exp(blast + u - mend), 0.0)
    gend = jnp.where(fcols, jnp.exp(blast + mprev - mend), 0.0)
    gint = jnp.where(fcols, jnp.exp(inter - mt_all), 0.0)
    emt = jnp.exp(-mt_all)
    bm = bcum - mt_all

    def head_cols(group, h):
        return slice(group * GROUP_W + h * DH, group * GROUP_W + (h + 1) * DH)

    g_scr[...] = gend
    for h in range(H_A):
        c = GATE_F + h
        kw_scr[h] = pa_ref[:, head_cols(1, h)] * wend[:, c:c + 1]

    def seq_body(j, carry):
        r0 = pl.multiple_of(j * lc, lc)
        grow = g_scr[pl.ds(r0, 1), :]
        for h in range(H_A):
            c = GATE_F + h
            qj = (pa_ref[pl.ds(r0, lc), head_cols(0, h)] * QK_SCALE).astype(BF16)
            vj = pa_ref[pl.ds(r0, lc), head_cols(2, h)].astype(BF16)
            kwj = kw_scr[h, pl.ds(r0, lc), :].astype(BF16)
            c_old = c0_ref[j, h]
            qc_scr[h, pl.ds(r0, lc), :] = _dot(qj, c_old.astype(BF16))
            c_ref[j, h] = grow[:, c:c + 1] * c_old + _dot_tn(kwj, vj)
        return carry

    lax.fori_loop(0, nseq, seq_body, 0)
    n_rows = []
    gfirst = _group_first(gend, lc)
    for h in range(H_A):
        c = GATE_F + h
        n_old = n0_ref[:, h, :]
        n_rows.append(_group_bcast(n_old, lc, rows))
        n_ref[:, h, :] = gfirst[:, c:c + 1] * n_old + _group_sum(kw_scr[h], lc)
    m_ref[...] = _group_first(mend, lc)

    scores = []
    for h in range(H_A):
        qb = (pa_ref[:, head_cols(0, h)] * QK_SCALE).astype(BF16)
        scores.append(_dot_nt(qb, pa_ref[:, head_cols(1, h)].astype(BF16)))
    for h in range(H_A):
        c = GATE_F + h
        qf = pa_ref[:, head_cols(0, h)] * QK_SCALE
        v = pa_ref[:, head_cols(2, h)].astype(BF16)
        og = pa_ref[:, head_cols(3, h)]
        p = jnp.where(causal, jnp.exp(bm[:, c:c + 1] + ut[c:c + 1, :]), 0.0)
        s = scores[h] * p
        g = gint[:, c:c + 1]
        num = _dot(s.astype(BF16), v) + g * qc_scr[h]
        qn = jnp.sum(qf * n_rows[h], axis=1, keepdims=True)
        den = jnp.sum(s, axis=1, keepdims=True) + g * qn
        hh = num / jnp.maximum(jnp.abs(den), emt[:, c:c + 1])
        hn = hh * lax.rsqrt(jnp.mean(hh * hh, axis=1, keepdims=True) + EPS) * norm_ref[:, h * DH:(h + 1) * DH]
        out_ref[:, h * DH:(h + 1) * DH] = (hn * jax.nn.sigmoid(og)).astype(out_ref.dtype)


def _mixer_params(ndims):
    return pltpu.CompilerParams(dimension_semantics=("arbitrary",) * ndims, vmem_limit_bytes=VMEM_LIMIT)


def _seq_spec(cols):
    return pl.BlockSpec((SEQ_PER_STEP, ROWS, cols), lambda b, c: (b, c, 0))


def _seq_state_spec(shape):
    nd = len(shape)
    return pl.BlockSpec((SEQ_PER_STEP,) + shape, lambda b, c: (b,) + (0,) * nd)


def _seq_t_spec(rows):
    return pl.BlockSpec((SEQ_PER_STEP, None, rows, ROWS), lambda b, c: (b, c, 0, 0))


def _seq_col_spec(cols, col_block):
    return pl.BlockSpec((SEQ_PER_STEP, ROWS, cols), lambda b, c: (b, c, col_block))


def _mlstm_prompt(pt, pk, gt, gb8, normb, bsz, nchunk):
    seq = nchunk * ROWS
    const = lambda b, c: (0, 0)
    out, c1, n1, m1 = pl.pallas_call(
        functools.partial(_mlstm_prompt_body, nb=SEQ_PER_STEP),
        grid=(bsz // SEQ_PER_STEP, nchunk),
        in_specs=[_seq_t_spec(3 * GROUP_W), _seq_col_spec(GROUP_W, 0), _seq_t_spec(GATE_ROWS),
                  pl.BlockSpec((SUBLANES, LANES), const), pl.BlockSpec((GROUP_W, LANES), const)],
        out_specs=[_seq_spec(GROUP_W), _seq_state_spec((H_A, DH, DH)), _seq_state_spec((H_A, DH)),
                   _seq_state_spec((SUBLANES, LANES))],
        out_shape=[jax.ShapeDtypeStruct((bsz, seq, GROUP_W), BF16),
                   jax.ShapeDtypeStruct((bsz, H_A, DH, DH), F32),
                   jax.ShapeDtypeStruct((bsz, H_A, DH), F32),
                   jax.ShapeDtypeStruct((bsz, SUBLANES, LANES), F32)],
        scratch_shapes=[pltpu.VMEM((SEQ_PER_STEP, H_A, DH, DH), F32)],
        compiler_params=_mixer_params(2),
        name="mlstm_prompt",
    )(pt.reshape(bsz, nchunk, 3 * GROUP_W, ROWS), pk.reshape(bsz, seq, -1),
      gt.reshape(bsz, nchunk, GATE_ROWS, ROWS), gb8, normb)
    return out.reshape(bsz * seq, GROUP_W), c1, n1, m1


def _layer_state_spec(shape, layer):
    nd = len(shape)
    return pl.BlockSpec((None,) + shape, lambda i: (layer, i) + (0,) * (nd - 1))


def _prev_args(prev):
    if prev is None:
        return [], [], False
    return [prev], [pl.BlockSpec(memory_space=pl.ANY)], True


def _mlstm_sample(pa, pg, gbias, norm, c_all, n_all, m0rows, lc, layer, c_prev):
    t = pa.shape[0]
    nseq = ROWS // lc
    bsz = c_all.shape[1]
    row = lambda i: (i, 0)
    const = lambda i: (0, 0)
    prev_ops, prev_specs, has_prev = _prev_args(c_prev)
    return pl.pallas_call(
        functools.partial(_mlstm_sample_body, lc=lc, nseq=nseq, has_prev=has_prev),
        grid=(t // ROWS,),
        in_specs=[pl.BlockSpec((ROWS, 4 * GROUP_W), row), pl.BlockSpec((ROWS, LANES), row),
                  pl.BlockSpec((1, LANES), const), pl.BlockSpec((1, GROUP_W), const),
                  _layer_state_spec((nseq, H_A, DH, DH), layer),
                  _layer_state_spec((nseq, H_A, DH), layer),
                  pl.BlockSpec((ROWS, LANES), row)] + prev_specs,
        out_specs=[pl.BlockSpec((ROWS, GROUP_W), row),
                   _layer_state_spec((nseq, H_A, DH, DH), layer),
                   pl.BlockSpec((nseq, H_A, DH), lambda i: (i, 0, 0)),
                   pl.BlockSpec((nseq, LANES), row)],
        out_shape=[jax.ShapeDtypeStruct((t, GROUP_W), BF16),
                   jax.ShapeDtypeStruct(c_all.shape, F32),
                   jax.ShapeDtypeStruct((bsz, H_A, DH), F32),
                   jax.ShapeDtypeStruct((bsz, LANES), F32)],
        input_output_aliases={7: 1} if has_prev else {},
        scratch_shapes=[pltpu.VMEM((H_A, ROWS, DH), F32), pltpu.VMEM((H_A, ROWS, DH), F32),
                        pltpu.VMEM((ROWS, LANES), F32)],
        compiler_params=_mixer_params(1),
        name="mlstm_sample",
    )(pa, pg, gbias, norm, c_all, n_all, m0rows, *prev_ops)


LOG_GAMMA = [math.log(1.0 - 2.0 ** (-5.0 - h)) for h in range(H_C)]


def _ret_prompt_body(pt_ref, k_ref, cost_ref, sint_ref, cos_ref, sin_ref, dec_ref, normb_ref, out_ref, s_ref,
                     st_scr, *, nb):
    @pl.when(pl.program_id(1) == 0)
    def _init():
        st_scr[...] = jnp.zeros_like(st_scr)

    tin = lax.broadcasted_iota(jnp.int32, (1, ROWS), 1).astype(F32)
    cost, sint = cost_ref[...], sint_ref[...]
    cos, sin = cos_ref[...], sin_ref[...]
    units = [(sq, h) for sq in range(nb) for h in range(H_C)]

    st_raw, qs, s_upd = {}, {}, {}
    for sq, h in units:
        lg = LOG_GAMMA[h]
        qt = pt_ref[sq, h * DH:(h + 1) * DH, :]
        vt = pt_ref[sq, GROUP_W + h * DH:GROUP_W + (h + 1) * DH, :]
        kf = k_ref[sq, :, h * DH:(h + 1) * DH]
        qr = (qt * cost + pltpu.roll(qt, DH // 2, axis=0) * sint).astype(BF16)
        kr = ((kf * cos + pltpu.roll(kf, DH // 2, axis=1) * sin) * QK_SCALE).astype(BF16)
        ve = (vt * jnp.exp((ROWS - 1.0 - tin) * lg)).astype(BF16)
        st_raw[sq, h] = _dot(kr, qr)
        qs[sq, h] = _dot(st_scr[sq, h].astype(BF16), qr)
        s_upd[sq, h] = _dot(ve, kr)

    o_all = {}
    for sq, h in units:
        lg = LOG_GAMMA[h]
        vt = pt_ref[sq, GROUP_W + h * DH:GROUP_W + (h + 1) * DH, :].astype(BF16)
        st = (st_raw[sq, h] * dec_ref[h]).astype(BF16)
        o_all[sq, h] = _dot(vt, st) + jnp.exp((tin + 1.0) * lg) * qs[sq, h]
        st_scr[sq, h] = math.exp(ROWS * lg) * st_scr[sq, h] + s_upd[sq, h]

    for sq, h in units:
        hs = slice(h * DH, (h + 1) * DH)
        gt = pt_ref[sq, 2 * GROUP_W + h * DH:2 * GROUP_W + (h + 1) * DH, :]
        o = o_all[sq, h]
        rs = lax.rsqrt(jnp.mean(o * o, axis=0, keepdims=True) + EPS)
        on = o * rs * normb_ref[hs, :] * _silu(gt)
        out_ref[sq, :, hs] = on.T.astype(out_ref.dtype)

    @pl.when(pl.program_id(1) == pl.num_programs(1) - 1)
    def _finish():
        for sq in range(nb):
            for h in range(H_C):
                s_ref[sq, h] = st_scr[sq, h].T


def _ret_sample_body(*refs, lc, nseq, has_prev):
    _ret_block(*refs[:5], *refs[5 + has_prev:], lc=lc, nseq=nseq)


def _ret_block(pc_ref, cos_ref, sin_ref, norm_ref, s0_ref, out_ref, s_ref, qs_scr, qr_scr, ke_scr, *, lc, nseq):
    rows = lc * nseq
    causal, r, c = _causal_mask(rows, lc)
    diff = (r - c).astype(F32)
    tin = (lax.broadcasted_iota(jnp.int32, (rows, 1), 0) & (lc - 1)).astype(F32)
    cos = cos_ref[...]
    sin = sin_ref[...]

    def head_cols(group, h):
        return slice(group * GROUP_W + h * DH, group * GROUP_W + (h + 1) * DH)

    def rot(x):
        return x * cos + pltpu.roll(x, DH // 2, axis=1) * sin

    for h in range(H_C):
        qr_scr[h] = rot(pc_ref[:, head_cols(0, h)])
        ke_scr[h] = rot(pc_ref[:, head_cols(1, h)]) * QK_SCALE * jnp.exp((lc - 1.0 - tin) * LOG_GAMMA[h])

    def seq_body(j, carry):
        r0 = pl.multiple_of(j * lc, lc)
        for h in range(H_C):
            qj = qr_scr[h, pl.ds(r0, lc), :].astype(BF16)
            kj = ke_scr[h, pl.ds(r0, lc), :].astype(BF16)
            vj = pc_ref[pl.ds(r0, lc), head_cols(2, h)].astype(BF16)
            s_old = s0_ref[j, h]
            qs_scr[h, pl.ds(r0, lc), :] = _dot(qj, s_old.astype(BF16))
            s_ref[j, h] = math.exp(lc * LOG_GAMMA[h]) * s_old + _dot_tn(kj, vj)
        return carry

    lax.fori_loop(0, nseq, seq_body, 0)

    scores = []
    for h in range(H_C):
        kr = (rot(pc_ref[:, head_cols(1, h)]) * QK_SCALE).astype(BF16)
        scores.append(_dot_nt(qr_scr[h].astype(BF16), kr))
    for h in range(H_C):
        lg = LOG_GAMMA[h]
        decay = jnp.where(causal, jnp.exp(diff * lg), 0.0)
        v = pc_ref[:, head_cols(2, h)].astype(BF16)
        gate = pc_ref[:, head_cols(3, h)]
        o = _dot((scores[h] * decay).astype(BF16), v) + jnp.exp((tin + 1.0) * lg) * qs_scr[h]
        on = o * lax.rsqrt(jnp.mean(o * o, axis=1, keepdims=True) + EPS) * norm_ref[:, h * DH:(h + 1) * DH]
        out_ref[:, h * DH:(h + 1) * DH] = (on * _silu(gate)).astype(out_ref.dtype)


def _ret_prompt(pt, pk, cos, sin, normb, bsz, nchunk):
    seq = nchunk * ROWS
    chunk = lambda b, c: (c, 0)
    chunk_t = lambda b, c: (0, c)
    const = lambda b, c: (0, 0)
    idx = jnp.arange(ROWS, dtype=F32)
    diff = idx[None, :] - idx[:, None]
    log_gamma = jnp.asarray(LOG_GAMMA, F32)[:, None, None]
    dec = jnp.where(diff >= 0, jnp.exp(diff * log_gamma), 0.0)
    out, s1 = pl.pallas_call(
        functools.partial(_ret_prompt_body, nb=SEQ_PER_STEP),
        grid=(bsz // SEQ_PER_STEP, nchunk),
        in_specs=[_seq_t_spec(3 * GROUP_W), _seq_col_spec(GROUP_W, 1),
                  pl.BlockSpec((DH, ROWS), chunk_t), pl.BlockSpec((DH, ROWS), chunk_t),
                  pl.BlockSpec((ROWS, DH), chunk), pl.BlockSpec((ROWS, DH), chunk),
                  pl.BlockSpec((H_C, ROWS, ROWS), lambda b, c: (0, 0, 0)),
                  pl.BlockSpec((GROUP_W, LANES), const)],
        out_specs=[_seq_spec(GROUP_W), _seq_state_spec((H_C, DH, DH))],
        out_shape=[jax.ShapeDtypeStruct((bsz, seq, GROUP_W), BF16),
                   jax.ShapeDtypeStruct((bsz, H_C, DH, DH), F32)],
        scratch_shapes=[pltpu.VMEM((SEQ_PER_STEP, H_C, DH, DH), F32)],
        compiler_params=_mixer_params(2),
        name="ret_prompt",
    )(pt.reshape(bsz, nchunk, 3 * GROUP_W, ROWS), pk.reshape(bsz, seq, -1), cos.T, sin.T, cos, sin, dec, normb)
    return out.reshape(bsz * seq, GROUP_W), s1


def _ret_sample(pc, cos, sin, norm, s_all, lc, layer, s_prev):
    t = pc.shape[0]
    nseq = ROWS // lc
    row = lambda i: (i, 0)
    const = lambda i: (0, 0)
    prev_ops, prev_specs, has_prev = _prev_args(s_prev)
    return pl.pallas_call(
        functools.partial(_ret_sample_body, lc=lc, nseq=nseq, has_prev=has_prev),
        grid=(t // ROWS,),
        in_specs=[pl.BlockSpec((ROWS, 4 * GROUP_W), row), pl.BlockSpec((ROWS, DH), const),
                  pl.BlockSpec((ROWS, DH), const), pl.BlockSpec((1, GROUP_W), const),
                  _layer_state_spec((nseq, H_C, DH, DH), layer)] + prev_specs,
        out_specs=[pl.BlockSpec((ROWS, GROUP_W), row),
                   _layer_state_spec((nseq, H_C, DH, DH), layer)],
        out_shape=[jax.ShapeDtypeStruct((t, GROUP_W), BF16),
                   jax.ShapeDtypeStruct(s_all.shape, F32)],
        input_output_aliases={5: 1} if has_prev else {},
        scratch_shapes=[pltpu.VMEM((H_C, ROWS, DH), F32)] * 3,
        compiler_params=_mixer_params(1),
        name="ret_sample",
    )(pc, cos, sin, norm, s_all, *prev_ops)


HEADS_PER_GROUP = H_B // G_B
PAIR_W = 2 * P_B
GROUP_CH = HEADS_PER_GROUP * P_B


def _ssd_prompt_body(pb_ref, pg_ref, gb_ref, alog_ref, dskip_ref, cw_ref, cb_ref, norm_ref,
                     out_ref, h_ref, buf_ref, xc_scr, tail_scr, *, nb):
    @pl.when(pl.program_id(1) == 0)
    def _init():
        h_ref[...] = jnp.zeros_like(h_ref)
        tail_scr[...] = jnp.zeros_like(tail_scr)

    rows = ROWS
    causal, _, _ = _causal_mask(rows, rows)
    lane = lax.broadcasted_iota(jnp.int32, (rows, LANES), 1)
    dcols = (lane >= GATE_DT) & (lane < GATE_DT + H_B)
    low = lax.broadcasted_iota(jnp.int32, (rows, PAIR_W), 1) < P_B
    tin8 = lax.broadcasted_iota(jnp.int32, (SUBLANES, 1), 0)
    pairs_per_group = HEADS_PER_GROUP // 2

    def pair_bcast(slab, c0):
        return jnp.where(low, slab[:, c0:c0 + 1], slab[:, c0 + 1:c0 + 2])

    def pair_cols(g, p):
        start = g * GROUP_CH + p * PAIR_W
        return slice(start, start + PAIR_W)

    def b_cols(g):
        return slice(GROUP_W + g * N_B, GROUP_W + (g + 1) * N_B)

    def c_cols(g):
        return slice(GROUP_W + G_B * N_B + g * N_B, GROUP_W + G_B * N_B + (g + 1) * N_B)

    gates = []
    for sq in range(nb):
        new = pb_ref[sq, :, GROUP_W:GROUP_W + CONV_DIM]
        acc = new * cw_ref[CONV_W - 1:CONV_W, :]
        for k in range(1, CONV_W):
            rolled = pltpu.roll(new, k, axis=0)
            tail = pltpu.roll(tail_scr[sq], k, axis=0)
            first = jnp.where(tin8 >= k, rolled[0:SUBLANES], tail)
            shifted = jnp.concatenate([first, rolled[SUBLANES:]], axis=0)
            acc = acc + shifted * cw_ref[CONV_W - 1 - k:CONV_W - k, :]
        xc_scr[sq] = _silu(acc + cb_ref[...])
        tail_scr[sq] = new[rows - SUBLANES:rows, :]
        buf_ref[sq] = new[rows - (CONV_W - 1):rows, :]

        dpre = pg_ref[sq] + gb_ref[...]
        dt = jnp.maximum(dpre, 0.0) + _log1p_exp_neg_abs(dpre)
        adt = dt * (-jnp.exp(alog_ref[...]))
        acum = jnp.where(dcols, _dot_exact(jnp.where(causal, 1.0, 0.0), adt), 0.0)
        alast = jnp.broadcast_to(acum[rows - 1:rows, :], (rows, LANES))
        gates.append(dict(dt=dt, acum=acum, at=acum.T, exp_a=jnp.exp(acum), wx=jnp.exp(alast - acum),
                          g_a=jnp.exp(alast)))

    groups = [(sq, g) for sq in range(nb) for g in range(G_B)]
    cbs, chs, upds, xdts = {}, {}, {}, {}
    for sq, g in groups:
        gs = gates[sq]
        bg = xc_scr[sq, :, b_cols(g)].astype(BF16)
        cg = xc_scr[sq, :, c_cols(g)].astype(BF16)
        xws = []
        for p in range(pairs_per_group):
            c0 = GATE_DT + g * HEADS_PER_GROUP + 2 * p
            xdt = xc_scr[sq, :, pair_cols(g, p)] * pair_bcast(gs["dt"], c0)
            xdts[sq, g, p] = xdt.astype(BF16)
            xws.append((xdt * pair_bcast(gs["wx"], c0)).astype(BF16))
        h_old = jnp.concatenate([h_ref[sq, g * HEADS_PER_GROUP + r] for r in range(HEADS_PER_GROUP)], axis=0)
        cbs[sq, g] = _dot_nt(cg, bg)
        chs[sq, g] = _dot_nt(cg, h_old.astype(BF16))
        upds[sq, g] = _dot_tn(jnp.concatenate(xws, axis=1), bg)

    halves = {}
    for sq, g in groups:
        gs = gates[sq]
        for r in range(HEADS_PER_GROUP):
            hd = g * HEADS_PER_GROUP + r
            c = GATE_DT + hd
            dec = jnp.where(causal, jnp.exp(gs["acum"][:, c:c + 1] - gs["at"][c:c + 1, :]), 0.0)
            halves[sq, hd] = _dot((cbs[sq, g] * dec).astype(BF16), xdts[sq, g, r // 2])
            h_ref[sq, hd] = gs["g_a"][0:1, c:c + 1] * h_ref[sq, hd] + upds[sq, g][r * P_B:(r + 1) * P_B, :]

    for sq, g in groups:
        gs = gates[sq]
        ys = []
        for p in range(pairs_per_group):
            hd0 = g * HEADS_PER_GROUP + 2 * p
            c0 = GATE_DT + hd0
            y = (jnp.where(low, halves[sq, hd0], halves[sq, hd0 + 1])
                 + pair_bcast(gs["exp_a"], c0) * chs[sq, g][:, p * PAIR_W:(p + 1) * PAIR_W])
            ys.append(y + dskip_ref[:, pair_cols(g, p)] * xc_scr[sq, :, pair_cols(g, p)])
        gcols = slice(g * GROUP_CH, (g + 1) * GROUP_CH)
        yz = jnp.concatenate(ys, axis=1) * _silu(pb_ref[sq, :, gcols])
        yn = yz * lax.rsqrt(jnp.mean(yz * yz, axis=1, keepdims=True) + EPS) * norm_ref[:, gcols]
        out_ref[sq, :, gcols] = yn.astype(out_ref.dtype)


def _ssd_sample_body(*refs, lc, nseq, has_prev):
    _ssd_block(*refs[:10], *refs[10 + has_prev:], lc=lc, nseq=nseq)


def _ssd_block(pb_ref, pg_ref, gb_ref, alog_ref, dskip_ref, cw_ref, cb_ref, norm_ref, h0_ref, hist_ref,
               out_ref, h_ref, buf_ref, ch_scr, xc_scr, xw_scr, g_scr, *, lc, nseq):
    rows = lc * nseq
    causal, _, _ = _causal_mask(rows, lc)
    lane = lax.broadcasted_iota(jnp.int32, (rows, LANES), 1)
    dcols = (lane >= GATE_DT) & (lane < GATE_DT + H_B)
    low = lax.broadcasted_iota(jnp.int32, (rows, PAIR_W), 1) < P_B

    new = pb_ref[:, GROUP_W:GROUP_W + CONV_DIM]
    tin = lax.broadcasted_iota(jnp.int32, (rows, 1), 0) & (lc - 1)
    acc = new * cw_ref[CONV_W - 1:CONV_W, :]
    for k in range(1, CONV_W):
        rolled = pltpu.roll(new, k, axis=0)
        hist = pltpu.roll(hist_ref[...], (rows + k - lc) % rows, axis=0)
        acc = acc + jnp.where(tin >= k, rolled, hist) * cw_ref[CONV_W - 1 - k:CONV_W - k, :]
    xc = _silu(acc + cb_ref[...])
    buf_ref[...] = new.reshape(nseq, lc, CONV_DIM)[:, lc - (CONV_W - 1):lc, :]

    dpre = pg_ref[...] + gb_ref[...]
    dt = jnp.maximum(dpre, 0.0) + _log1p_exp_neg_abs(dpre)
    adt = dt * (-jnp.exp(alog_ref[...]))
    acum = jnp.where(dcols, _dot_exact(jnp.where(causal, 1.0, 0.0), adt), 0.0)
    at = acum.T
    alast = _group_last(acum, lc)
    exp_a = jnp.exp(acum)
    wx = jnp.exp(alast - acum)
    g_a = jnp.exp(alast)

    def pair_bcast(slab, c0):
        return jnp.where(low, slab[:, c0:c0 + 1], slab[:, c0 + 1:c0 + 2])

    def pair_cols(g, p):
        start = g * GROUP_CH + p * PAIR_W
        return slice(start, start + PAIR_W)

    def b_cols(g):
        return slice(GROUP_W + g * N_B, GROUP_W + (g + 1) * N_B)

    def c_cols(g):
        return slice(GROUP_W + G_B * N_B + g * N_B, GROUP_W + G_B * N_B + (g + 1) * N_B)

    xw_pairs = {}
    xdt_pairs = {}
    for g in range(G_B):
        for p in range(HEADS_PER_GROUP // 2):
            c0 = GATE_DT + g * HEADS_PER_GROUP + 2 * p
            xdt = xc[:, pair_cols(g, p)] * pair_bcast(dt, c0)
            xdt_pairs[g, p] = xdt.astype(BF16)
            xw_pairs[g, p] = xdt * pair_bcast(wx, c0)
    xc_scr[...] = xc
    g_scr[...] = g_a
    for g in range(G_B):
        for p in range(HEADS_PER_GROUP // 2):
            xw_scr[:, pair_cols(g, p)] = xw_pairs[g, p]

    def seq_body(j, carry):
        r0 = pl.multiple_of(j * lc, lc)
        grow = g_scr[pl.ds(r0, 1), :]
        for g in range(G_B):
            bj = xc_scr[pl.ds(r0, lc), b_cols(g)].astype(BF16)
            cj = xc_scr[pl.ds(r0, lc), c_cols(g)].astype(BF16)
            xwj = xw_scr[pl.ds(r0, lc), g * GROUP_CH:(g + 1) * GROUP_CH].astype(BF16)
            hs = [h0_ref[j, g * HEADS_PER_GROUP + r] for r in range(HEADS_PER_GROUP)]
            h_old = jnp.concatenate(hs, axis=0)
            ch_scr[g, pl.ds(r0, lc), :] = _dot_nt(cj, h_old.astype(BF16))
            upd = _dot_tn(xwj, bj)
            for r in range(HEADS_PER_GROUP):
                hd = g * HEADS_PER_GROUP + r
                c = GATE_DT + hd
                h_ref[j, hd] = grow[:, c:c + 1] * hs[r] + upd[r * P_B:(r + 1) * P_B, :]
        return carry

    lax.fori_loop(0, nseq, seq_body, 0)

    cbs = [_dot_nt(xc[:, c_cols(g)].astype(BF16), xc[:, b_cols(g)].astype(BF16)) for g in range(G_B)]
    for g in range(G_B):
        cb = cbs[g]
        ys = []
        for p in range(HEADS_PER_GROUP // 2):
            c0 = GATE_DT + g * HEADS_PER_GROUP + 2 * p
            xpair = xc[:, pair_cols(g, p)]
            xdt = xdt_pairs[g, p]
            halves = []
            for c in (c0, c0 + 1):
                dec = jnp.where(causal, jnp.exp(acum[:, c:c + 1] - at[c:c + 1, :]), 0.0)
                halves.append(_dot((cb * dec).astype(BF16), xdt))
            y = (jnp.where(low, halves[0], halves[1])
                 + pair_bcast(exp_a, c0) * ch_scr[g, :, p * PAIR_W:(p + 1) * PAIR_W])
            ys.append(y + dskip_ref[:, pair_cols(g, p)] * xpair)
        yg = jnp.concatenate(ys, axis=1)
        gcols = slice(g * GROUP_CH, (g + 1) * GROUP_CH)
        yz = yg * _silu(pb_ref[:, gcols])
        yn = yz * lax.rsqrt(jnp.mean(yz * yz, axis=1, keepdims=True) + EPS) * norm_ref[:, gcols]
        out_ref[:, gcols] = yn.astype(out_ref.dtype)


def _ssd_prompt(pb, pg, gbias, alog, dskip, cw, cb, norm, bsz, nchunk):
    seq = nchunk * ROWS
    const = lambda b, c: (0, 0)
    out, h1, buf1 = pl.pallas_call(
        functools.partial(_ssd_prompt_body, nb=SEQ_PER_STEP),
        grid=(bsz // SEQ_PER_STEP, nchunk),
        in_specs=[_seq_spec(GROUP_W + CONV_DIM), _seq_spec(LANES),
                  pl.BlockSpec((1, LANES), const), pl.BlockSpec((1, LANES), const),
                  pl.BlockSpec((1, GROUP_W), const), pl.BlockSpec((CONV_W, CONV_DIM), const),
                  pl.BlockSpec((1, CONV_DIM), const), pl.BlockSpec((1, GROUP_W), const)],
        out_specs=[_seq_spec(GROUP_W), _seq_state_spec((H_B, P_B, N_B)),
                   _seq_state_spec((CONV_W - 1, CONV_DIM))],
        out_shape=[jax.ShapeDtypeStruct((bsz, seq, GROUP_W), BF16),
                   jax.ShapeDtypeStruct((bsz, H_B, P_B, N_B), F32),
                   jax.ShapeDtypeStruct((bsz, CONV_W - 1, CONV_DIM), F32)],
        scratch_shapes=[pltpu.VMEM((SEQ_PER_STEP, ROWS, CONV_DIM), F32),
                        pltpu.VMEM((SEQ_PER_STEP, SUBLANES, CONV_DIM), F32)],
        compiler_params=_mixer_params(2),
        name="ssd_prompt",
    )(pb.reshape(bsz, seq, -1), pg.reshape(bsz, seq, -1), gbias, alog, dskip, cw, cb, norm)
    return out.reshape(bsz * seq, GROUP_W), h1, buf1


def _ssd_sample(pb, pg, gbias, alog, dskip, cw, cb, norm, h_all, hist, lc, layer, h_prev):
    t = pb.shape[0]
    nseq = ROWS // lc
    bsz = h_all.shape[1]
    row = lambda i: (i, 0)
    const = lambda i: (0, 0)
    prev_ops, prev_specs, has_prev = _prev_args(h_prev)
    return pl.pallas_call(
        functools.partial(_ssd_sample_body, lc=lc, nseq=nseq, has_prev=has_prev),
        grid=(t // ROWS,),
        in_specs=[pl.BlockSpec((ROWS, GROUP_W + CONV_DIM), row), pl.BlockSpec((ROWS, LANES), row),
                  pl.BlockSpec((1, LANES), const), pl.BlockSpec((1, LANES), const),
                  pl.BlockSpec((1, GROUP_W), const), pl.BlockSpec((CONV_W, CONV_DIM), const),
                  pl.BlockSpec((1, CONV_DIM), const), pl.BlockSpec((1, GROUP_W), const),
                  _layer_state_spec((nseq, H_B, P_B, N_B), layer),
                  pl.BlockSpec((ROWS, CONV_DIM), row)] + prev_specs,
        out_specs=[pl.BlockSpec((ROWS, GROUP_W), row),
                   _layer_state_spec((nseq, H_B, P_B, N_B), layer),
                   pl.BlockSpec((nseq, CONV_W - 1, CONV_DIM), lambda i: (i, 0, 0))],
        out_shape=[jax.ShapeDtypeStruct((t, GROUP_W), BF16),
                   jax.ShapeDtypeStruct(h_all.shape, F32),
                   jax.ShapeDtypeStruct((bsz, CONV_W - 1, CONV_DIM), F32)],
        input_output_aliases={10: 1} if has_prev else {},
        scratch_shapes=[pltpu.VMEM((G_B, ROWS, GROUP_CH), F32), pltpu.VMEM((ROWS, CONV_DIM), F32),
                        pltpu.VMEM((ROWS, GROUP_W), F32), pltpu.VMEM((ROWS, LANES), F32)],
        compiler_params=_mixer_params(1),
        name="ssd_sample",
    )(pb, pg, gbias, alog, dskip, cw, cb, norm, h_all, hist, *prev_ops)


def _rope_tables(pos):
    half = DH // 2
    freqs = ROPE_BASE ** (-jnp.arange(half, dtype=F32) / half)
    ang = pos.astype(F32)[:, None] * freqs
    cos, sin = jnp.cos(ang), jnp.sin(ang)
    return jnp.concatenate([cos, cos], axis=-1), jnp.concatenate([-sin, sin], axis=-1)


def _pad_lanes(parts):
    row = jnp.concatenate([p.astype(F32) for p in parts])
    return jnp.pad(row, (0, LANES - row.shape[0]))[None, :]


def kernel(x_prompt, x_sample, state_mlstm_C, state_mlstm_n, state_mlstm_m, state_ssd, state_conv, state_ret,
           ffn1_norm, ffn1_w1, ffn1_w3, ffn1_w2, mix_norm, w_in, b_igate, b_fgate, mlstm_norm,
           conv_w, conv_b, dt_bias, a_log, d_skip, ssd_norm, ret_norm, w_out,
           ffn2_norm, ffn2_w1, ffn2_w3, ffn2_w2, final_norm):
    depth = w_in.shape[0]
    bsz, seq, _ = x_prompt.shape
    dbsz, dseq, _ = x_sample.shape
    assert seq % CHUNK == 0 and ROWS % dseq == 0 and (dbsz * dseq) % ROWS == 0
    assert dseq >= SUBLANES and dseq & (dseq - 1) == 0
    nchunk = seq // CHUNK
    seq_per_block = ROWS // dseq

    xp = x_prompt.reshape(bsz * seq, D_MODEL)
    xs = x_sample.reshape(dbsz * dseq, D_MODEL)

    cos_p, sin_p = _rope_tables(jnp.arange(seq))
    cos_s, sin_s = _rope_tables(PAST_LEN + jnp.arange(dseq))
    cos_s, sin_s = jnp.tile(cos_s, (seq_per_block, 1)), jnp.tile(sin_s, (seq_per_block, 1))

    a0 = 0
    a_gate = a0 + 4 * GROUP_W
    b0 = a_gate + 2 * H_A
    b_dt = b0 + GROUP_W + CONV_DIM
    c0 = b_dt + H_B

    outs_p = [[] for _ in range(6)]
    outs_s = [[] for _ in range(3)]
    c_stack = h_stack = s_stack = None
    for l in range(depth):
        w1a, w3a, w2a = ffn1_w1[l].astype(BF16), ffn1_w3[l].astype(BF16), ffn1_w2[l].astype(BF16)
        w1b, w3b, w2b = ffn2_w1[l].astype(BF16), ffn2_w3[l].astype(BF16), ffn2_w2[l].astype(BF16)
        wi = w_in[l]
        wa = wi[:, a0:a_gate].astype(BF16)
        wb = wi[:, b0:b_dt].astype(BF16)
        wc = wi[:, c0:c0 + 4 * GROUP_W].astype(BF16)
        wg = jnp.concatenate([wi[:, a_gate:b0], wi[:, b_dt:c0]], axis=1)
        wg = jnp.pad(wg, ((0, 0), (0, LANES - wg.shape[1]))).astype(BF16)
        wo = w_out[l].astype(BF16)
        g1, gm, g2 = ffn1_norm[l][None, :], mix_norm[l][None, :], ffn2_norm[l][None, :]
        gbias = _pad_lanes([b_igate[l], b_fgate[l], dt_bias[l]])
        alog = _pad_lanes([jnp.zeros((GATE_DT,), F32), a_log[l]])
        dskip = jnp.repeat(d_skip[l].astype(F32), P_B)[None, :]
        na, nb, nc = mlstm_norm[l][None, :], ssd_norm[l][None, :], ret_norm[l][None, :]
        cw, cb = conv_w[l], conv_b[l][None, :]
        final = l == depth - 1
        fin = final_norm[None, :]

        qa, ka, va, ga = (wi[:, a0 + i * GROUP_W:a0 + (i + 1) * GROUP_W] for i in range(4))
        qc, kc, vc, gc = (wi[:, c0 + i * GROUP_W:c0 + (i + 1) * GROUP_W] for i in range(4))
        wta = jnp.concatenate([qa, va, ga], axis=1).T.astype(BF16)
        wtc = jnp.concatenate([qc, vc, gc], axis=1).T.astype(BF16)
        wtg = jnp.concatenate([wi[:, a_gate:b0], wi[:, b_dt:c0]], axis=1).T.astype(BF16)
        wk = jnp.concatenate([ka, kc], axis=1).astype(BF16)
        gb8 = jnp.broadcast_to(jnp.concatenate([b_igate[l], b_fgate[l]]).astype(F32)[:, None], (SUBLANES, LANES))
        nab = jnp.broadcast_to(mlstm_norm[l].astype(F32)[:, None], (GROUP_W, LANES))
        ncb = jnp.broadcast_to(ret_norm[l].astype(F32)[:, None], (GROUP_W, LANES))
        xp = _ffn_call(xp, g1, w1a, w3a, w2a)
        pta, ptc, gt, pb, pk, pg = _inproj_prompt_call(xp, gm, wta, wtc, wtg, wb, wk, wg)
        oa, c1, n1, m1 = _mlstm_prompt(pta, pk, gt, gb8, nab, bsz, nchunk)
        ob, h1, buf1 = _ssd_prompt(pb, pg, gbias, alog, dskip, cw, cb, nb, bsz, nchunk)
        oc, s1 = _ret_prompt(ptc, pk, cos_p, sin_p, ncb, bsz, nchunk)
        xp = _mix_ffn_call(xp, oa, ob, oc, wo, g2, w1b, w3b, w2b, fin, final)
        for acc, v in zip(outs_p, (c1, n1, m1[:, GATE_F:GATE_F + H_A, 0], h1, buf1, s1)):
            acc.append(v)

        m0rows = jnp.pad(jnp.repeat(state_mlstm_m[l].astype(F32), dseq, axis=0),
                         ((0, 0), (GATE_F, LANES - GATE_F - H_A)))
        hist = jnp.pad(state_conv[l].astype(F32), ((0, 0), (dseq - (CONV_W - 1), 0), (0, 0)))
        hist = hist.reshape(dbsz * dseq, CONV_DIM)
        xs = _ffn_call(xs, g1, w1a, w3a, w2a)
        pa, pb, pc, pg = _inproj_call(xs, gm, wa, wb, wc, wg)
        oa, c_stack, n1, m1 = _mlstm_sample(pa, pg, gbias, na, state_mlstm_C, state_mlstm_n, m0rows, dseq,
                                            l, c_stack)
        ob, h_stack, buf1 = _ssd_sample(pb, pg, gbias, alog, dskip, cw, cb, nb, state_ssd, hist, dseq, l, h_stack)
        oc, s_stack = _ret_sample(pc, cos_s, sin_s, nc, state_ret, dseq, l, s_stack)
        xs = _mix_ffn_call(xs, oa, ob, oc, wo, g2, w1b, w3b, w2b, fin, final)
        for acc, v in zip(outs_s, (n1, m1[:, GATE_F:GATE_F + H_A], buf1)):
            acc.append(v)

    y_prompt = xp.reshape(bsz, seq, D_MODEL)
    y_sample = xs.reshape(dbsz, dseq, D_MODEL)
    s_n, s_m, s_buf = [jnp.stack(a) for a in outs_s]
    return (y_prompt, y_sample, *[jnp.stack(a) for a in outs_p], c_stack, s_n, s_m, h_stack, s_buf, s_stack)
```

```python
import functools
import math

import jax
import jax.numpy as jnp
from jax import lax
from jax.experimental import pallas as pl
from jax.experimental.pallas import tpu as pltpu

F32 = jnp.float32
BF16 = jnp.bfloat16
HIGHEST = lax.Precision.HIGHEST

D_MODEL = 1024
D_FF = 2816
GROUP_W = 512
H_A = 4
DH = 128
H_B = 8
P_B = 64
N_B = 128
G_B = 2
CONV_W = 4
CONV_DIM = GROUP_W + 2 * G_B * N_B
H_C = 4
CHUNK = 128
PAST_LEN = 16384
GATE_SOFTCAP = 15.0
ROPE_BASE = 10000.0
EPS = 1e-6
QK_SCALE = DH ** -0.5

LANES = 128
SUBLANES = 8

ROWS = 128
SEQ_PER_STEP = 4
SEQ_UNROLL = 8
TM = 512
FC = 256
VMEM_LIMIT = 56 * 1024 * 1024

GATE_I = 0
GATE_F = 4
GATE_DT = 8
GATE_ROWS = 16


def _dot(a, b):
    return jnp.dot(a, b, preferred_element_type=F32)


def _dot_nt(a, b):
    return lax.dot_general(a, b, (((1,), (1,)), ((), ())), preferred_element_type=F32)


def _dot_tn(a, b):
    return lax.dot_general(a, b, (((0,), (0,)), ((), ())), preferred_element_type=F32)


def _dot_exact(a, b):
    return jnp.dot(a, b, precision=HIGHEST, preferred_element_type=F32)


def _rms(x, g):
    return x * lax.rsqrt(jnp.mean(x * x, axis=-1, keepdims=True) + EPS) * g


def _silu(x):
    return x * jax.nn.sigmoid(x)


def _log1p_exp_neg_abs(x):
    return jnp.log1p(jnp.exp(-jnp.abs(x)))


def _causal_mask(rows, lc):
    r = lax.broadcasted_iota(jnp.int32, (rows, rows), 0)
    c = lax.broadcasted_iota(jnp.int32, (rows, rows), 1)
    m = c <= r
    if lc != rows:
        shift = lc.bit_length() - 1
        m = m & ((r >> shift) == (c >> shift))
    return m, r, c


def _group_last(x, lc):
    rows, w = x.shape
    x3 = x.reshape(rows // lc, lc, w)
    return jnp.broadcast_to(x3[:, lc - 1:lc, :], x3.shape).reshape(rows, w)


def _group_first(x, lc):
    rows, w = x.shape
    return x.reshape(rows // lc, lc, w)[:, 0, :]


def _group_sum(x, lc):
    rows, w = x.shape
    return jnp.sum(x.reshape(rows // lc, lc, w), axis=1)


def _group_bcast(x, lc, rows):
    n, w = x.shape
    return jnp.broadcast_to(x[:, None, :], (n, lc, w)).reshape(rows, w)


def _ffn_core(x, g_ref, w1_ref, w3_ref, w2_ref, act_ref):
    h = _rms(x, g_ref[...]).astype(BF16)
    for c in range(D_FF // FC):
        cols = slice(c * FC, (c + 1) * FC)
        a = _dot(h, w1_ref[:, cols])
        b = _dot(h, w3_ref[:, cols])
        act_ref[:, cols] = (_silu(a) * b).astype(BF16)
    return x + 0.5 * _dot(act_ref[...], w2_ref[...])


def _ffn_body(x_ref, g_ref, w1_ref, w3_ref, w2_ref, o_ref, act_ref):
    o_ref[...] = _ffn_core(x_ref[...], g_ref, w1_ref, w3_ref, w2_ref, act_ref)


def _mix_ffn_body(x_ref, oa_ref, ob_ref, oc_ref, wo_ref, g_ref, w1_ref, w3_ref, w2_ref, fin_ref,
                  o_ref, act_ref, *, final):
    x = x_ref[...]
    x = x + (_dot(oa_ref[...], wo_ref[0:GROUP_W, :])
             + _dot(ob_ref[...], wo_ref[GROUP_W:2 * GROUP_W, :])
             + _dot(oc_ref[...], wo_ref[2 * GROUP_W:3 * GROUP_W, :]))
    y = _ffn_core(x, g_ref, w1_ref, w3_ref, w2_ref, act_ref)
    if final:
        y = _rms(y, fin_ref[...])
    o_ref[...] = y


def _inproj_body(x_ref, g_ref, wa_ref, wb_ref, wc_ref, wg_ref, pa_ref, pb_ref, pc_ref, pg_ref):
    h = _rms(x_ref[...], g_ref[...]).astype(BF16)
    pa_ref[...] = _dot(h, wa_ref[...])
    pb_ref[...] = _dot(h, wb_ref[...])
    pc_ref[...] = _dot(h, wc_ref[...])
    pg_ref[...] = _dot(h, wg_ref[...])


def _inproj_prompt_body(x_ref, g_ref, wta_ref, wtc_ref, wtg_ref, wb_ref, wk_ref, wg_ref,
                        pta_ref, ptc_ref, gt_ref, pb_ref, pk_ref, pg_ref):
    h = _rms(x_ref[...], g_ref[...]).astype(BF16)
    for w_ref, o_ref in ((wta_ref, pta_ref), (wtc_ref, ptc_ref), (wtg_ref, gt_ref)):
        pt = _dot_nt(w_ref[...], h)
        for j in range(TM // ROWS):
            o_ref[j] = pt[:, j * ROWS:(j + 1) * ROWS]
    pb_ref[...] = _dot(h, wb_ref[...])
    pk_ref[...] = _dot(h, wk_ref[...])
    pg_ref[...] = _dot(h, wg_ref[...])


def _const_spec(shape):
    nd = len(shape)
    return pl.BlockSpec(shape, lambda *_: (0,) * nd, pipeline_mode=pl.Buffered(1))


def _row_spec(rows, cols):
    return pl.BlockSpec((rows, cols), lambda i: (i, 0))


def _dense_params():
    return pltpu.CompilerParams(dimension_semantics=("arbitrary",), vmem_limit_bytes=VMEM_LIMIT)


def _ffn_call(x, g, w1, w3, w2):
    t = x.shape[0]
    return pl.pallas_call(
        _ffn_body,
        grid=(t // TM,),
        in_specs=[_row_spec(TM, D_MODEL), _const_spec((1, D_MODEL)),
                  _const_spec((D_MODEL, D_FF)), _const_spec((D_MODEL, D_FF)), _const_spec((D_FF, D_MODEL))],
        out_specs=_row_spec(TM, D_MODEL),
        out_shape=jax.ShapeDtypeStruct((t, D_MODEL), F32),
        scratch_shapes=[pltpu.VMEM((TM, D_FF), BF16)],
        compiler_params=_dense_params(),
        name="ffn",
    )(x, g, w1, w3, w2)


def _mix_ffn_call(x, oa, ob, oc, wo, g, w1, w3, w2, fin, final):
    t = x.shape[0]
    return pl.pallas_call(
        functools.partial(_mix_ffn_body, final=final),
        grid=(t // TM,),
        in_specs=[_row_spec(TM, D_MODEL), _row_spec(TM, GROUP_W), _row_spec(TM, GROUP_W), _row_spec(TM, GROUP_W),
                  _const_spec((3 * GROUP_W, D_MODEL)), _const_spec((1, D_MODEL)),
                  _const_spec((D_MODEL, D_FF)), _const_spec((D_MODEL, D_FF)), _const_spec((D_FF, D_MODEL)),
                  _const_spec((1, D_MODEL))],
        out_specs=_row_spec(TM, D_MODEL),
        out_shape=jax.ShapeDtypeStruct((t, D_MODEL), F32),
        scratch_shapes=[pltpu.VMEM((TM, D_FF), BF16)],
        compiler_params=_dense_params(),
        name="mix_ffn",
    )(x, oa, ob, oc, wo, g, w1, w3, w2, fin)


def _inproj_call(x, g, wa, wb, wc, wg):
    t = x.shape[0]
    widths = (wa.shape[1], wb.shape[1], wc.shape[1], wg.shape[1])
    return pl.pallas_call(
        _inproj_body,
        grid=(t // TM,),
        in_specs=[_row_spec(TM, D_MODEL), _const_spec((1, D_MODEL))] + [_const_spec((D_MODEL, w)) for w in widths],
        out_specs=[_row_spec(TM, w) for w in widths],
        out_shape=[jax.ShapeDtypeStruct((t, w), F32) for w in widths],
        compiler_params=_dense_params(),
        name="inproj",
    )(x, g, wa, wb, wc, wg)


def _inproj_prompt_call(x, g, wta, wtc, wtg, wb, wk, wg):
    t = x.shape[0]
    t_rows = (wta.shape[0], wtc.shape[0], wtg.shape[0])
    n_cols = (wb.shape[1], wk.shape[1], wg.shape[1])
    chunk_spec = lambda r: pl.BlockSpec((TM // ROWS, r, ROWS), lambda i: (i, 0, 0))
    return pl.pallas_call(
        _inproj_prompt_body,
        grid=(t // TM,),
        in_specs=([_row_spec(TM, D_MODEL), _const_spec((1, D_MODEL))]
                  + [_const_spec((r, D_MODEL)) for r in t_rows] + [_const_spec((D_MODEL, w)) for w in n_cols]),
        out_specs=[chunk_spec(r) for r in t_rows] + [_row_spec(TM, w) for w in n_cols],
        out_shape=([jax.ShapeDtypeStruct((t // ROWS, r, ROWS), F32) for r in t_rows]
                   + [jax.ShapeDtypeStruct((t, w), F32) for w in n_cols]),
        compiler_params=_dense_params(),
        name="inproj_prompt",
    )(x, g, wta, wtc, wtg, wb, wk, wg)


def _source_target_mask():
    src = lax.broadcasted_iota(jnp.int32, (ROWS, ROWS), 0)
    tgt = lax.broadcasted_iota(jnp.int32, (ROWS, ROWS), 1)
    return src <= tgt, src, tgt


def _last_lane(x):
    return jnp.broadcast_to(x[:, LANES - 1:LANES], x.shape)


def _mlstm_prompt_body(pt_ref, k_ref, gt_ref, gb_ref, normb_ref, out_ref, c_ref, n_ref, m_ref, ct_scr, *, nb):
    @pl.when(pl.program_id(1) == 0)
    def _init():
        ct_scr[...] = jnp.zeros_like(ct_scr)
        n_ref[...] = jnp.zeros_like(n_ref)
        m_ref[...] = jnp.zeros_like(m_ref)

    causal_t, _, _ = _source_target_mask()
    gates = [_mlstm_prompt_gates(gt_ref.at[sq], gb_ref, m_ref.at[sq], causal_t) for sq in range(nb)]
    units = [(sq, h) for sq in range(nb) for h in range(H_A)]

    st_raw, cq, qn, c_upd, n_upd, p_all = {}, {}, {}, {}, {}, {}
    for sq, h in units:
        c = GATE_F + h
        gs = gates[sq]
        qt = (pt_ref[sq, h * DH:(h + 1) * DH, :] * QK_SCALE).astype(BF16)
        vt = pt_ref[sq, GROUP_W + h * DH:GROUP_W + (h + 1) * DH, :]
        kf = k_ref[sq, :, h * DH:(h + 1) * DH]
        kb = kf.astype(BF16)
        n8 = jnp.concatenate([n_ref[sq, h:h + 1, :], jnp.zeros((SUBLANES - 1, DH), F32)], axis=0)
        st_raw[sq, h] = _dot(kb, qt)
        cq[sq, h] = _dot(ct_scr[sq, h].astype(BF16), qt)
        qn[sq, h] = _dot(n8.astype(BF16), qt)[0:1, :]
        c_upd[sq, h] = _dot((vt * gs["wend"][c:c + 1, :]).astype(BF16), kb)
        n_upd[sq, h] = _dot_exact(gs["wend"], kf)[c:c + 1, :]
        p_all[sq, h] = jnp.where(causal_t, jnp.exp(gs["ut"][:, c:c + 1] + gs["bm"][c:c + 1, :]), 0.0)

    num, den = {}, {}
    for sq, h in units:
        c = GATE_F + h
        gs = gates[sq]
        st = st_raw[sq, h] * p_all[sq, h]
        vt = pt_ref[sq, GROUP_W + h * DH:GROUP_W + (h + 1) * DH, :].astype(BF16)
        g = gs["gint"][c:c + 1, :]
        num[sq, h] = _dot(vt, st.astype(BF16)) + g * cq[sq, h]
        den[sq, h] = jnp.sum(st, axis=0, keepdims=True) + g * qn[sq, h]
        g1 = gs["gend"][c:c + 1, 0:1]
        ct_scr[sq, h] = g1 * ct_scr[sq, h] + c_upd[sq, h]
        n_ref[sq, h:h + 1, :] = g1 * n_ref[sq, h:h + 1, :] + n_upd[sq, h]

    for sq, h in units:
        c = GATE_F + h
        hs = slice(h * DH, (h + 1) * DH)
        ot = pt_ref[sq, 2 * GROUP_W + h * DH:2 * GROUP_W + (h + 1) * DH, :]
        hh = num[sq, h] * (1.0 / jnp.maximum(jnp.abs(den[sq, h]), gates[sq]["emt"][c:c + 1, :]))
        rs = lax.rsqrt(jnp.mean(hh * hh, axis=0, keepdims=True) + EPS)
        o = hh * rs * normb_ref[hs, :] * jax.nn.sigmoid(ot)
        out_ref[sq, :, hs] = o.T.astype(out_ref.dtype)

    @pl.when(pl.program_id(1) == pl.num_programs(1) - 1)
    def _finish():
        for sq in range(nb):
            for h in range(H_A):
                c_ref[sq, h] = ct_scr[sq, h].T


def _mlstm_prompt_gates(gt_ref, gb_ref, m_ref, causal_t):
    rowid = lax.broadcasted_iota(jnp.int32, (SUBLANES, LANES), 0)
    frows = rowid >= GATE_F

    pre = gt_ref[0:SUBLANES, :] + gb_ref[...]
    cap = GATE_SOFTCAP * jnp.tanh(pre / GATE_SOFTCAP)
    logf = jnp.minimum(cap, 0.0) - _log1p_exp_neg_abs(cap)
    bcum = _dot_exact(logf, jnp.where(causal_t, 1.0, 0.0))
    mprev = m_ref[...]
    inter = bcum + mprev
    u = pltpu.roll(cap, GATE_F - GATE_I, axis=0) - bcum
    ut = jnp.concatenate([u, jnp.zeros((ROWS - SUBLANES, LANES), F32)], axis=0).T
    mt_all = jnp.zeros((SUBLANES, LANES), F32)
    for h in range(H_A):
        c = GATE_F + h
        d = jnp.where(causal_t, ut[:, c:c + 1] + bcum[c:c + 1, :], -jnp.inf)
        mt = jnp.maximum(inter[c:c + 1, :], jnp.max(d, axis=0, keepdims=True))
        mt_all = jnp.where(rowid == c, mt, mt_all)
    blast = _last_lane(bcum)
    mend = _last_lane(mt_all)
    wend = jnp.where(frows, jnp.exp(blast + u - mend), 0.0)
    gend = jnp.where(frows, jnp.exp(blast + mprev - mend), 0.0)
    gint = jnp.where(frows, jnp.exp(inter - mt_all), 0.0)
    emt = jnp.exp(-mt_all)
    bm = bcum - mt_all
    m_ref[...] = jnp.where(frows, mend, 0.0)
    return dict(ut=ut, bm=bm, wend=wend, gend=gend, gint=gint, emt=emt)


def _mlstm_sample_body(*refs, lc, nseq):
    _mlstm_block(*refs[:7], *refs[8:], lc=lc, nseq=nseq)


def _mlstm_block(pa_ref, pg_ref, gb_ref, norm_ref, c0_ref, n0_ref, m0_ref,
                 out_ref, c_ref, n_ref, m_ref, qc_scr, kw_scr, g_scr, *, lc, nseq):
    rows = lc * nseq
    mprev = m0_ref[...]
    causal, _, _ = _causal_mask(rows, lc)
    lane = lax.broadcasted_iota(jnp.int32, (rows, LANES), 1)
    fcols = (lane >= GATE_F) & (lane < GATE_F + H_A)

    pre = pg_ref[...] + gb_ref[...]
    cap = GATE_SOFTCAP * jnp.tanh(pre / GATE_SOFTCAP)
    logf = jnp.minimum(cap, 0.0) - _log1p_exp_neg_abs(cap)
    bcum = _dot_exact(jnp.where(causal, 1.0, 0.0), logf)
    inter = bcum + mprev
    u = pltpu.roll(cap, GATE_F - GATE_I, axis=1) - bcum
    ut = u.T
    mt_all = jnp.zeros((rows, LANES), F32)
    for h in range(H_A):
        c = GATE_F + h
        d = jnp.where(causal, bcum[:, c:c + 1] + ut[c:c + 1, :], -jnp.inf)
        mt = jnp.maximum(inter[:, c:c + 1], jnp.max(d, axis=1, keepdims=True))
        mt_all = jnp.where(lane == c, mt, mt_all)
    blast = _group_last(bcum, lc)
    mend = _group_last(mt_all, lc)
    wend = jnp.where(fcols, jnp.exp(blast + u - mend), 0.0)
    gend = jnp.where(fcols, jnp.exp(blast + mprev - mend), 0.0)
    gint = jnp.where(fcols, jnp.exp(inter - mt_all), 0.0)
    emt = jnp.exp(-mt_all)
    bm = bcum - mt_all

    def head_cols(group, h):
        return slice(group * GROUP_W + h * DH, group * GROUP_W + (h + 1) * DH)

    g_scr[...] = gend
    for h in range(H_A):
        c = GATE_F + h
        kw_scr[h] = pa_ref[:, head_cols(1, h)] * wend[:, c:c + 1]

    def seq_body(j, carry):
        r0 = pl.multiple_of(j * lc, lc)
        grow = g_scr[pl.ds(r0, 1), :]
        for h in range(H_A):
            c = GATE_F + h
            qj = (pa_ref[pl.ds(r0, lc), head_cols(0, h)] * QK_SCALE).astype(BF16)
            vj = pa_ref[pl.ds(r0, lc), head_cols(2, h)].astype(BF16)
            kwj = kw_scr[h, pl.ds(r0, lc), :].astype(BF16)
            c_old = c0_ref[j, h]
            qc_scr[h, pl.ds(r0, lc), :] = _dot(qj, c_old.astype(BF16))
            c_ref[j, h] = grow[:, c:c + 1] * c_old + _dot_tn(kwj, vj)
        return carry

    lax.fori_loop(0, nseq, seq_body, 0, unroll=SEQ_UNROLL)
    n_rows = []
    gfirst = _group_first(gend, lc)
    for h in range(H_A):
        c = GATE_F + h
        n_old = n0_ref[:, h, :]
        n_rows.append(_group_bcast(n_old, lc, rows))
        n_ref[:, h, :] = gfirst[:, c:c + 1] * n_old + _group_sum(kw_scr[h], lc)
    m_ref[...] = _group_first(mend, lc)

    scores = []
    for h in range(H_A):
        qb = (pa_ref[:, head_cols(0, h)] * QK_SCALE).astype(BF16)
        scores.append(_dot_nt(qb, pa_ref[:, head_cols(1, h)].astype(BF16)))
    for h in range(H_A):
        c = GATE_F + h
        qf = pa_ref[:, head_cols(0, h)] * QK_SCALE
        v = pa_ref[:, head_cols(2, h)].astype(BF16)
        og = pa_ref[:, head_cols(3, h)]
        p = jnp.where(causal, jnp.exp(bm[:, c:c + 1] + ut[c:c + 1, :]), 0.0)
        s = scores[h] * p
        g = gint[:, c:c + 1]
        num = _dot(s.astype(BF16), v) + g * qc_scr[h]
        qn = jnp.sum(qf * n_rows[h], axis=1, keepdims=True)
        den = jnp.sum(s, axis=1, keepdims=True) + g * qn
        hh = num / jnp.maximum(jnp.abs(den), emt[:, c:c + 1])
        hn = hh * lax.rsqrt(jnp.mean(hh * hh, axis=1, keepdims=True) + EPS) * norm_ref[:, h * DH:(h + 1) * DH]
        out_ref[:, h * DH:(h + 1) * DH] = (hn * jax.nn.sigmoid(og)).astype(out_ref.dtype)


def _mixer_params(ndims):
    return pltpu.CompilerParams(dimension_semantics=("arbitrary",) * ndims, vmem_limit_bytes=VMEM_LIMIT)


def _seq_spec(cols):
    return pl.BlockSpec((SEQ_PER_STEP, ROWS, cols), lambda b, c: (b, c, 0))


def _seq_state_spec(shape):
    nd = len(shape)
    return pl.BlockSpec((SEQ_PER_STEP,) + shape, lambda b, c: (b,) + (0,) * nd)


def _seq_t_spec(rows):
    return pl.BlockSpec((SEQ_PER_STEP, None, rows, ROWS), lambda b, c: (b, c, 0, 0))


def _seq_col_spec(cols, col_block):
    return pl.BlockSpec((SEQ_PER_STEP, ROWS, cols), lambda b, c: (b, c, col_block))


def _mlstm_prompt(pt, pk, gt, gb8, normb, bsz, nchunk):
    seq = nchunk * ROWS
    const = lambda b, c: (0, 0)
    out, c1, n1, m1 = pl.pallas_call(
        functools.partial(_mlstm_prompt_body, nb=SEQ_PER_STEP),
        grid=(bsz // SEQ_PER_STEP, nchunk),
        in_specs=[_seq_t_spec(3 * GROUP_W), _seq_col_spec(GROUP_W, 0), _seq_t_spec(GATE_ROWS),
                  pl.BlockSpec((SUBLANES, LANES), const), pl.BlockSpec((GROUP_W, LANES), const)],
        out_specs=[_seq_spec(GROUP_W), _seq_state_spec((H_A, DH, DH)), _seq_state_spec((H_A, DH)),
                   _seq_state_spec((SUBLANES, LANES))],
        out_shape=[jax.ShapeDtypeStruct((bsz, seq, GROUP_W), BF16),
                   jax.ShapeDtypeStruct((bsz, H_A, DH, DH), F32),
                   jax.ShapeDtypeStruct((bsz, H_A, DH), F32),
                   jax.ShapeDtypeStruct((bsz, SUBLANES, LANES), F32)],
        scratch_shapes=[pltpu.VMEM((SEQ_PER_STEP, H_A, DH, DH), F32)],
        compiler_params=_mixer_params(2),
        name="mlstm_prompt",
    )(pt.reshape(bsz, nchunk, 3 * GROUP_W, ROWS), pk.reshape(bsz, seq, -1),
      gt.reshape(bsz, nchunk, GATE_ROWS, ROWS), gb8, normb)
    return out.reshape(bsz * seq, GROUP_W), c1, n1, m1


def _layer_state_spec(shape, layer):
    nd = len(shape)
    return pl.BlockSpec((None,) + shape, lambda i: (layer, i) + (0,) * (nd - 1))


_STACK_SPEC = pl.BlockSpec(memory_space=pl.ANY)


def _mlstm_sample(pa, pg, gbias, norm, c_all, n_all, m0rows, lc, layer, c_prev):
    t = pa.shape[0]
    nseq = ROWS // lc
    bsz = c_all.shape[1]
    row = lambda i: (i, 0)
    const = lambda i: (0, 0)
    return pl.pallas_call(
        functools.partial(_mlstm_sample_body, lc=lc, nseq=nseq),
        grid=(t // ROWS,),
        in_specs=[pl.BlockSpec((ROWS, 4 * GROUP_W), row), pl.BlockSpec((ROWS, LANES), row),
                  pl.BlockSpec((1, LANES), const), pl.BlockSpec((1, GROUP_W), const),
                  _layer_state_spec((nseq, H_A, DH, DH), layer),
                  _layer_state_spec((nseq, H_A, DH), layer),
                  pl.BlockSpec((ROWS, LANES), row), _STACK_SPEC],
        out_specs=[pl.BlockSpec((ROWS, GROUP_W), row),
                   _layer_state_spec((nseq, H_A, DH, DH), layer),
                   pl.BlockSpec((nseq, H_A, DH), lambda i: (i, 0, 0)),
                   pl.BlockSpec((nseq, LANES), row)],
        out_shape=[jax.ShapeDtypeStruct((t, GROUP_W), BF16),
                   jax.ShapeDtypeStruct(c_all.shape, F32),
                   jax.ShapeDtypeStruct((bsz, H_A, DH), F32),
                   jax.ShapeDtypeStruct((bsz, LANES), F32)],
        input_output_aliases={7: 1},
        scratch_shapes=[pltpu.VMEM((H_A, ROWS, DH), F32), pltpu.VMEM((H_A, ROWS, DH), F32),
                        pltpu.VMEM((ROWS, LANES), F32)],
        compiler_params=_mixer_params(1),
        name="mlstm_sample",
    )(pa, pg, gbias, norm, c_all, n_all, m0rows, c_prev)


LOG_GAMMA = [math.log(1.0 - 2.0 ** (-5.0 - h)) for h in range(H_C)]


def _ret_prompt_body(pt_ref, k_ref, cost_ref, sint_ref, cos_ref, sin_ref, dec_ref, normb_ref, out_ref, s_ref,
                     st_scr, *, nb):
    @pl.when(pl.program_id(1) == 0)
    def _init():
        st_scr[...] = jnp.zeros_like(st_scr)

    tin = lax.broadcasted_iota(jnp.int32, (1, ROWS), 1).astype(F32)
    cost, sint = cost_ref[...], sint_ref[...]
    cos, sin = cos_ref[...], sin_ref[...]
    units = [(sq, h) for sq in range(nb) for h in range(H_C)]

    st_raw, qs, s_upd = {}, {}, {}
    for sq, h in units:
        lg = LOG_GAMMA[h]
        qt = pt_ref[sq, h * DH:(h + 1) * DH, :]
        vt = pt_ref[sq, GROUP_W + h * DH:GROUP_W + (h + 1) * DH, :]
        kf = k_ref[sq, :, h * DH:(h + 1) * DH]
        qr = (qt * cost + pltpu.roll(qt, DH // 2, axis=0) * sint).astype(BF16)
        kr = ((kf * cos + pltpu.roll(kf, DH // 2, axis=1) * sin) * QK_SCALE).astype(BF16)
        ve = (vt * jnp.exp((ROWS - 1.0 - tin) * lg)).astype(BF16)
        st_raw[sq, h] = _dot(kr, qr)
        qs[sq, h] = _dot(st_scr[sq, h].astype(BF16), qr)
        s_upd[sq, h] = _dot(ve, kr)

    o_all = {}
    for sq, h in units:
        lg = LOG_GAMMA[h]
        vt = pt_ref[sq, GROUP_W + h * DH:GROUP_W + (h + 1) * DH, :].astype(BF16)
        st = (st_raw[sq, h] * dec_ref[h]).astype(BF16)
        o_all[sq, h] = _dot(vt, st) + jnp.exp((tin + 1.0) * lg) * qs[sq, h]
        st_scr[sq, h] = math.exp(ROWS * lg) * st_scr[sq, h] + s_upd[sq, h]

    for sq, h in units:
        hs = slice(h * DH, (h + 1) * DH)
        gt = pt_ref[sq, 2 * GROUP_W + h * DH:2 * GROUP_W + (h + 1) * DH, :]
        o = o_all[sq, h]
        rs = lax.rsqrt(jnp.mean(o * o, axis=0, keepdims=True) + EPS)
        on = o * rs * normb_ref[hs, :] * _silu(gt)
        out_ref[sq, :, hs] = on.T.astype(out_ref.dtype)

    @pl.when(pl.program_id(1) == pl.num_programs(1) - 1)
    def _finish():
        for sq in range(nb):
            for h in range(H_C):
                s_ref[sq, h] = st_scr[sq, h].T


def _ret_sample_body(*refs, lc, nseq):
    _ret_block(*refs[:5], *refs[6:], lc=lc, nseq=nseq)


def _ret_block(pc_ref, cos_ref, sin_ref, norm_ref, s0_ref, out_ref, s_ref, qs_scr, qr_scr, ke_scr, *, lc, nseq):
    rows = lc * nseq
    causal, r, c = _causal_mask(rows, lc)
    diff = (r - c).astype(F32)
    tin = (lax.broadcasted_iota(jnp.int32, (rows, 1), 0) & (lc - 1)).astype(F32)
    cos = cos_ref[...]
    sin = sin_ref[...]

    def head_cols(group, h):
        return slice(group * GROUP_W + h * DH, group * GROUP_W + (h + 1) * DH)

    def rot(x):
        return x * cos + pltpu.roll(x, DH // 2, axis=1) * sin

    for h in range(H_C):
        qr_scr[h] = rot(pc_ref[:, head_cols(0, h)])
        ke_scr[h] = rot(pc_ref[:, head_cols(1, h)]) * QK_SCALE * jnp.exp((lc - 1.0 - tin) * LOG_GAMMA[h])

    def seq_body(j, carry):
        r0 = pl.multiple_of(j * lc, lc)
        for h in range(H_C):
            qj = qr_scr[h, pl.ds(r0, lc), :].astype(BF16)
            kj = ke_scr[h, pl.ds(r0, lc), :].astype(BF16)
            vj = pc_ref[pl.ds(r0, lc), head_cols(2, h)].astype(BF16)
            s_old = s0_ref[j, h]
            qs_scr[h, pl.ds(r0, lc), :] = _dot(qj, s_old.astype(BF16))
            s_ref[j, h] = math.exp(lc * LOG_GAMMA[h]) * s_old + _dot_tn(kj, vj)
        return carry

    lax.fori_loop(0, nseq, seq_body, 0, unroll=SEQ_UNROLL)

    scores = []
    for h in range(H_C):
        kr = (rot(pc_ref[:, head_cols(1, h)]) * QK_SCALE).astype(BF16)
        scores.append(_dot_nt(qr_scr[h].astype(BF16), kr))
    for h in range(H_C):
        lg = LOG_GAMMA[h]
        decay = jnp.where(causal, jnp.exp(diff * lg), 0.0)
        v = pc_ref[:, head_cols(2, h)].astype(BF16)
        gate = pc_ref[:, head_cols(3, h)]
        o = _dot((scores[h] * decay).astype(BF16), v) + jnp.exp((tin + 1.0) * lg) * qs_scr[h]
        on = o * lax.rsqrt(jnp.mean(o * o, axis=1, keepdims=True) + EPS) * norm_ref[:, h * DH:(h + 1) * DH]
        out_ref[:, h * DH:(h + 1) * DH] = (on * _silu(gate)).astype(out_ref.dtype)


def _ret_prompt(pt, pk, cos, sin, normb, bsz, nchunk):
    seq = nchunk * ROWS
    chunk = lambda b, c: (c, 0)
    chunk_t = lambda b, c: (0, c)
    const = lambda b, c: (0, 0)
    idx = jnp.arange(ROWS, dtype=F32)
    diff = idx[None, :] - idx[:, None]
    log_gamma = jnp.asarray(LOG_GAMMA, F32)[:, None, None]
    dec = jnp.where(diff >= 0, jnp.exp(diff * log_gamma), 0.0)
    out, s1 = pl.pallas_call(
        functools.partial(_ret_prompt_body, nb=SEQ_PER_STEP),
        grid=(bsz // SEQ_PER_STEP, nchunk),
        in_specs=[_seq_t_spec(3 * GROUP_W), _seq_col_spec(GROUP_W, 1),
                  pl.BlockSpec((DH, ROWS), chunk_t), pl.BlockSpec((DH, ROWS), chunk_t),
                  pl.BlockSpec((ROWS, DH), chunk), pl.BlockSpec((ROWS, DH), chunk),
                  pl.BlockSpec((H_C, ROWS, ROWS), lambda b, c: (0, 0, 0)),
                  pl.BlockSpec((GROUP_W, LANES), const)],
        out_specs=[_seq_spec(GROUP_W), _seq_state_spec((H_C, DH, DH))],
        out_shape=[jax.ShapeDtypeStruct((bsz, seq, GROUP_W), BF16),
                   jax.ShapeDtypeStruct((bsz, H_C, DH, DH), F32)],
        scratch_shapes=[pltpu.VMEM((SEQ_PER_STEP, H_C, DH, DH), F32)],
        compiler_params=_mixer_params(2),
        name="ret_prompt",
    )(pt.reshape(bsz, nchunk, 3 * GROUP_W, ROWS), pk.reshape(bsz, seq, -1), cos.T, sin.T, cos, sin, dec, normb)
    return out.reshape(bsz * seq, GROUP_W), s1


def _ret_sample(pc, cos, sin, norm, s_all, lc, layer, s_prev):
    t = pc.shape[0]
    nseq = ROWS // lc
    row = lambda i: (i, 0)
    const = lambda i: (0, 0)
    return pl.pallas_call(
        functools.partial(_ret_sample_body, lc=lc, nseq=nseq),
        grid=(t // ROWS,),
        in_specs=[pl.BlockSpec((ROWS, 4 * GROUP_W), row), pl.BlockSpec((ROWS, DH), const),
                  pl.BlockSpec((ROWS, DH), const), pl.BlockSpec((1, GROUP_W), const),
                  _layer_state_spec((nseq, H_C, DH, DH), layer), _STACK_SPEC],
        out_specs=[pl.BlockSpec((ROWS, GROUP_W), row),
                   _layer_state_spec((nseq, H_C, DH, DH), layer)],
        out_shape=[jax.ShapeDtypeStruct((t, GROUP_W), BF16),
                   jax.ShapeDtypeStruct(s_all.shape, F32)],
        input_output_aliases={5: 1},
        scratch_shapes=[pltpu.VMEM((H_C, ROWS, DH), F32)] * 3,
        compiler_params=_mixer_params(1),
        name="ret_sample",
    )(pc, cos, sin, norm, s_all, s_prev)


HEADS_PER_GROUP = H_B // G_B
PAIR_W = 2 * P_B
GROUP_CH = HEADS_PER_GROUP * P_B


def _ssd_prompt_body(pb_ref, pg_ref, gb_ref, alog_ref, dskip_ref, cw_ref, cb_ref, norm_ref,
                     out_ref, h_ref, buf_ref, xc_scr, tail_scr, *, nb):
    @pl.when(pl.program_id(1) == 0)
    def _init():
        h_ref[...] = jnp.zeros_like(h_ref)
        tail_scr[...] = jnp.zeros_like(tail_scr)

    rows = ROWS
    causal, _, _ = _causal_mask(rows, rows)
    lane = lax.broadcasted_iota(jnp.int32, (rows, LANES), 1)
    dcols = (lane >= GATE_DT) & (lane < GATE_DT + H_B)
    low = lax.broadcasted_iota(jnp.int32, (rows, PAIR_W), 1) < P_B
    tin8 = lax.broadcasted_iota(jnp.int32, (SUBLANES, 1), 0)
    pairs_per_group = HEADS_PER_GROUP // 2

    def pair_bcast(slab, c0):
        return jnp.where(low, slab[:, c0:c0 + 1], slab[:, c0 + 1:c0 + 2])

    def pair_cols(g, p):
        start = g * GROUP_CH + p * PAIR_W
        return slice(start, start + PAIR_W)

    def b_cols(g):
        return slice(GROUP_W + g * N_B, GROUP_W + (g + 1) * N_B)

    def c_cols(g):
        return slice(GROUP_W + G_B * N_B + g * N_B, GROUP_W + G_B * N_B + (g + 1) * N_B)

    gates = []
    for sq in range(nb):
        new = pb_ref[sq, :, GROUP_W:GROUP_W + CONV_DIM]
        acc = new * cw_ref[CONV_W - 1:CONV_W, :]
        for k in range(1, CONV_W):
            rolled = pltpu.roll(new, k, axis=0)
            tail = pltpu.roll(tail_scr[sq], k, axis=0)
            first = jnp.where(tin8 >= k, rolled[0:SUBLANES], tail)
            shifted = jnp.concatenate([first, rolled[SUBLANES:]], axis=0)
            acc = acc + shifted * cw_ref[CONV_W - 1 - k:CONV_W - k, :]
        xc_scr[sq] = _silu(acc + cb_ref[...])
        tail_scr[sq] = new[rows - SUBLANES:rows, :]
        buf_ref[sq] = new[rows - (CONV_W - 1):rows, :]

        dpre = pg_ref[sq] + gb_ref[...]
        dt = jnp.maximum(dpre, 0.0) + _log1p_exp_neg_abs(dpre)
        adt = dt * (-jnp.exp(alog_ref[...]))
        acum = jnp.where(dcols, _dot_exact(jnp.where(causal, 1.0, 0.0), adt), 0.0)
        alast = jnp.broadcast_to(acum[rows - 1:rows, :], (rows, LANES))
        gates.append(dict(dt=dt, acum=acum, at=acum.T, exp_a=jnp.exp(acum), wx=jnp.exp(alast - acum),
                          g_a=jnp.exp(alast)))

    groups = [(sq, g) for sq in range(nb) for g in range(G_B)]
    cbs, chs, upds, xdts = {}, {}, {}, {}
    for sq, g in groups:
        gs = gates[sq]
        bg = xc_scr[sq, :, b_cols(g)].astype(BF16)
        cg = xc_scr[sq, :, c_cols(g)].astype(BF16)
        xws = []
        for p in range(pairs_per_group):
            c0 = GATE_DT + g * HEADS_PER_GROUP + 2 * p
            xdt = xc_scr[sq, :, pair_cols(g, p)] * pair_bcast(gs["dt"], c0)
            xdts[sq, g, p] = xdt.astype(BF16)
            xws.append((xdt * pair_bcast(gs["wx"], c0)).astype(BF16))
        h_old = jnp.concatenate([h_ref[sq, g * HEADS_PER_GROUP + r] for r in range(HEADS_PER_GROUP)], axis=0)
        cbs[sq, g] = _dot_nt(cg, bg)
        chs[sq, g] = _dot_nt(cg, h_old.astype(BF16))
        upds[sq, g] = _dot_tn(jnp.concatenate(xws, axis=1), bg)

    halves = {}
    for sq, g in groups:
        gs = gates[sq]
        for r in range(HEADS_PER_GROUP):
            hd = g * HEADS_PER_GROUP + r
            c = GATE_DT + hd
            dec = jnp.where(causal, jnp.exp(gs["acum"][:, c:c + 1] - gs["at"][c:c + 1, :]), 0.0)
            halves[sq, hd] = _dot((cbs[sq, g] * dec).astype(BF16), xdts[sq, g, r // 2])
            h_ref[sq, hd] = gs["g_a"][0:1, c:c + 1] * h_ref[sq, hd] + upds[sq, g][r * P_B:(r + 1) * P_B, :]

    for sq, g in groups:
        gs = gates[sq]
        ys = []
        for p in range(pairs_per_group):
            hd0 = g * HEADS_PER_GROUP + 2 * p
            c0 = GATE_DT + hd0
            y = (jnp.where(low, halves[sq, hd0], halves[sq, hd0 + 1])
                 + pair_bcast(gs["exp_a"], c0) * chs[sq, g][:, p * PAIR_W:(p + 1) * PAIR_W])
            ys.append(y + dskip_ref[:, pair_cols(g, p)] * xc_scr[sq, :, pair_cols(g, p)])
        gcols = slice(g * GROUP_CH, (g + 1) * GROUP_CH)
        yz = jnp.concatenate(ys, axis=1) * _silu(pb_ref[sq, :, gcols])
        yn = yz * lax.rsqrt(jnp.mean(yz * yz, axis=1, keepdims=True) + EPS) * norm_ref[:, gcols]
        out_ref[sq, :, gcols] = yn.astype(out_ref.dtype)


def _ssd_sample_body(*refs, lc, nseq):
    _ssd_block(*refs[:10], *refs[11:], lc=lc, nseq=nseq)


def _ssd_block(pb_ref, pg_ref, gb_ref, alog_ref, dskip_ref, cw_ref, cb_ref, norm_ref, h0_ref, hist_ref,
               out_ref, h_ref, buf_ref, ch_scr, xc_scr, xw_scr, g_scr, *, lc, nseq):
    rows = lc * nseq
    causal, _, _ = _causal_mask(rows, lc)
    lane = lax.broadcasted_iota(jnp.int32, (rows, LANES), 1)
    dcols = (lane >= GATE_DT) & (lane < GATE_DT + H_B)
    low = lax.broadcasted_iota(jnp.int32, (rows, PAIR_W), 1) < P_B

    new = pb_ref[:, GROUP_W:GROUP_W + CONV_DIM]
    tin = lax.broadcasted_iota(jnp.int32, (rows, 1), 0) & (lc - 1)
    acc = new * cw_ref[CONV_W - 1:CONV_W, :]
    for k in range(1, CONV_W):
        rolled = pltpu.roll(new, k, axis=0)
        hist = pltpu.roll(hist_ref[...], (rows + k - lc) % rows, axis=0)
        acc = acc + jnp.where(tin >= k, rolled, hist) * cw_ref[CONV_W - 1 - k:CONV_W - k, :]
    xc = _silu(acc + cb_ref[...])
    buf_ref[...] = new.reshape(nseq, lc, CONV_DIM)[:, lc - (CONV_W - 1):lc, :]

    dpre = pg_ref[...] + gb_ref[...]
    dt = jnp.maximum(dpre, 0.0) + _log1p_exp_neg_abs(dpre)
    adt = dt * (-jnp.exp(alog_ref[...]))
    acum = jnp.where(dcols, _dot_exact(jnp.where(causal, 1.0, 0.0), adt), 0.0)
    at = acum.T
    alast = _group_last(acum, lc)
    exp_a = jnp.exp(acum)
    wx = jnp.exp(alast - acum)
    g_a = jnp.exp(alast)

    def pair_bcast(slab, c0):
        return jnp.where(low, slab[:, c0:c0 + 1], slab[:, c0 + 1:c0 + 2])

    def pair_cols(g, p):
        start = g * GROUP_CH + p * PAIR_W
        return slice(start, start + PAIR_W)

    def b_cols(g):
        return slice(GROUP_W + g * N_B, GROUP_W + (g + 1) * N_B)

    def c_cols(g):
        return slice(GROUP_W + G_B * N_B + g * N_B, GROUP_W + G_B * N_B + (g + 1) * N_B)

    xw_pairs = {}
    xdt_pairs = {}
    for g in range(G_B):
        for p in range(HEADS_PER_GROUP // 2):
            c0 = GATE_DT + g * HEADS_PER_GROUP + 2 * p
            xdt = xc[:, pair_cols(g, p)] * pair_bcast(dt, c0)
            xdt_pairs[g, p] = xdt.astype(BF16)
            xw_pairs[g, p] = xdt * pair_bcast(wx, c0)
    xc_scr[...] = xc
    g_scr[...] = g_a
    for g in range(G_B):
        for p in range(HEADS_PER_GROUP // 2):
            xw_scr[:, pair_cols(g, p)] = xw_pairs[g, p]

    def seq_body(j, carry):
        r0 = pl.multiple_of(j * lc, lc)
        grow = g_scr[pl.ds(r0, 1), :]
        for g in range(G_B):
            bj = xc_scr[pl.ds(r0, lc), b_cols(g)].astype(BF16)
            cj = xc_scr[pl.ds(r0, lc), c_cols(g)].astype(BF16)
            xwj = xw_scr[pl.ds(r0, lc), g * GROUP_CH:(g + 1) * GROUP_CH].astype(BF16)
            hs = [h0_ref[j, g * HEADS_PER_GROUP + r] for r in range(HEADS_PER_GROUP)]
            h_old = jnp.concatenate(hs, axis=0)
            ch_scr[g, pl.ds(r0, lc), :] = _dot_nt(cj, h_old.astype(BF16))
            upd = _dot_tn(xwj, bj)
            for r in range(HEADS_PER_GROUP):
                hd = g * HEADS_PER_GROUP + r
                c = GATE_DT + hd
                h_ref[j, hd] = grow[:, c:c + 1] * hs[r] + upd[r * P_B:(r + 1) * P_B, :]
        return carry

    lax.fori_loop(0, nseq, seq_body, 0, unroll=SEQ_UNROLL)

    cbs = [_dot_nt(xc[:, c_cols(g)].astype(BF16), xc[:, b_cols(g)].astype(BF16)) for g in range(G_B)]
    for g in range(G_B):
        cb = cbs[g]
        ys = []
        for p in range(HEADS_PER_GROUP // 2):
            c0 = GATE_DT + g * HEADS_PER_GROUP + 2 * p
            xpair = xc[:, pair_cols(g, p)]
            xdt = xdt_pairs[g, p]
            halves = []
            for c in (c0, c0 + 1):
                dec = jnp.where(causal, jnp.exp(acum[:, c:c + 1] - at[c:c + 1, :]), 0.0)
                halves.append(_dot((cb * dec).astype(BF16), xdt))
            y = (jnp.where(low, halves[0], halves[1])
                 + pair_bcast(exp_a, c0) * ch_scr[g, :, p * PAIR_W:(p + 1) * PAIR_W])
            ys.append(y + dskip_ref[:, pair_cols(g, p)] * xpair)
        yg = jnp.concatenate(ys, axis=1)
        gcols = slice(g * GROUP_CH, (g + 1) * GROUP_CH)
        yz = yg * _silu(pb_ref[:, gcols])
        yn = yz * lax.rsqrt(jnp.mean(yz * yz, axis=1, keepdims=True) + EPS) * norm_ref[:, gcols]
        out_ref[:, gcols] = yn.astype(out_ref.dtype)


def _ssd_prompt(pb, pg, gbias, alog, dskip, cw, cb, norm, bsz, nchunk):
    seq = nchunk * ROWS
    const = lambda b, c: (0, 0)
    out, h1, buf1 = pl.pallas_call(
        functools.partial(_ssd_prompt_body, nb=SEQ_PER_STEP),
        grid=(bsz // SEQ_PER_STEP, nchunk),
        in_specs=[_seq_spec(GROUP_W + CONV_DIM), _seq_spec(LANES),
                  pl.BlockSpec((1, LANES), const), pl.BlockSpec((1, LANES), const),
                  pl.BlockSpec((1, GROUP_W), const), pl.BlockSpec((CONV_W, CONV_DIM), const),
                  pl.BlockSpec((1, CONV_DIM), const), pl.BlockSpec((1, GROUP_W), const)],
        out_specs=[_seq_spec(GROUP_W), _seq_state_spec((H_B, P_B, N_B)),
                   _seq_state_spec((CONV_W - 1, CONV_DIM))],
        out_shape=[jax.ShapeDtypeStruct((bsz, seq, GROUP_W), BF16),
                   jax.ShapeDtypeStruct((bsz, H_B, P_B, N_B), F32),
                   jax.ShapeDtypeStruct((bsz, CONV_W - 1, CONV_DIM), F32)],
        scratch_shapes=[pltpu.VMEM((SEQ_PER_STEP, ROWS, CONV_DIM), F32),
                        pltpu.VMEM((SEQ_PER_STEP, SUBLANES, CONV_DIM), F32)],
        compiler_params=_mixer_params(2),
        name="ssd_prompt",
    )(pb.reshape(bsz, seq, -1), pg.reshape(bsz, seq, -1), gbias, alog, dskip, cw, cb, norm)
    return out.reshape(bsz * seq, GROUP_W), h1, buf1


def _ssd_sample(pb, pg, gbias, alog, dskip, cw, cb, norm, h_all, hist, lc, layer, h_prev):
    t = pb.shape[0]
    nseq = ROWS // lc
    bsz = h_all.shape[1]
    row = lambda i: (i, 0)
    const = lambda i: (0, 0)
    return pl.pallas_call(
        functools.partial(_ssd_sample_body, lc=lc, nseq=nseq),
        grid=(t // ROWS,),
        in_specs=[pl.BlockSpec((ROWS, GROUP_W + CONV_DIM), row), pl.BlockSpec((ROWS, LANES), row),
                  pl.BlockSpec((1, LANES), const), pl.BlockSpec((1, LANES), const),
                  pl.BlockSpec((1, GROUP_W), const), pl.BlockSpec((CONV_W, CONV_DIM), const),
                  pl.BlockSpec((1, CONV_DIM), const), pl.BlockSpec((1, GROUP_W), const),
                  _layer_state_spec((nseq, H_B, P_B, N_B), layer),
                  pl.BlockSpec((ROWS, CONV_DIM), row), _STACK_SPEC],
        out_specs=[pl.BlockSpec((ROWS, GROUP_W), row),
                   _layer_state_spec((nseq, H_B, P_B, N_B), layer),
                   pl.BlockSpec((nseq, CONV_W - 1, CONV_DIM), lambda i: (i, 0, 0))],
        out_shape=[jax.ShapeDtypeStruct((t, GROUP_W), BF16),
                   jax.ShapeDtypeStruct(h_all.shape, F32),
                   jax.ShapeDtypeStruct((bsz, CONV_W - 1, CONV_DIM), F32)],
        input_output_aliases={10: 1},
        scratch_shapes=[pltpu.VMEM((G_B, ROWS, GROUP_CH), F32), pltpu.VMEM((ROWS, CONV_DIM), F32),
                        pltpu.VMEM((ROWS, GROUP_W), F32), pltpu.VMEM((ROWS, LANES), F32)],
        compiler_params=_mixer_params(1),
        name="ssd_sample",
    )(pb, pg, gbias, alog, dskip, cw, cb, norm, h_all, hist, h_prev)


def _rope_tables(pos):
    half = DH // 2
    freqs = ROPE_BASE ** (-jnp.arange(half, dtype=F32) / half)
    ang = pos.astype(F32)[:, None] * freqs
    cos, sin = jnp.cos(ang), jnp.sin(ang)
    return jnp.concatenate([cos, cos], axis=-1), jnp.concatenate([-sin, sin], axis=-1)


def _pad_lanes(parts):
    row = jnp.concatenate([p.astype(F32) for p in parts])
    return jnp.pad(row, (0, LANES - row.shape[0]))[None, :]


def kernel(x_prompt, x_sample, state_mlstm_C, state_mlstm_n, state_mlstm_m, state_ssd, state_conv, state_ret,
           ffn1_norm, ffn1_w1, ffn1_w3, ffn1_w2, mix_norm, w_in, b_igate, b_fgate, mlstm_norm,
           conv_w, conv_b, dt_bias, a_log, d_skip, ssd_norm, ret_norm, w_out,
           ffn2_norm, ffn2_w1, ffn2_w3, ffn2_w2, final_norm):
    depth = w_in.shape[0]
    bsz, seq, _ = x_prompt.shape
    dbsz, dseq, _ = x_sample.shape
    assert seq % CHUNK == 0 and ROWS % dseq == 0 and (dbsz * dseq) % ROWS == 0
    assert dseq >= SUBLANES and dseq & (dseq - 1) == 0
    nchunk = seq // CHUNK
    seq_per_block = ROWS // dseq

    xp = x_prompt.reshape(bsz * seq, D_MODEL)
    xs = x_sample.reshape(dbsz * dseq, D_MODEL)

    cos_p, sin_p = _rope_tables(jnp.arange(seq))
    cos_s, sin_s = _rope_tables(PAST_LEN + jnp.arange(dseq))
    cos_s, sin_s = jnp.tile(cos_s, (seq_per_block, 1)), jnp.tile(sin_s, (seq_per_block, 1))

    a0 = 0
    a_gate = a0 + 4 * GROUP_W
    b0 = a_gate + 2 * H_A
    b_dt = b0 + GROUP_W + CONV_DIM
    c0 = b_dt + H_B

    outs_p = [[] for _ in range(6)]
    outs_s = [[] for _ in range(3)]
    c_stack = jnp.zeros(state_mlstm_C.shape, F32)
    h_stack = jnp.zeros(state_ssd.shape, F32)
    s_stack = jnp.zeros(state_ret.shape, F32)
    for l in range(depth):
        w1a, w3a, w2a = ffn1_w1[l].astype(BF16), ffn1_w3[l].astype(BF16), ffn1_w2[l].astype(BF16)
        w1b, w3b, w2b = ffn2_w1[l].astype(BF16), ffn2_w3[l].astype(BF16), ffn2_w2[l].astype(BF16)
        wi = w_in[l]
        wa = wi[:, a0:a_gate].astype(BF16)
        wb = wi[:, b0:b_dt].astype(BF16)
        wc = wi[:, c0:c0 + 4 * GROUP_W].astype(BF16)
        wg = jnp.concatenate([wi[:, a_gate:b0], wi[:, b_dt:c0]], axis=1)
        wg = jnp.pad(wg, ((0, 0), (0, LANES - wg.shape[1]))).astype(BF16)
        wo = w_out[l].astype(BF16)
        g1, gm, g2 = ffn1_norm[l][None, :], mix_norm[l][None, :], ffn2_norm[l][None, :]
        gbias = _pad_lanes([b_igate[l], b_fgate[l], dt_bias[l]])
        alog = _pad_lanes([jnp.zeros((GATE_DT,), F32), a_log[l]])
        dskip = jnp.repeat(d_skip[l].astype(F32), P_B)[None, :]
        na, nb, nc = mlstm_norm[l][None, :], ssd_norm[l][None, :], ret_norm[l][None, :]
        cw, cb = conv_w[l], conv_b[l][None, :]
        final = l == depth - 1
        fin = final_norm[None, :]

        qa, ka, va, ga = (wi[:, a0 + i * GROUP_W:a0 + (i + 1) * GROUP_W] for i in range(4))
        qc, kc, vc, gc = (wi[:, c0 + i * GROUP_W:c0 + (i + 1) * GROUP_W] for i in range(4))
        wta = jnp.concatenate([qa, va, ga], axis=1).T.astype(BF16)
        wtc = jnp.concatenate([qc, vc, gc], axis=1).T.astype(BF16)
        wtg = jnp.concatenate([wi[:, a_gate:b0], wi[:, b_dt:c0]], axis=1).T.astype(BF16)
        wk = jnp.concatenate([ka, kc], axis=1).astype(BF16)
        gb8 = jnp.broadcast_to(jnp.concatenate([b_igate[l], b_fgate[l]]).astype(F32)[:, None], (SUBLANES, LANES))
        nab = jnp.broadcast_to(mlstm_norm[l].astype(F32)[:, None], (GROUP_W, LANES))
        ncb = jnp.broadcast_to(ret_norm[l].astype(F32)[:, None], (GROUP_W, LANES))
        xp = _ffn_call(xp, g1, w1a, w3a, w2a)
        pta, ptc, gt, pb, pk, pg = _inproj_prompt_call(xp, gm, wta, wtc, wtg, wb, wk, wg)
        oa, c1, n1, m1 = _mlstm_prompt(pta, pk, gt, gb8, nab, bsz, nchunk)
        ob, h1, buf1 = _ssd_prompt(pb, pg, gbias, alog, dskip, cw, cb, nb, bsz, nchunk)
        oc, s1 = _ret_prompt(ptc, pk, cos_p, sin_p, ncb, bsz, nchunk)
        xp = _mix_ffn_call(xp, oa, ob, oc, wo, g2, w1b, w3b, w2b, fin, final)
        for acc, v in zip(outs_p, (c1, n1, m1[:, GATE_F:GATE_F + H_A, 0], h1, buf1, s1)):
            acc.append(v)

        m0rows = jnp.pad(jnp.repeat(state_mlstm_m[l].astype(F32), dseq, axis=0),
                         ((0, 0), (GATE_F, LANES - GATE_F - H_A)))
        hist = jnp.pad(state_conv[l].astype(F32), ((0, 0), (dseq - (CONV_W - 1), 0), (0, 0)))
        hist = hist.reshape(dbsz * dseq, CONV_DIM)
        xs = _ffn_call(xs, g1, w1a, w3a, w2a)
        pa, pb, pc, pg = _inproj_call(xs, gm, wa, wb, wc, wg)
        oa, c_stack, n1, m1 = _mlstm_sample(pa, pg, gbias, na, state_mlstm_C, state_mlstm_n, m0rows, dseq,
                                            l, c_stack)
        ob, h_stack, buf1 = _ssd_sample(pb, pg, gbias, alog, dskip, cw, cb, nb, state_ssd, hist, dseq, l, h_stack)
        oc, s_stack = _ret_sample(pc, cos_s, sin_s, nc, state_ret, dseq, l, s_stack)
        xs = _mix_ffn_call(xs, oa, ob, oc, wo, g2, w1b, w3b, w2b, fin, final)
        for acc, v in zip(outs_s, (n1, m1[:, GATE_F:GATE_F + H_A], buf1)):
            acc.append(v)

    y_prompt = xp.reshape(bsz, seq, D_MODEL)
    y_sample = xs.reshape(dbsz, dseq, D_MODEL)
    s_n, s_m, s_buf = [jnp.stack(a) for a in outs_s]
    return (y_prompt, y_sample, *[jnp.stack(a) for a in outs_p], c_stack, s_n, s_m, h_stack, s_buf, s_stack)
```

```python
import functools
import math

import jax
import jax.numpy as jnp
from jax import lax
from jax.experimental import pallas as pl
from jax.experimental.pallas import tpu as pltpu

F32 = jnp.float32
BF16 = jnp.bfloat16
HIGHEST = lax.Precision.HIGHEST

D_MODEL = 1024
D_FF = 2816
GROUP_W = 512
H_A = 4
DH = 128
H_B = 8
P_B = 64
N_B = 128
G_B = 2
CONV_W = 4
CONV_DIM = GROUP_W + 2 * G_B * N_B
H_C = 4
CHUNK = 128
PAST_LEN = 16384
GATE_SOFTCAP = 15.0
ROPE_BASE = 10000.0
EPS = 1e-6
QK_SCALE = DH ** -0.5

LANES = 128
SUBLANES = 8

ROWS = 128
SEQ_PER_STEP = 4
SEQ_UNROLL = 8
TM = 512
FC = 256
VMEM_LIMIT = 56 * 1024 * 1024

GATE_I = 0
GATE_F = 4
GATE_DT = 8
GATE_ROWS = 16


def _dot(a, b):
    return jnp.dot(a, b, preferred_element_type=F32)


def _dot_nt(a, b):
    return lax.dot_general(a, b, (((1,), (1,)), ((), ())), preferred_element_type=F32)


def _dot_tn(a, b):
    return lax.dot_general(a, b, (((0,), (0,)), ((), ())), preferred_element_type=F32)


def _dot_exact(a, b):
    return jnp.dot(a, b, precision=HIGHEST, preferred_element_type=F32)


def _rms(x, g):
    return x * lax.rsqrt(jnp.mean(x * x, axis=-1, keepdims=True) + EPS) * g


def _silu(x):
    return x * jax.nn.sigmoid(x)


def _log1p_exp_neg_abs(x):
    return jnp.log1p(jnp.exp(-jnp.abs(x)))


def _causal_mask(rows, lc):
    r = lax.broadcasted_iota(jnp.int32, (rows, rows), 0)
    c = lax.broadcasted_iota(jnp.int32, (rows, rows), 1)
    m = c <= r
    if lc != rows:
        shift = lc.bit_length() - 1
        m = m & ((r >> shift) == (c >> shift))
    return m, r, c


def _group_last(x, lc):
    rows, w = x.shape
    x3 = x.reshape(rows // lc, lc, w)
    return jnp.broadcast_to(x3[:, lc - 1:lc, :], x3.shape).reshape(rows, w)


def _group_first(x, lc):
    rows, w = x.shape
    return x.reshape(rows // lc, lc, w)[:, 0, :]


def _group_sum(x, lc):
    rows, w = x.shape
    return jnp.sum(x.reshape(rows // lc, lc, w), axis=1)


def _group_bcast(x, lc, rows):
    n, w = x.shape
    return jnp.broadcast_to(x[:, None, :], (n, lc, w)).reshape(rows, w)


def _ffn_core(x, g_ref, w1_ref, w3_ref, w2_ref, act_ref):
    h = _rms(x, g_ref[...]).astype(BF16)
    for c in range(D_FF // FC):
        cols = slice(c * FC, (c + 1) * FC)
        a = _dot(h, w1_ref[:, cols])
        b = _dot(h, w3_ref[:, cols])
        act_ref[:, cols] = (_silu(a) * b).astype(BF16)
    return x + 0.5 * _dot(act_ref[...], w2_ref[...])


def _ffn_body(x_ref, g_ref, w1_ref, w3_ref, w2_ref, o_ref, act_ref):
    o_ref[...] = _ffn_core(x_ref[...], g_ref, w1_ref, w3_ref, w2_ref, act_ref)


def _mix_ffn_body(x_ref, oa_ref, ob_ref, oc_ref, wo_ref, g_ref, w1_ref, w3_ref, w2_ref, fin_ref,
                  o_ref, act_ref, *, final):
    x = x_ref[...]
    x = x + (_dot(oa_ref[...], wo_ref[0:GROUP_W, :])
             + _dot(ob_ref[...], wo_ref[GROUP_W:2 * GROUP_W, :])
             + _dot(oc_ref[...], wo_ref[2 * GROUP_W:3 * GROUP_W, :]))
    y = _ffn_core(x, g_ref, w1_ref, w3_ref, w2_ref, act_ref)
    if final:
        y = _rms(y, fin_ref[...])
    o_ref[...] = y


def _inproj_body(x_ref, g_ref, wa_ref, wb_ref, wc_ref, wg_ref, pa_ref, pb_ref, pc_ref, pg_ref):
    h = _rms(x_ref[...], g_ref[...]).astype(BF16)
    pa_ref[...] = _dot(h, wa_ref[...])
    pb_ref[...] = _dot(h, wb_ref[...])
    pc_ref[...] = _dot(h, wc_ref[...])
    pg_ref[...] = _dot(h, wg_ref[...])


def _inproj_prompt_body(x_ref, g_ref, wta_ref, wtb_ref, wtc_ref, wtg_ref, wk_ref,
                        pta_ref, ptb_ref, ptc_ref, gt_ref, pk_ref):
    h = _rms(x_ref[...], g_ref[...]).astype(BF16)
    for w_ref, o_ref in ((wta_ref, pta_ref), (wtb_ref, ptb_ref), (wtc_ref, ptc_ref), (wtg_ref, gt_ref)):
        pt = _dot_nt(w_ref[...], h)
        for j in range(TM // ROWS):
            o_ref[j] = pt[:, j * ROWS:(j + 1) * ROWS]
    pk_ref[...] = _dot(h, wk_ref[...])


def _const_spec(shape):
    nd = len(shape)
    return pl.BlockSpec(shape, lambda *_: (0,) * nd, pipeline_mode=pl.Buffered(1))


def _row_spec(rows, cols):
    return pl.BlockSpec((rows, cols), lambda i: (i, 0))


def _dense_params():
    return pltpu.CompilerParams(dimension_semantics=("arbitrary",), vmem_limit_bytes=VMEM_LIMIT)


def _ffn_call(x, g, w1, w3, w2):
    t = x.shape[0]
    return pl.pallas_call(
        _ffn_body,
        grid=(t // TM,),
        in_specs=[_row_spec(TM, D_MODEL), _const_spec((1, D_MODEL)),
                  _const_spec((D_MODEL, D_FF)), _const_spec((D_MODEL, D_FF)), _const_spec((D_FF, D_MODEL))],
        out_specs=_row_spec(TM, D_MODEL),
        out_shape=jax.ShapeDtypeStruct((t, D_MODEL), F32),
        scratch_shapes=[pltpu.VMEM((TM, D_FF), BF16)],
        compiler_params=_dense_params(),
        name="ffn",
    )(x, g, w1, w3, w2)


def _mix_ffn_call(x, oa, ob, oc, wo, g, w1, w3, w2, fin, final):
    t = x.shape[0]
    return pl.pallas_call(
        functools.partial(_mix_ffn_body, final=final),
        grid=(t // TM,),
        in_specs=[_row_spec(TM, D_MODEL), _row_spec(TM, GROUP_W), _row_spec(TM, GROUP_W), _row_spec(TM, GROUP_W),
                  _const_spec((3 * GROUP_W, D_MODEL)), _const_spec((1, D_MODEL)),
                  _const_spec((D_MODEL, D_FF)), _const_spec((D_MODEL, D_FF)), _const_spec((D_FF, D_MODEL)),
                  _const_spec((1, D_MODEL))],
        out_specs=_row_spec(TM, D_MODEL),
        out_shape=jax.ShapeDtypeStruct((t, D_MODEL), F32),
        scratch_shapes=[pltpu.VMEM((TM, D_FF), BF16)],
        compiler_params=_dense_params(),
        name="mix_ffn",
    )(x, oa, ob, oc, wo, g, w1, w3, w2, fin)


def _inproj_call(x, g, wa, wb, wc, wg):
    t = x.shape[0]
    widths = (wa.shape[1], wb.shape[1], wc.shape[1], wg.shape[1])
    return pl.pallas_call(
        _inproj_body,
        grid=(t // TM,),
        in_specs=[_row_spec(TM, D_MODEL), _const_spec((1, D_MODEL))] + [_const_spec((D_MODEL, w)) for w in widths],
        out_specs=[_row_spec(TM, w) for w in widths],
        out_shape=[jax.ShapeDtypeStruct((t, w), F32) for w in widths],
        compiler_params=_dense_params(),
        name="inproj",
    )(x, g, wa, wb, wc, wg)


def _inproj_prompt_call(x, g, wta, wtb, wtc, wtg, wk):
    t = x.shape[0]
    t_rows = (wta.shape[0], wtb.shape[0], wtc.shape[0], wtg.shape[0])
    n_cols = (wk.shape[1],)
    chunk_spec = lambda r: pl.BlockSpec((TM // ROWS, r, ROWS), lambda i: (i, 0, 0))
    return pl.pallas_call(
        _inproj_prompt_body,
        grid=(t // TM,),
        in_specs=([_row_spec(TM, D_MODEL), _const_spec((1, D_MODEL))]
                  + [_const_spec((r, D_MODEL)) for r in t_rows] + [_const_spec((D_MODEL, w)) for w in n_cols]),
        out_specs=[chunk_spec(r) for r in t_rows] + [_row_spec(TM, w) for w in n_cols],
        out_shape=([jax.ShapeDtypeStruct((t // ROWS, r, ROWS), F32) for r in t_rows]
                   + [jax.ShapeDtypeStruct((t, w), F32) for w in n_cols]),
        compiler_params=_dense_params(),
        name="inproj_prompt",
    )(x, g, wta, wtb, wtc, wtg, wk)


def _source_target_mask():
    src = lax.broadcasted_iota(jnp.int32, (ROWS, ROWS), 0)
    tgt = lax.broadcasted_iota(jnp.int32, (ROWS, ROWS), 1)
    return src <= tgt, src, tgt


def _last_lane(x):
    return jnp.broadcast_to(x[:, LANES - 1:LANES], x.shape)


def _mlstm_prompt_body(pt_ref, k_ref, gt_ref, gb_ref, normb_ref, out_ref, c_ref, n_ref, m_ref, ct_scr, *, nb):
    @pl.when(pl.program_id(1) == 0)
    def _init():
        ct_scr[...] = jnp.zeros_like(ct_scr)
        n_ref[...] = jnp.zeros_like(n_ref)
        m_ref[...] = jnp.zeros_like(m_ref)

    causal_t, _, _ = _source_target_mask()
    gates = [_mlstm_prompt_gates(gt_ref.at[sq], gb_ref, m_ref.at[sq], causal_t) for sq in range(nb)]
    units = [(sq, h) for sq in range(nb) for h in range(H_A)]

    st_raw, cq, qn, c_upd, n_upd, p_all = {}, {}, {}, {}, {}, {}
    for sq, h in units:
        c = GATE_F + h
        gs = gates[sq]
        qt = (pt_ref[sq, h * DH:(h + 1) * DH, :] * QK_SCALE).astype(BF16)
        vt = pt_ref[sq, GROUP_W + h * DH:GROUP_W + (h + 1) * DH, :]
        kf = k_ref[sq, :, h * DH:(h + 1) * DH]
        kb = kf.astype(BF16)
        n8 = jnp.concatenate([n_ref[sq, h:h + 1, :], jnp.zeros((SUBLANES - 1, DH), F32)], axis=0)
        st_raw[sq, h] = _dot(kb, qt)
        cq[sq, h] = _dot(ct_scr[sq, h].astype(BF16), qt)
        qn[sq, h] = _dot(n8.astype(BF16), qt)[0:1, :]
        c_upd[sq, h] = _dot((vt * gs["wend"][c:c + 1, :]).astype(BF16), kb)
        n_upd[sq, h] = _dot_exact(gs["wend"], kf)[c:c + 1, :]
        p_all[sq, h] = jnp.where(causal_t, jnp.exp(gs["ut"][:, c:c + 1] + gs["bm"][c:c + 1, :]), 0.0)

    num, den = {}, {}
    for sq, h in units:
        c = GATE_F + h
        gs = gates[sq]
        st = st_raw[sq, h] * p_all[sq, h]
        vt = pt_ref[sq, GROUP_W + h * DH:GROUP_W + (h + 1) * DH, :].astype(BF16)
        g = gs["gint"][c:c + 1, :]
        num[sq, h] = _dot(vt, st.astype(BF16)) + g * cq[sq, h]
        den[sq, h] = jnp.sum(st, axis=0, keepdims=True) + g * qn[sq, h]
        g1 = gs["gend"][c:c + 1, 0:1]
        ct_scr[sq, h] = g1 * ct_scr[sq, h] + c_upd[sq, h]
        n_ref[sq, h:h + 1, :] = g1 * n_ref[sq, h:h + 1, :] + n_upd[sq, h]

    for sq, h in units:
        c = GATE_F + h
        hs = slice(h * DH, (h + 1) * DH)
        ot = pt_ref[sq, 2 * GROUP_W + h * DH:2 * GROUP_W + (h + 1) * DH, :]
        hh = num[sq, h] * (1.0 / jnp.maximum(jnp.abs(den[sq, h]), gates[sq]["emt"][c:c + 1, :]))
        rs = lax.rsqrt(jnp.mean(hh * hh, axis=0, keepdims=True) + EPS)
        o = hh * rs * normb_ref[hs, :] * jax.nn.sigmoid(ot)
        out_ref[sq, :, hs] = o.T.astype(out_ref.dtype)

    @pl.when(pl.program_id(1) == pl.num_programs(1) - 1)
    def _finish():
        for sq in range(nb):
            for h in range(H_A):
                c_ref[sq, h] = ct_scr[sq, h].T


def _mlstm_prompt_gates(gt_ref, gb_ref, m_ref, causal_t):
    rowid = lax.broadcasted_iota(jnp.int32, (SUBLANES, LANES), 0)
    frows = rowid >= GATE_F

    pre = gt_ref[0:SUBLANES, :] + gb_ref[...]
    cap = GATE_SOFTCAP * jnp.tanh(pre / GATE_SOFTCAP)
    logf = jnp.minimum(cap, 0.0) - _log1p_exp_neg_abs(cap)
    bcum = _dot_exact(logf, jnp.where(causal_t, 1.0, 0.0))
    mprev = m_ref[...]
    inter = bcum + mprev
    u = pltpu.roll(cap, GATE_F - GATE_I, axis=0) - bcum
    ut = jnp.concatenate([u, jnp.zeros((ROWS - SUBLANES, LANES), F32)], axis=0).T
    mt_all = jnp.zeros((SUBLANES, LANES), F32)
    for h in range(H_A):
        c = GATE_F + h
        d = jnp.where(causal_t, ut[:, c:c + 1] + bcum[c:c + 1, :], -jnp.inf)
        mt = jnp.maximum(inter[c:c + 1, :], jnp.max(d, axis=0, keepdims=True))
        mt_all = jnp.where(rowid == c, mt, mt_all)
    blast = _last_lane(bcum)
    mend = _last_lane(mt_all)
    wend = jnp.where(frows, jnp.exp(blast + u - mend), 0.0)
    gend = jnp.where(frows, jnp.exp(blast + mprev - mend), 0.0)
    gint = jnp.where(frows, jnp.exp(inter - mt_all), 0.0)
    emt = jnp.exp(-mt_all)
    bm = bcum - mt_all
    m_ref[...] = jnp.where(frows, mend, 0.0)
    return dict(ut=ut, bm=bm, wend=wend, gend=gend, gint=gint, emt=emt)


def _mlstm_sample_body(*refs, lc, nseq):
    _mlstm_block(*refs[:7], *refs[8:], lc=lc, nseq=nseq)


def _mlstm_block(pa_ref, pg_ref, gb_ref, norm_ref, c0_ref, n0_ref, m0_ref,
                 out_ref, c_ref, n_ref, m_ref, qc_scr, kw_scr, g_scr, *, lc, nseq):
    rows = lc * nseq
    mprev = m0_ref[...]
    causal, _, _ = _causal_mask(rows, lc)
    lane = lax.broadcasted_iota(jnp.int32, (rows, LANES), 1)
    fcols = (lane >= GATE_F) & (lane < GATE_F + H_A)

    pre = pg_ref[...] + gb_ref[...]
    cap = GATE_SOFTCAP * jnp.tanh(pre / GATE_SOFTCAP)
    logf = jnp.minimum(cap, 0.0) - _log1p_exp_neg_abs(cap)
    bcum = _dot_exact(jnp.where(causal, 1.0, 0.0), logf)
    inter = bcum + mprev
    u = pltpu.roll(cap, GATE_F - GATE_I, axis=1) - bcum
    ut = u.T
    mt_all = jnp.zeros((rows, LANES), F32)
    for h in range(H_A):
        c = GATE_F + h
        d = jnp.where(causal, bcum[:, c:c + 1] + ut[c:c + 1, :], -jnp.inf)
        mt = jnp.maximum(inter[:, c:c + 1], jnp.max(d, axis=1, keepdims=True))
        mt_all = jnp.where(lane == c, mt, mt_all)
    blast = _group_last(bcum, lc)
    mend = _group_last(mt_all, lc)
    wend = jnp.where(fcols, jnp.exp(blast + u - mend), 0.0)
    gend = jnp.where(fcols, jnp.exp(blast + mprev - mend), 0.0)
    gint = jnp.where(fcols, jnp.exp(inter - mt_all), 0.0)
    emt = jnp.exp(-mt_all)
    bm = bcum - mt_all

    def head_cols(group, h):
        return slice(group * GROUP_W + h * DH, group * GROUP_W + (h + 1) * DH)

    g_scr[...] = gend
    for h in range(H_A):
        c = GATE_F + h
        kw_scr[h] = pa_ref[:, head_cols(1, h)] * wend[:, c:c + 1]

    def seq_body(j, carry):
        r0 = pl.multiple_of(j * lc, lc)
        grow = g_scr[pl.ds(r0, 1), :]
        for h in range(H_A):
            c = GATE_F + h
            qj = (pa_ref[pl.ds(r0, lc), head_cols(0, h)] * QK_SCALE).astype(BF16)
            vj = pa_ref[pl.ds(r0, lc), head_cols(2, h)].astype(BF16)
            kwj = kw_scr[h, pl.ds(r0, lc), :].astype(BF16)
            c_old = c0_ref[j, h]
            qc_scr[h, pl.ds(r0, lc), :] = _dot(qj, c_old.astype(BF16))
            c_ref[j, h] = grow[:, c:c + 1] * c_old + _dot_tn(kwj, vj)
        return carry

    lax.fori_loop(0, nseq, seq_body, 0, unroll=SEQ_UNROLL)
    n_rows = []
    gfirst = _group_first(gend, lc)
    for h in range(H_A):
        c = GATE_F + h
        n_old = n0_ref[:, h, :]
        n_rows.append(_group_bcast(n_old, lc, rows))
        n_ref[:, h, :] = gfirst[:, c:c + 1] * n_old + _group_sum(kw_scr[h], lc)
    m_ref[...] = _group_first(mend, lc)

    scores = []
    for h in range(H_A):
        qb = (pa_ref[:, head_cols(0, h)] * QK_SCALE).astype(BF16)
        scores.append(_dot_nt(qb, pa_ref[:, head_cols(1, h)].astype(BF16)))
    for h in range(H_A):
        c = GATE_F + h
        qf = pa_ref[:, head_cols(0, h)] * QK_SCALE
        v = pa_ref[:, head_cols(2, h)].astype(BF16)
        og = pa_ref[:, head_cols(3, h)]
        p = jnp.where(causal, jnp.exp(bm[:, c:c + 1] + ut[c:c + 1, :]), 0.0)
        s = scores[h] * p
        g = gint[:, c:c + 1]
        num = _dot(s.astype(BF16), v) + g * qc_scr[h]
        qn = jnp.sum(qf * n_rows[h], axis=1, keepdims=True)
        den = jnp.sum(s, axis=1, keepdims=True) + g * qn
        hh = num / jnp.maximum(jnp.abs(den), emt[:, c:c + 1])
        hn = hh * lax.rsqrt(jnp.mean(hh * hh, axis=1, keepdims=True) + EPS) * norm_ref[:, h * DH:(h + 1) * DH]
        out_ref[:, h * DH:(h + 1) * DH] = (hn * jax.nn.sigmoid(og)).astype(out_ref.dtype)


def _mixer_params(ndims):
    return pltpu.CompilerParams(dimension_semantics=("arbitrary",) * ndims, vmem_limit_bytes=VMEM_LIMIT)


def _seq_spec(cols):
    return pl.BlockSpec((SEQ_PER_STEP, ROWS, cols), lambda b, c: (b, c, 0))


def _seq_state_spec(shape):
    nd = len(shape)
    return pl.BlockSpec((SEQ_PER_STEP,) + shape, lambda b, c: (b,) + (0,) * nd)


def _seq_t_spec(rows):
    return pl.BlockSpec((SEQ_PER_STEP, None, rows, ROWS), lambda b, c: (b, c, 0, 0))


def _seq_col_spec(cols, col_block):
    return pl.BlockSpec((SEQ_PER_STEP, ROWS, cols), lambda b, c: (b, c, col_block))


def _mlstm_prompt(pt, pk, gt, gb8, normb, bsz, nchunk):
    seq = nchunk * ROWS
    const = lambda b, c: (0, 0)
    out, c1, n1, m1 = pl.pallas_call(
        functools.partial(_mlstm_prompt_body, nb=SEQ_PER_STEP),
        grid=(bsz // SEQ_PER_STEP, nchunk),
        in_specs=[_seq_t_spec(3 * GROUP_W), _seq_col_spec(GROUP_W, 0), _seq_t_spec(GATE_ROWS),
                  pl.BlockSpec((SUBLANES, LANES), const), pl.BlockSpec((GROUP_W, LANES), const)],
        out_specs=[_seq_spec(GROUP_W), _seq_state_spec((H_A, DH, DH)), _seq_state_spec((H_A, DH)),
                   _seq_state_spec((SUBLANES, LANES))],
        out_shape=[jax.ShapeDtypeStruct((bsz, seq, GROUP_W), BF16),
                   jax.ShapeDtypeStruct((bsz, H_A, DH, DH), F32),
                   jax.ShapeDtypeStruct((bsz, H_A, DH), F32),
                   jax.ShapeDtypeStruct((bsz, SUBLANES, LANES), F32)],
        scratch_shapes=[pltpu.VMEM((SEQ_PER_STEP, H_A, DH, DH), F32)],
        compiler_params=_mixer_params(2),
        name="mlstm_prompt",
    )(pt.reshape(bsz, nchunk, 3 * GROUP_W, ROWS), pk.reshape(bsz, seq, -1),
      gt.reshape(bsz, nchunk, GATE_ROWS, ROWS), gb8, normb)
    return out.reshape(bsz * seq, GROUP_W), c1, n1, m1


def _layer_state_spec(shape, layer):
    nd = len(shape)
    return pl.BlockSpec((None,) + shape, lambda i: (layer, i) + (0,) * (nd - 1))


_STACK_SPEC = pl.BlockSpec(memory_space=pl.ANY)


def _mlstm_sample(pa, pg, gbias, norm, c_all, n_all, m0rows, lc, layer, c_prev):
    t = pa.shape[0]
    nseq = ROWS // lc
    bsz = c_all.shape[1]
    row = lambda i: (i, 0)
    const = lambda i: (0, 0)
    return pl.pallas_call(
        functools.partial(_mlstm_sample_body, lc=lc, nseq=nseq),
        grid=(t // ROWS,),
        in_specs=[pl.BlockSpec((ROWS, 4 * GROUP_W), row), pl.BlockSpec((ROWS, LANES), row),
                  pl.BlockSpec((1, LANES), const), pl.BlockSpec((1, GROUP_W), const),
                  _layer_state_spec((nseq, H_A, DH, DH), layer),
                  _layer_state_spec((nseq, H_A, DH), layer),
                  pl.BlockSpec((ROWS, LANES), row), _STACK_SPEC],
        out_specs=[pl.BlockSpec((ROWS, GROUP_W), row),
                   _layer_state_spec((nseq, H_A, DH, DH), layer),
                   pl.BlockSpec((nseq, H_A, DH), lambda i: (i, 0, 0)),
                   pl.BlockSpec((nseq, LANES), row)],
        out_shape=[jax.ShapeDtypeStruct((t, GROUP_W), BF16),
                   jax.ShapeDtypeStruct(c_all.shape, F32),
                   jax.ShapeDtypeStruct((bsz, H_A, DH), F32),
                   jax.ShapeDtypeStruct((bsz, LANES), F32)],
        input_output_aliases={7: 1},
        scratch_shapes=[pltpu.VMEM((H_A, ROWS, DH), F32), pltpu.VMEM((H_A, ROWS, DH), F32),
                        pltpu.VMEM((ROWS, LANES), F32)],
        compiler_params=_mixer_params(1),
        name="mlstm_sample",
    )(pa, pg, gbias, norm, c_all, n_all, m0rows, c_prev)


LOG_GAMMA = [math.log(1.0 - 2.0 ** (-5.0 - h)) for h in range(H_C)]


def _ret_prompt_body(pt_ref, k_ref, cost_ref, sint_ref, cos_ref, sin_ref, dec_ref, normb_ref, out_ref, s_ref,
                     st_scr, *, nb):
    @pl.when(pl.program_id(1) == 0)
    def _init():
        st_scr[...] = jnp.zeros_like(st_scr)

    tin = lax.broadcasted_iota(jnp.int32, (1, ROWS), 1).astype(F32)
    cost, sint = cost_ref[...], sint_ref[...]
    cos, sin = cos_ref[...], sin_ref[...]
    units = [(sq, h) for sq in range(nb) for h in range(H_C)]

    st_raw, qs, s_upd = {}, {}, {}
    for sq, h in units:
        lg = LOG_GAMMA[h]
        qt = pt_ref[sq, h * DH:(h + 1) * DH, :]
        vt = pt_ref[sq, GROUP_W + h * DH:GROUP_W + (h + 1) * DH, :]
        kf = k_ref[sq, :, h * DH:(h + 1) * DH]
        qr = (qt * cost + pltpu.roll(qt, DH // 2, axis=0) * sint).astype(BF16)
        kr = ((kf * cos + pltpu.roll(kf, DH // 2, axis=1) * sin) * QK_SCALE).astype(BF16)
        ve = (vt * jnp.exp((ROWS - 1.0 - tin) * lg)).astype(BF16)
        st_raw[sq, h] = _dot(kr, qr)
        qs[sq, h] = _dot(st_scr[sq, h].astype(BF16), qr)
        s_upd[sq, h] = _dot(ve, kr)

    o_all = {}
    for sq, h in units:
        lg = LOG_GAMMA[h]
        vt = pt_ref[sq, GROUP_W + h * DH:GROUP_W + (h + 1) * DH, :].astype(BF16)
        st = (st_raw[sq, h] * dec_ref[h]).astype(BF16)
        o_all[sq, h] = _dot(vt, st) + jnp.exp((tin + 1.0) * lg) * qs[sq, h]
        st_scr[sq, h] = math.exp(ROWS * lg) * st_scr[sq, h] + s_upd[sq, h]

    for sq, h in units:
        hs = slice(h * DH, (h + 1) * DH)
        gt = pt_ref[sq, 2 * GROUP_W + h * DH:2 * GROUP_W + (h + 1) * DH, :]
        o = o_all[sq, h]
        rs = lax.rsqrt(jnp.mean(o * o, axis=0, keepdims=True) + EPS)
        on = o * rs * normb_ref[hs, :] * _silu(gt)
        out_ref[sq, :, hs] = on.T.astype(out_ref.dtype)

    @pl.when(pl.program_id(1) == pl.num_programs(1) - 1)
    def _finish():
        for sq in range(nb):
            for h in range(H_C):
                s_ref[sq, h] = st_scr[sq, h].T


def _ret_sample_body(*refs, lc, nseq):
    _ret_block(*refs[:5], *refs[6:], lc=lc, nseq=nseq)


def _ret_block(pc_ref, cos_ref, sin_ref, norm_ref, s0_ref, out_ref, s_ref, qs_scr, qr_scr, ke_scr, *, lc, nseq):
    rows = lc * nseq
    causal, r, c = _causal_mask(rows, lc)
    diff = (r - c).astype(F32)
    tin = (lax.broadcasted_iota(jnp.int32, (rows, 1), 0) & (lc - 1)).astype(F32)
    cos = cos_ref[...]
    sin = sin_ref[...]

    def head_cols(group, h):
        return slice(group * GROUP_W + h * DH, group * GROUP_W + (h + 1) * DH)

    def rot(x):
        return x * cos + pltpu.roll(x, DH // 2, axis=1) * sin

    for h in range(H_C):
        qr_scr[h] = rot(pc_ref[:, head_cols(0, h)])
        ke_scr[h] = rot(pc_ref[:, head_cols(1, h)]) * QK_SCALE * jnp.exp((lc - 1.0 - tin) * LOG_GAMMA[h])

    def seq_body(j, carry):
        r0 = pl.multiple_of(j * lc, lc)
        for h in range(H_C):
            qj = qr_scr[h, pl.ds(r0, lc), :].astype(BF16)
            kj = ke_scr[h, pl.ds(r0, lc), :].astype(BF16)
            vj = pc_ref[pl.ds(r0, lc), head_cols(2, h)].astype(BF16)
            s_old = s0_ref[j, h]
            qs_scr[h, pl.ds(r0, lc), :] = _dot(qj, s_old.astype(BF16))
            s_ref[j, h] = math.exp(lc * LOG_GAMMA[h]) * s_old + _dot_tn(kj, vj)
        return carry

    lax.fori_loop(0, nseq, seq_body, 0, unroll=SEQ_UNROLL)

    scores = []
    for h in range(H_C):
        kr = (rot(pc_ref[:, head_cols(1, h)]) * QK_SCALE).astype(BF16)
        scores.append(_dot_nt(qr_scr[h].astype(BF16), kr))
    for h in range(H_C):
        lg = LOG_GAMMA[h]
        decay = jnp.where(causal, jnp.exp(diff * lg), 0.0)
        v = pc_ref[:, head_cols(2, h)].astype(BF16)
        gate = pc_ref[:, head_cols(3, h)]
        o = _dot((scores[h] * decay).astype(BF16), v) + jnp.exp((tin + 1.0) * lg) * qs_scr[h]
        on = o * lax.rsqrt(jnp.mean(o * o, axis=1, keepdims=True) + EPS) * norm_ref[:, h * DH:(h + 1) * DH]
        out_ref[:, h * DH:(h + 1) * DH] = (on * _silu(gate)).astype(out_ref.dtype)


def _ret_prompt(pt, pk, cos, sin, normb, bsz, nchunk):
    seq = nchunk * ROWS
    chunk = lambda b, c: (c, 0)
    chunk_t = lambda b, c: (0, c)
    const = lambda b, c: (0, 0)
    idx = jnp.arange(ROWS, dtype=F32)
    diff = idx[None, :] - idx[:, None]
    log_gamma = jnp.asarray(LOG_GAMMA, F32)[:, None, None]
    dec = jnp.where(diff >= 0, jnp.exp(diff * log_gamma), 0.0)
    out, s1 = pl.pallas_call(
        functools.partial(_ret_prompt_body, nb=SEQ_PER_STEP),
        grid=(bsz // SEQ_PER_STEP, nchunk),
        in_specs=[_seq_t_spec(3 * GROUP_W), _seq_col_spec(GROUP_W, 1),
                  pl.BlockSpec((DH, ROWS), chunk_t), pl.BlockSpec((DH, ROWS), chunk_t),
                  pl.BlockSpec((ROWS, DH), chunk), pl.BlockSpec((ROWS, DH), chunk),
                  pl.BlockSpec((H_C, ROWS, ROWS), lambda b, c: (0, 0, 0)),
                  pl.BlockSpec((GROUP_W, LANES), const)],
        out_specs=[_seq_spec(GROUP_W), _seq_state_spec((H_C, DH, DH))],
        out_shape=[jax.ShapeDtypeStruct((bsz, seq, GROUP_W), BF16),
                   jax.ShapeDtypeStruct((bsz, H_C, DH, DH), F32)],
        scratch_shapes=[pltpu.VMEM((SEQ_PER_STEP, H_C, DH, DH), F32)],
        compiler_params=_mixer_params(2),
        name="ret_prompt",
    )(pt.reshape(bsz, nchunk, 3 * GROUP_W, ROWS), pk.reshape(bsz, seq, -1), cos.T, sin.T, cos, sin, dec, normb)
    return out.reshape(bsz * seq, GROUP_W), s1


def _ret_sample(pc, cos, sin, norm, s_all, lc, layer, s_prev):
    t = pc.shape[0]
    nseq = ROWS // lc
    row = lambda i: (i, 0)
    const = lambda i: (0, 0)
    return pl.pallas_call(
        functools.partial(_ret_sample_body, lc=lc, nseq=nseq),
        grid=(t // ROWS,),
        in_specs=[pl.BlockSpec((ROWS, 4 * GROUP_W), row), pl.BlockSpec((ROWS, DH), const),
                  pl.BlockSpec((ROWS, DH), const), pl.BlockSpec((1, GROUP_W), const),
                  _layer_state_spec((nseq, H_C, DH, DH), layer), _STACK_SPEC],
        out_specs=[pl.BlockSpec((ROWS, GROUP_W), row),
                   _layer_state_spec((nseq, H_C, DH, DH), layer)],
        out_shape=[jax.ShapeDtypeStruct((t, GROUP_W), BF16),
                   jax.ShapeDtypeStruct(s_all.shape, F32)],
        input_output_aliases={5: 1},
        scratch_shapes=[pltpu.VMEM((H_C, ROWS, DH), F32)] * 3,
        compiler_params=_mixer_params(1),
        name="ret_sample",
    )(pc, cos, sin, norm, s_all, s_prev)


HEADS_PER_GROUP = H_B // G_B
PAIR_W = 2 * P_B
GROUP_CH = HEADS_PER_GROUP * P_B


def _ssd_prompt_body(pt_ref, gt_ref, gb_ref, alog_ref, dskipb_ref, cwb_ref, cbb_ref, normb_ref,
                     out_ref, h_ref, tail_ref, xc_scr, *, nb):
    @pl.when(pl.program_id(1) == 0)
    def _init():
        h_ref[...] = jnp.zeros_like(h_ref)
        tail_ref[...] = jnp.zeros_like(tail_ref)

    causal_t, _, _ = _source_target_mask()
    lane = lax.broadcasted_iota(jnp.int32, (DH, ROWS), 1)
    rowid = lax.broadcasted_iota(jnp.int32, (GATE_ROWS, LANES), 0)
    drows = rowid >= GATE_DT
    pad = jnp.zeros((ROWS - GATE_ROWS, LANES), F32)

    gates = []
    for sq in range(nb):
        for blk in range(CONV_DIM // DH):
            ch = slice(blk * DH, (blk + 1) * DH)
            new = pt_ref[sq, GROUP_W + blk * DH:GROUP_W + (blk + 1) * DH, :]
            prev = tail_ref[sq, ch, :]
            acc = new * cwb_ref[CONV_W - 1, ch, :]
            for k in range(1, CONV_W):
                shifted = pltpu.roll(jnp.where(lane >= ROWS - k, prev, new), k, axis=1)
                acc = acc + shifted * cwb_ref[CONV_W - 1 - k, ch, :]
            xc_scr[sq, ch, :] = _silu(acc + cbb_ref[ch, :])
            tail_ref[sq, ch, :] = new

        dpre = gt_ref[sq] + gb_ref[...]
        dt = jnp.maximum(dpre, 0.0) + _log1p_exp_neg_abs(dpre)
        adt = dt * (-jnp.exp(alog_ref[...]))
        acum = jnp.where(drows, _dot_exact(adt, jnp.where(causal_t, 1.0, 0.0)), 0.0)
        alast = _last_lane(acum)
        gates.append(dict(dt=dt, acum=acum, at=jnp.concatenate([acum, pad], axis=0).T, exp_a=jnp.exp(acum),
                          wx=jnp.exp(alast - acum), g_a=jnp.exp(alast)))

    def x_rows(hd):
        return slice(hd * P_B, (hd + 1) * P_B)

    def b_rows(g):
        return slice(GROUP_W + g * N_B, GROUP_W + (g + 1) * N_B)

    def c_rows(g):
        return slice(GROUP_W + G_B * N_B + g * N_B, GROUP_W + G_B * N_B + (g + 1) * N_B)

    groups = [(sq, g) for sq in range(nb) for g in range(G_B)]
    cbs, chs, upds, xdts = {}, {}, {}, {}
    for sq, g in groups:
        gs = gates[sq]
        bg = xc_scr[sq, b_rows(g), :].T.astype(BF16)
        ct = xc_scr[sq, c_rows(g), :].astype(BF16)
        xws = []
        for r in range(HEADS_PER_GROUP):
            hd = g * HEADS_PER_GROUP + r
            c = GATE_DT + hd
            xdt = xc_scr[sq, x_rows(hd), :] * gs["dt"][c:c + 1, :]
            xdts[sq, hd] = xdt.astype(BF16)
            xws.append((xdt * gs["wx"][c:c + 1, :]).astype(BF16))
        h_old = jnp.concatenate([h_ref[sq, g * HEADS_PER_GROUP + r] for r in range(HEADS_PER_GROUP)], axis=0)
        cbs[sq, g] = _dot(bg, ct)
        chs[sq, g] = _dot(h_old.astype(BF16), ct)
        upds[sq, g] = _dot(jnp.concatenate(xws, axis=0), bg)

    ys = {}
    for sq, g in groups:
        gs = gates[sq]
        for r in range(HEADS_PER_GROUP):
            hd = g * HEADS_PER_GROUP + r
            c = GATE_DT + hd
            dec = jnp.where(causal_t, jnp.exp(gs["acum"][c:c + 1, :] - gs["at"][:, c:c + 1]), 0.0)
            ys[sq, hd] = _dot(xdts[sq, hd], (cbs[sq, g] * dec).astype(BF16))
            h_ref[sq, hd] = gs["g_a"][c:c + 1, 0:1] * h_ref[sq, hd] + upds[sq, g][r * P_B:(r + 1) * P_B, :]

    for sq, g in groups:
        gs = gates[sq]
        parts = []
        for r in range(HEADS_PER_GROUP):
            hd = g * HEADS_PER_GROUP + r
            c = GATE_DT + hd
            y = ys[sq, hd] + gs["exp_a"][c:c + 1, :] * chs[sq, g][r * P_B:(r + 1) * P_B, :]
            parts.append(y + dskipb_ref[x_rows(hd), :] * xc_scr[sq, x_rows(hd), :])
        grows = slice(g * GROUP_CH, (g + 1) * GROUP_CH)
        yz = jnp.concatenate(parts, axis=0) * _silu(pt_ref[sq, grows, :])
        yn = yz * lax.rsqrt(jnp.mean(yz * yz, axis=0, keepdims=True) + EPS) * normb_ref[grows, :]
        for j in range(GROUP_CH // DH):
            cols = slice(g * GROUP_CH + j * DH, g * GROUP_CH + (j + 1) * DH)
            out_ref[sq, :, cols] = yn[j * DH:(j + 1) * DH, :].T.astype(out_ref.dtype)


def _ssd_sample_body(*refs, lc, nseq):
    _ssd_block(*refs[:10], *refs[11:], lc=lc, nseq=nseq)


def _ssd_block(pb_ref, pg_ref, gb_ref, alog_ref, dskip_ref, cw_ref, cb_ref, norm_ref, h0_ref, hist_ref,
               out_ref, h_ref, buf_ref, ch_scr, xc_scr, xw_scr, g_scr, *, lc, nseq):
    rows = lc * nseq
    causal, _, _ = _causal_mask(rows, lc)
    lane = lax.broadcasted_iota(jnp.int32, (rows, LANES), 1)
    dcols = (lane >= GATE_DT) & (lane < GATE_DT + H_B)
    low = lax.broadcasted_iota(jnp.int32, (rows, PAIR_W), 1) < P_B

    new = pb_ref[:, GROUP_W:GROUP_W + CONV_DIM]
    tin = lax.broadcasted_iota(jnp.int32, (rows, 1), 0) & (lc - 1)
    acc = new * cw_ref[CONV_W - 1:CONV_W, :]
    for k in range(1, CONV_W):
        rolled = pltpu.roll(new, k, axis=0)
        hist = pltpu.roll(hist_ref[...], (rows + k - lc) % rows, axis=0)
        acc = acc + jnp.where(tin >= k, rolled, hist) * cw_ref[CONV_W - 1 - k:CONV_W - k, :]
    xc = _silu(acc + cb_ref[...])
    buf_ref[...] = new.reshape(nseq, lc, CONV_DIM)[:, lc - (CONV_W - 1):lc, :]

    dpre = pg_ref[...] + gb_ref[...]
    dt = jnp.maximum(dpre, 0.0) + _log1p_exp_neg_abs(dpre)
    adt = dt * (-jnp.exp(alog_ref[...]))
    acum = jnp.where(dcols, _dot_exact(jnp.where(causal, 1.0, 0.0), adt), 0.0)
    at = acum.T
    alast = _group_last(acum, lc)
    exp_a = jnp.exp(acum)
    wx = jnp.exp(alast - acum)
    g_a = jnp.exp(alast)

    def pair_bcast(slab, c0):
        return jnp.where(low, slab[:, c0:c0 + 1], slab[:, c0 + 1:c0 + 2])

    def pair_cols(g, p):
        start = g * GROUP_CH + p * PAIR_W
        return slice(start, start + PAIR_W)

    def b_cols(g):
        return slice(GROUP_W + g * N_B, GROUP_W + (g + 1) * N_B)

    def c_cols(g):
        return slice(GROUP_W + G_B * N_B + g * N_B, GROUP_W + G_B * N_B + (g + 1) * N_B)

    xw_pairs = {}
    xdt_pairs = {}
    for g in range(G_B):
        for p in range(HEADS_PER_GROUP // 2):
            c0 = GATE_DT + g * HEADS_PER_GROUP + 2 * p
            xdt = xc[:, pair_cols(g, p)] * pair_bcast(dt, c0)
            xdt_pairs[g, p] = xdt.astype(BF16)
            xw_pairs[g, p] = xdt * pair_bcast(wx, c0)
    xc_scr[...] = xc
    g_scr[...] = g_a
    for g in range(G_B):
        for p in range(HEADS_PER_GROUP // 2):
            xw_scr[:, pair_cols(g, p)] = xw_pairs[g, p]

    def seq_body(j, carry):
        r0 = pl.multiple_of(j * lc, lc)
        grow = g_scr[pl.ds(r0, 1), :]
        for g in range(G_B):
            bj = xc_scr[pl.ds(r0, lc), b_cols(g)].astype(BF16)
            cj = xc_scr[pl.ds(r0, lc), c_cols(g)].astype(BF16)
            xwj = xw_scr[pl.ds(r0, lc), g * GROUP_CH:(g + 1) * GROUP_CH].astype(BF16)
            hs = [h0_ref[j, g * HEADS_PER_GROUP + r] for r in range(HEADS_PER_GROUP)]
            h_old = jnp.concatenate(hs, axis=0)
            ch_scr[g, pl.ds(r0, lc), :] = _dot_nt(cj, h_old.astype(BF16))
            upd = _dot_tn(xwj, bj)
            for r in range(HEADS_PER_GROUP):
                hd = g * HEADS_PER_GROUP + r
                c = GATE_DT + hd
                h_ref[j, hd] = grow[:, c:c + 1] * hs[r] + upd[r * P_B:(r + 1) * P_B, :]
        return carry

    lax.fori_loop(0, nseq, seq_body, 0, unroll=SEQ_UNROLL)

    cbs = [_dot_nt(xc[:, c_cols(g)].astype(BF16), xc[:, b_cols(g)].astype(BF16)) for g in range(G_B)]
    for g in range(G_B):
        cb = cbs[g]
        ys = []
        for p in range(HEADS_PER_GROUP // 2):
            c0 = GATE_DT + g * HEADS_PER_GROUP + 2 * p
            xpair = xc[:, pair_cols(g, p)]
            xdt = xdt_pairs[g, p]
            halves = []
            for c in (c0, c0 + 1):
                dec = jnp.where(causal, jnp.exp(acum[:, c:c + 1] - at[c:c + 1, :]), 0.0)
                halves.append(_dot((cb * dec).astype(BF16), xdt))
            y = (jnp.where(low, halves[0], halves[1])
                 + pair_bcast(exp_a, c0) * ch_scr[g, :, p * PAIR_W:(p + 1) * PAIR_W])
            ys.append(y + dskip_ref[:, pair_cols(g, p)] * xpair)
        yg = jnp.concatenate(ys, axis=1)
        gcols = slice(g * GROUP_CH, (g + 1) * GROUP_CH)
        yz = yg * _silu(pb_ref[:, gcols])
        yn = yz * lax.rsqrt(jnp.mean(yz * yz, axis=1, keepdims=True) + EPS) * norm_ref[:, gcols]
        out_ref[:, gcols] = yn.astype(out_ref.dtype)


def _ssd_prompt(pt, gt, gb16, alog16, dskipb, cwb, cbb, normb, bsz, nchunk):
    seq = nchunk * ROWS
    const = lambda b, c: (0, 0)
    out, h1, tail = pl.pallas_call(
        functools.partial(_ssd_prompt_body, nb=SEQ_PER_STEP),
        grid=(bsz // SEQ_PER_STEP, nchunk),
        in_specs=[_seq_t_spec(GROUP_W + CONV_DIM), _seq_t_spec(GATE_ROWS),
                  pl.BlockSpec((GATE_ROWS, LANES), const), pl.BlockSpec((GATE_ROWS, LANES), const),
                  pl.BlockSpec((GROUP_W, LANES), const),
                  pl.BlockSpec((CONV_W, CONV_DIM, LANES), lambda b, c: (0, 0, 0)),
                  pl.BlockSpec((CONV_DIM, LANES), const), pl.BlockSpec((GROUP_W, LANES), const)],
        out_specs=[_seq_spec(GROUP_W), _seq_state_spec((H_B, P_B, N_B)), _seq_state_spec((CONV_DIM, ROWS))],
        out_shape=[jax.ShapeDtypeStruct((bsz, seq, GROUP_W), BF16),
                   jax.ShapeDtypeStruct((bsz, H_B, P_B, N_B), F32),
                   jax.ShapeDtypeStruct((bsz, CONV_DIM, ROWS), F32)],
        scratch_shapes=[pltpu.VMEM((SEQ_PER_STEP, CONV_DIM, ROWS), F32)],
        compiler_params=_mixer_params(2),
        name="ssd_prompt",
    )(pt.reshape(bsz, nchunk, GROUP_W + CONV_DIM, ROWS), gt.reshape(bsz, nchunk, GATE_ROWS, ROWS),
      gb16, alog16, dskipb, cwb, cbb, normb)
    return out.reshape(bsz * seq, GROUP_W), h1, tail


def _ssd_sample(pb, pg, gbias, alog, dskip, cw, cb, norm, h_all, hist, lc, layer, h_prev):
    t = pb.shape[0]
    nseq = ROWS // lc
    bsz = h_all.shape[1]
    row = lambda i: (i, 0)
    const = lambda i: (0, 0)
    return pl.pallas_call(
        functools.partial(_ssd_sample_body, lc=lc, nseq=nseq),
        grid=(t // ROWS,),
        in_specs=[pl.BlockSpec((ROWS, GROUP_W + CONV_DIM), row), pl.BlockSpec((ROWS, LANES), row),
                  pl.BlockSpec((1, LANES), const), pl.BlockSpec((1, LANES), const),
                  pl.BlockSpec((1, GROUP_W), const), pl.BlockSpec((CONV_W, CONV_DIM), const),
                  pl.BlockSpec((1, CONV_DIM), const), pl.BlockSpec((1, GROUP_W), const),
                  _layer_state_spec((nseq, H_B, P_B, N_B), layer),
                  pl.BlockSpec((ROWS, CONV_DIM), row), _STACK_SPEC],
        out_specs=[pl.BlockSpec((ROWS, GROUP_W), row),
                   _layer_state_spec((nseq, H_B, P_B, N_B), layer),
                   pl.BlockSpec((nseq, CONV_W - 1, CONV_DIM), lambda i: (i, 0, 0))],
        out_shape=[jax.ShapeDtypeStruct((t, GROUP_W), BF16),
                   jax.ShapeDtypeStruct(h_all.shape, F32),
                   jax.ShapeDtypeStruct((bsz, CONV_W - 1, CONV_DIM), F32)],
        input_output_aliases={10: 1},
        scratch_shapes=[pltpu.VMEM((G_B, ROWS, GROUP_CH), F32), pltpu.VMEM((ROWS, CONV_DIM), F32),
                        pltpu.VMEM((ROWS, GROUP_W), F32), pltpu.VMEM((ROWS, LANES), F32)],
        compiler_params=_mixer_params(1),
        name="ssd_sample",
    )(pb, pg, gbias, alog, dskip, cw, cb, norm, h_all, hist, h_prev)


def _rope_tables(pos):
    half = DH // 2
    freqs = ROPE_BASE ** (-jnp.arange(half, dtype=F32) / half)
    ang = pos.astype(F32)[:, None] * freqs
    cos, sin = jnp.cos(ang), jnp.sin(ang)
    return jnp.concatenate([cos, cos], axis=-1), jnp.concatenate([-sin, sin], axis=-1)


def _pad_lanes(parts):
    row = jnp.concatenate([p.astype(F32) for p in parts])
    return jnp.pad(row, (0, LANES - row.shape[0]))[None, :]


def kernel(x_prompt, x_sample, state_mlstm_C, state_mlstm_n, state_mlstm_m, state_ssd, state_conv, state_ret,
           ffn1_norm, ffn1_w1, ffn1_w3, ffn1_w2, mix_norm, w_in, b_igate, b_fgate, mlstm_norm,
           conv_w, conv_b, dt_bias, a_log, d_skip, ssd_norm, ret_norm, w_out,
           ffn2_norm, ffn2_w1, ffn2_w3, ffn2_w2, final_norm):
    depth = w_in.shape[0]
    bsz, seq, _ = x_prompt.shape
    dbsz, dseq, _ = x_sample.shape
    assert seq % CHUNK == 0 and ROWS % dseq == 0 and (dbsz * dseq) % ROWS == 0
    assert dseq >= SUBLANES and dseq & (dseq - 1) == 0
    nchunk = seq // CHUNK
    seq_per_block = ROWS // dseq

    xp = x_prompt.reshape(bsz * seq, D_MODEL)
    xs = x_sample.reshape(dbsz * dseq, D_MODEL)

    cos_p, sin_p = _rope_tables(jnp.arange(seq))
    cos_s, sin_s = _rope_tables(PAST_LEN + jnp.arange(dseq))
    cos_s, sin_s = jnp.tile(cos_s, (seq_per_block, 1)), jnp.tile(sin_s, (seq_per_block, 1))

    a0 = 0
    a_gate = a0 + 4 * GROUP_W
    b0 = a_gate + 2 * H_A
    b_dt = b0 + GROUP_W + CONV_DIM
    c0 = b_dt + H_B

    outs_p = [[] for _ in range(6)]
    outs_s = [[] for _ in range(3)]
    c_stack = jnp.zeros(state_mlstm_C.shape, F32)
    h_stack = jnp.zeros(state_ssd.shape, F32)
    s_stack = jnp.zeros(state_ret.shape, F32)
    for l in range(depth):
        w1a, w3a, w2a = ffn1_w1[l].astype(BF16), ffn1_w3[l].astype(BF16), ffn1_w2[l].astype(BF16)
        w1b, w3b, w2b = ffn2_w1[l].astype(BF16), ffn2_w3[l].astype(BF16), ffn2_w2[l].astype(BF16)
        wi = w_in[l]
        wa = wi[:, a0:a_gate].astype(BF16)
        wb = wi[:, b0:b_dt].astype(BF16)
        wc = wi[:, c0:c0 + 4 * GROUP_W].astype(BF16)
        wg = jnp.concatenate([wi[:, a_gate:b0], wi[:, b_dt:c0]], axis=1)
        wg = jnp.pad(wg, ((0, 0), (0, LANES - wg.shape[1]))).astype(BF16)
        wo = w_out[l].astype(BF16)
        g1, gm, g2 = ffn1_norm[l][None, :], mix_norm[l][None, :], ffn2_norm[l][None, :]
        gbias = _pad_lanes([b_igate[l], b_fgate[l], dt_bias[l]])
        alog = _pad_lanes([jnp.zeros((GATE_DT,), F32), a_log[l]])
        dskip = jnp.repeat(d_skip[l].astype(F32), P_B)[None, :]
        na, nb, nc = mlstm_norm[l][None, :], ssd_norm[l][None, :], ret_norm[l][None, :]
        cw, cb = conv_w[l], conv_b[l][None, :]
        final = l == depth - 1
        fin = final_norm[None, :]

        lanes_bcast = lambda v: jnp.broadcast_to(v.astype(F32)[..., None], v.shape + (LANES,))
        qa, ka, va, ga = (wi[:, a0 + i * GROUP_W:a0 + (i + 1) * GROUP_W] for i in range(4))
        qc, kc, vc, gc = (wi[:, c0 + i * GROUP_W:c0 + (i + 1) * GROUP_W] for i in range(4))
        wta = jnp.concatenate([qa, va, ga], axis=1).T.astype(BF16)
        wtb = wb.T
        wtc = jnp.concatenate([qc, vc, gc], axis=1).T.astype(BF16)
        wtg = jnp.concatenate([wi[:, a_gate:b0], wi[:, b_dt:c0]], axis=1).T.astype(BF16)
        wk = jnp.concatenate([ka, kc], axis=1).astype(BF16)
        gb8 = lanes_bcast(jnp.concatenate([b_igate[l], b_fgate[l]]))
        gb16 = lanes_bcast(jnp.concatenate([jnp.zeros((GATE_DT,), F32), dt_bias[l].astype(F32)]))
        alog16 = lanes_bcast(jnp.concatenate([jnp.zeros((GATE_DT,), F32), a_log[l].astype(F32)]))
        nab, nbb, ncb = lanes_bcast(mlstm_norm[l]), lanes_bcast(ssd_norm[l]), lanes_bcast(ret_norm[l])
        xp = _ffn_call(xp, g1, w1a, w3a, w2a)
        pta, ptb, ptc, gt, pk = _inproj_prompt_call(xp, gm, wta, wtb, wtc, wtg, wk)
        oa, c1, n1, m1 = _mlstm_prompt(pta, pk, gt, gb8, nab, bsz, nchunk)
        ob, h1, tail = _ssd_prompt(ptb, gt, gb16, alog16, lanes_bcast(dskip[0]), lanes_bcast(cw), lanes_bcast(cb[0]),
                                   nbb, bsz, nchunk)
        oc, s1 = _ret_prompt(ptc, pk, cos_p, sin_p, ncb, bsz, nchunk)
        xp = _mix_ffn_call(xp, oa, ob, oc, wo, g2, w1b, w3b, w2b, fin, final)
        buf1 = jnp.swapaxes(tail[:, :, ROWS - (CONV_W - 1):], 1, 2)
        for acc, v in zip(outs_p, (c1, n1, m1[:, GATE_F:GATE_F + H_A, 0], h1, buf1, s1)):
            acc.append(v)

        m0rows = jnp.pad(jnp.repeat(state_mlstm_m[l].astype(F32), dseq, axis=0),
                         ((0, 0), (GATE_F, LANES - GATE_F - H_A)))
        hist = jnp.pad(state_conv[l].astype(F32), ((0, 0), (dseq - (CONV_W - 1), 0), (0, 0)))
        hist = hist.reshape(dbsz * dseq, CONV_DIM)
        xs = _ffn_call(xs, g1, w1a, w3a, w2a)
        pa, pb, pc, pg = _inproj_call(xs, gm, wa, wb, wc, wg)
        oa, c_stack, n1, m1 = _mlstm_sample(pa, pg, gbias, na, state_mlstm_C, state_mlstm_n, m0rows, dseq,
                                            l, c_stack)
        ob, h_stack, buf1 = _ssd_sample(pb, pg, gbias, alog, dskip, cw, cb, nb, state_ssd, hist, dseq, l, h_stack)
        oc, s_stack = _ret_sample(pc, cos_s, sin_s, nc, state_ret, dseq, l, s_stack)
        xs = _mix_ffn_call(xs, oa, ob, oc, wo, g2, w1b, w3b, w2b, fin, final)
        for acc, v in zip(outs_s, (n1, m1[:, GATE_F:GATE_F + H_A], buf1)):
            acc.append(v)

    y_prompt = xp.reshape(bsz, seq, D_MODEL)
    y_sample = xs.reshape(dbsz, dseq, D_MODEL)
    s_n, s_m, s_buf = [jnp.stack(a) for a in outs_s]
    return (y_prompt, y_sample, *[jnp.stack(a) for a in outs_p], c_stack, s_n, s_m, h_stack, s_buf, s_stack)
```

```python
import functools
import math

import jax
import jax.numpy as jnp
from jax import lax
from jax.experimental import pallas as pl
from jax.experimental.pallas import tpu as pltpu

F32 = jnp.float32
BF16 = jnp.bfloat16
HIGHEST = lax.Precision.HIGHEST

D_MODEL = 1024
D_FF = 2816
GROUP_W = 512
H_A = 4
DH = 128
H_B = 8
P_B = 64
N_B = 128
G_B = 2
CONV_W = 4
CONV_DIM = GROUP_W + 2 * G_B * N_B
H_C = 4
CHUNK = 128
PAST_LEN = 16384
GATE_SOFTCAP = 15.0
ROPE_BASE = 10000.0
EPS = 1e-6
QK_SCALE = DH ** -0.5

LANES = 128
SUBLANES = 8

ROWS = 128
SEQ_PER_STEP = 8
TM_FFN = 1024
SEQ_UNROLL = 8
TM = 512
FC = 256
VMEM_LIMIT = 56 * 1024 * 1024

GATE_I = 0
GATE_F = 4
GATE_DT = 8
GATE_ROWS = 16


def _dot(a, b):
    return jnp.dot(a, b, preferred_element_type=F32)


def _dot_nt(a, b):
    return lax.dot_general(a, b, (((1,), (1,)), ((), ())), preferred_element_type=F32)


def _dot_tn(a, b):
    return lax.dot_general(a, b, (((0,), (0,)), ((), ())), preferred_element_type=F32)


def _dot_exact(a, b):
    return jnp.dot(a, b, precision=HIGHEST, preferred_element_type=F32)


def _rms(x, g):
    return x * lax.rsqrt(jnp.mean(x * x, axis=-1, keepdims=True) + EPS) * g


def _silu(x):
    return x * jax.nn.sigmoid(x)


def _log1p_exp_neg_abs(x):
    return jnp.log1p(jnp.exp(-jnp.abs(x)))


def _causal_mask(rows, lc):
    r = lax.broadcasted_iota(jnp.int32, (rows, rows), 0)
    c = lax.broadcasted_iota(jnp.int32, (rows, rows), 1)
    m = c <= r
    if lc != rows:
        shift = lc.bit_length() - 1
        m = m & ((r >> shift) == (c >> shift))
    return m, r, c


def _group_last(x, lc):
    rows, w = x.shape
    x3 = x.reshape(rows // lc, lc, w)
    return jnp.broadcast_to(x3[:, lc - 1:lc, :], x3.shape).reshape(rows, w)


def _group_first(x, lc):
    rows, w = x.shape
    return x.reshape(rows // lc, lc, w)[:, 0, :]


def _group_sum(x, lc):
    rows, w = x.shape
    return jnp.sum(x.reshape(rows // lc, lc, w), axis=1)


def _group_bcast(x, lc, rows):
    n, w = x.shape
    return jnp.broadcast_to(x[:, None, :], (n, lc, w)).reshape(rows, w)


def _ffn_core(x, g_ref, w1_ref, w3_ref, w2_ref, act_ref):
    h = _rms(x, g_ref[...]).astype(BF16)
    for c in range(D_FF // FC):
        cols = slice(c * FC, (c + 1) * FC)
        a = _dot(h, w1_ref[:, cols])
        b = _dot(h, w3_ref[:, cols])
        act_ref[:, cols] = (_silu(a) * b).astype(BF16)
    return x + 0.5 * _dot(act_ref[...], w2_ref[...])


def _ffn_body(x_ref, g_ref, w1_ref, w3_ref, w2_ref, o_ref, act_ref):
    o_ref[...] = _ffn_core(x_ref[...], g_ref, w1_ref, w3_ref, w2_ref, act_ref)


def _mix_ffn_body(x_ref, oa_ref, ob_ref, oc_ref, wo_ref, g_ref, w1_ref, w3_ref, w2_ref, fin_ref,
                  o_ref, act_ref, *, final):
    x = x_ref[...]
    x = x + (_dot(oa_ref[...], wo_ref[0:GROUP_W, :])
             + _dot(ob_ref[...], wo_ref[GROUP_W:2 * GROUP_W, :])
             + _dot(oc_ref[...], wo_ref[2 * GROUP_W:3 * GROUP_W, :]))
    y = _ffn_core(x, g_ref, w1_ref, w3_ref, w2_ref, act_ref)
    if final:
        y = _rms(y, fin_ref[...])
    o_ref[...] = y


def _inproj_body(x_ref, g_ref, wa_ref, wb_ref, wc_ref, wg_ref, pa_ref, pb_ref, pc_ref, pg_ref):
    h = _rms(x_ref[...], g_ref[...]).astype(BF16)
    pa_ref[...] = _dot(h, wa_ref[...])
    pb_ref[...] = _dot(h, wb_ref[...])
    pc_ref[...] = _dot(h, wc_ref[...])
    pg_ref[...] = _dot(h, wg_ref[...])


def _inproj_prompt_body(x_ref, g_ref, wta_ref, wtb_ref, wtc_ref, wtg_ref, wk_ref,
                        pta_ref, ptb_ref, ptc_ref, gt_ref, pk_ref):
    h = _rms(x_ref[...], g_ref[...]).astype(BF16)
    for w_ref, o_ref in ((wta_ref, pta_ref), (wtb_ref, ptb_ref), (wtc_ref, ptc_ref), (wtg_ref, gt_ref)):
        pt = _dot_nt(w_ref[...], h)
        for j in range(TM // ROWS):
            o_ref[j] = pt[:, j * ROWS:(j + 1) * ROWS]
    pk_ref[...] = _dot(h, wk_ref[...])


def _const_spec(shape):
    nd = len(shape)
    return pl.BlockSpec(shape, lambda *_: (0,) * nd, pipeline_mode=pl.Buffered(1))


def _layer_weight_spec(shape, layer):
    nd = len(shape)
    return pl.BlockSpec((None,) + shape, lambda *_: (layer,) + (0,) * nd, pipeline_mode=pl.Buffered(1))


def _row_spec(rows, cols):
    return pl.BlockSpec((rows, cols), lambda i: (i, 0))


def _dense_params():
    return pltpu.CompilerParams(dimension_semantics=("arbitrary",), vmem_limit_bytes=VMEM_LIMIT)


def _ffn_call(x, g, w1, w3, w2, layer):
    t = x.shape[0]
    return pl.pallas_call(
        _ffn_body,
        grid=(t // TM_FFN,),
        in_specs=[_row_spec(TM_FFN, D_MODEL), _const_spec((1, D_MODEL)),
                  _layer_weight_spec((D_MODEL, D_FF), layer), _layer_weight_spec((D_MODEL, D_FF), layer),
                  _layer_weight_spec((D_FF, D_MODEL), layer)],
        out_specs=_row_spec(TM_FFN, D_MODEL),
        out_shape=jax.ShapeDtypeStruct((t, D_MODEL), F32),
        scratch_shapes=[pltpu.VMEM((TM_FFN, D_FF), BF16)],
        compiler_params=_dense_params(),
        name="ffn",
    )(x, g, w1, w3, w2)


def _mix_ffn_call(x, oa, ob, oc, wo, g, w1, w3, w2, fin, final, layer):
    t = x.shape[0]
    return pl.pallas_call(
        functools.partial(_mix_ffn_body, final=final),
        grid=(t // TM_FFN,),
        in_specs=[_row_spec(TM_FFN, D_MODEL)] + [_row_spec(TM_FFN, GROUP_W)] * 3 + [
                  _layer_weight_spec((3 * GROUP_W, D_MODEL), layer), _const_spec((1, D_MODEL)),
                  _layer_weight_spec((D_MODEL, D_FF), layer), _layer_weight_spec((D_MODEL, D_FF), layer),
                  _layer_weight_spec((D_FF, D_MODEL), layer), _const_spec((1, D_MODEL))],
        out_specs=_row_spec(TM_FFN, D_MODEL),
        out_shape=jax.ShapeDtypeStruct((t, D_MODEL), F32),
        scratch_shapes=[pltpu.VMEM((TM_FFN, D_FF), BF16)],
        compiler_params=_dense_params(),
        name="mix_ffn",
    )(x, oa, ob, oc, wo, g, w1, w3, w2, fin)


def _inproj_call(x, g, wa, wb, wc, wg):
    t = x.shape[0]
    widths = (wa.shape[1], wb.shape[1], wc.shape[1], wg.shape[1])
    return pl.pallas_call(
        _inproj_body,
        grid=(t // TM,),
        in_specs=[_row_spec(TM, D_MODEL), _const_spec((1, D_MODEL))] + [_const_spec((D_MODEL, w)) for w in widths],
        out_specs=[_row_spec(TM, w) for w in widths],
        out_shape=[jax.ShapeDtypeStruct((t, w), F32) for w in widths],
        compiler_params=_dense_params(),
        name="inproj",
    )(x, g, wa, wb, wc, wg)


def _inproj_prompt_call(x, g, wta, wtb, wtc, wtg, wk):
    t = x.shape[0]
    t_rows = (wta.shape[0], wtb.shape[0], wtc.shape[0], wtg.shape[0])
    n_cols = (wk.shape[1],)
    chunk_spec = lambda r: pl.BlockSpec((TM // ROWS, r, ROWS), lambda i: (i, 0, 0))
    return pl.pallas_call(
        _inproj_prompt_body,
        grid=(t // TM,),
        in_specs=([_row_spec(TM, D_MODEL), _const_spec((1, D_MODEL))]
                  + [_const_spec((r, D_MODEL)) for r in t_rows] + [_const_spec((D_MODEL, w)) for w in n_cols]),
        out_specs=[chunk_spec(r) for r in t_rows] + [_row_spec(TM, w) for w in n_cols],
        out_shape=([jax.ShapeDtypeStruct((t // ROWS, r, ROWS), F32) for r in t_rows]
                   + [jax.ShapeDtypeStruct((t, w), F32) for w in n_cols]),
        compiler_params=_dense_params(),
        name="inproj_prompt",
    )(x, g, wta, wtb, wtc, wtg, wk)


def _source_target_mask():
    src = lax.broadcasted_iota(jnp.int32, (ROWS, ROWS), 0)
    tgt = lax.broadcasted_iota(jnp.int32, (ROWS, ROWS), 1)
    return src <= tgt, src, tgt


def _last_lane(x):
    return jnp.broadcast_to(x[:, LANES - 1:LANES], x.shape)


def _mlstm_prompt_body(pt_ref, k_ref, gt_ref, gb_ref, normb_ref, out_ref, c_ref, n_ref, m_ref, ct_scr, *, nb):
    @pl.when(pl.program_id(1) == 0)
    def _init():
        ct_scr[...] = jnp.zeros_like(ct_scr)
        n_ref[...] = jnp.zeros_like(n_ref)
        m_ref[...] = jnp.zeros_like(m_ref)

    causal_t, _, _ = _source_target_mask()
    gates = [_mlstm_prompt_gates(gt_ref.at[sq], gb_ref, m_ref.at[sq], causal_t) for sq in range(nb)]
    units = [(sq, h) for sq in range(nb) for h in range(H_A)]

    st_raw, cq, qn, c_upd, n_upd, p_all = {}, {}, {}, {}, {}, {}
    for sq, h in units:
        c = GATE_F + h
        gs = gates[sq]
        qt = (pt_ref[sq, h * DH:(h + 1) * DH, :] * QK_SCALE).astype(BF16)
        vt = pt_ref[sq, GROUP_W + h * DH:GROUP_W + (h + 1) * DH, :]
        kf = k_ref[sq, :, h * DH:(h + 1) * DH]
        kb = kf.astype(BF16)
        n8 = jnp.concatenate([n_ref[sq, h:h + 1, :], jnp.zeros((SUBLANES - 1, DH), F32)], axis=0)
        st_raw[sq, h] = _dot(kb, qt)
        cq[sq, h] = _dot(ct_scr[sq, h].astype(BF16), qt)
        qn[sq, h] = _dot(n8.astype(BF16), qt)[0:1, :]
        c_upd[sq, h] = _dot((vt * gs["wend"][c:c + 1, :]).astype(BF16), kb)
        n_upd[sq, h] = _dot_exact(gs["wend"], kf)[c:c + 1, :]
        p_all[sq, h] = jnp.where(causal_t, jnp.exp(gs["ut"][:, c:c + 1] + gs["bm"][c:c + 1, :]), 0.0)

    num, den = {}, {}
    for sq, h in units:
        c = GATE_F + h
        gs = gates[sq]
        st = st_raw[sq, h] * p_all[sq, h]
        vt = pt_ref[sq, GROUP_W + h * DH:GROUP_W + (h + 1) * DH, :].astype(BF16)
        g = gs["gint"][c:c + 1, :]
        num[sq, h] = _dot(vt, st.astype(BF16)) + g * cq[sq, h]
        den[sq, h] = jnp.sum(st, axis=0, keepdims=True) + g * qn[sq, h]
        g1 = gs["gend"][c:c + 1, 0:1]
        ct_scr[sq, h] = g1 * ct_scr[sq, h] + c_upd[sq, h]
        n_ref[sq, h:h + 1, :] = g1 * n_ref[sq, h:h + 1, :] + n_upd[sq, h]

    for sq, h in units:
        c = GATE_F + h
        hs = slice(h * DH, (h + 1) * DH)
        ot = pt_ref[sq, 2 * GROUP_W + h * DH:2 * GROUP_W + (h + 1) * DH, :]
        hh = num[sq, h] * (1.0 / jnp.maximum(jnp.abs(den[sq, h]), gates[sq]["emt"][c:c + 1, :]))
        rs = lax.rsqrt(jnp.mean(hh * hh, axis=0, keepdims=True) + EPS)
        o = hh * rs * normb_ref[hs, :] * jax.nn.sigmoid(ot)
        out_ref[sq, :, hs] = o.T.astype(out_ref.dtype)

    @pl.when(pl.program_id(1) == pl.num_programs(1) - 1)
    def _finish():
        for sq in range(nb):
            for h in range(H_A):
                c_ref[sq, h] = ct_scr[sq, h].T


def _mlstm_prompt_gates(gt_ref, gb_ref, m_ref, causal_t):
    rowid = lax.broadcasted_iota(jnp.int32, (SUBLANES, LANES), 0)
    frows = rowid >= GATE_F

    pre = gt_ref[0:SUBLANES, :] + gb_ref[...]
    cap = GATE_SOFTCAP * jnp.tanh(pre / GATE_SOFTCAP)
    logf = jnp.minimum(cap, 0.0) - _log1p_exp_neg_abs(cap)
    bcum = _dot_exact(logf, jnp.where(causal_t, 1.0, 0.0))
    mprev = m_ref[...]
    inter = bcum + mprev
    u = pltpu.roll(cap, GATE_F - GATE_I, axis=0) - bcum
    ut = jnp.concatenate([u, jnp.zeros((ROWS - SUBLANES, LANES), F32)], axis=0).T
    mt_all = jnp.zeros((SUBLANES, LANES), F32)
    for h in range(H_A):
        c = GATE_F + h
        d = jnp.where(causal_t, ut[:, c:c + 1] + bcum[c:c + 1, :], -jnp.inf)
        mt = jnp.maximum(inter[c:c + 1, :], jnp.max(d, axis=0, keepdims=True))
        mt_all = jnp.where(rowid == c, mt, mt_all)
    blast = _last_lane(bcum)
    mend = _last_lane(mt_all)
    wend = jnp.where(frows, jnp.exp(blast + u - mend), 0.0)
    gend = jnp.where(frows, jnp.exp(blast + mprev - mend), 0.0)
    gint = jnp.where(frows, jnp.exp(inter - mt_all), 0.0)
    emt = jnp.exp(-mt_all)
    bm = bcum - mt_all
    m_ref[...] = jnp.where(frows, mend, 0.0)
    return dict(ut=ut, bm=bm, wend=wend, gend=gend, gint=gint, emt=emt)


def _mlstm_sample_body(*refs, lc, nseq):
    _mlstm_block(*refs[:7], *refs[8:], lc=lc, nseq=nseq)


def _mlstm_block(pa_ref, pg_ref, gb_ref, norm_ref, c0_ref, n0_ref, m0_ref,
                 out_ref, c_ref, n_ref, m_ref, qc_scr, kw_scr, g_scr, *, lc, nseq):
    rows = lc * nseq
    mprev = m0_ref[...]
    causal, _, _ = _causal_mask(rows, lc)
    lane = lax.broadcasted_iota(jnp.int32, (rows, LANES), 1)
    fcols = (lane >= GATE_F) & (lane < GATE_F + H_A)

    pre = pg_ref[...] + gb_ref[...]
    cap = GATE_SOFTCAP * jnp.tanh(pre / GATE_SOFTCAP)
    logf = jnp.minimum(cap, 0.0) - _log1p_exp_neg_abs(cap)
    bcum = _dot_exact(jnp.where(causal, 1.0, 0.0), logf)
    inter = bcum + mprev
    u = pltpu.roll(cap, GATE_F - GATE_I, axis=1) - bcum
    ut = u.T
    mt_all = jnp.zeros((rows, LANES), F32)
    for h in range(H_A):
        c = GATE_F + h
        d = jnp.where(causal, bcum[:, c:c + 1] + ut[c:c + 1, :], -jnp.inf)
        mt = jnp.maximum(inter[:, c:c + 1], jnp.max(d, axis=1, keepdims=True))
        mt_all = jnp.where(lane == c, mt, mt_all)
    blast = _group_last(bcum, lc)
    mend = _group_last(mt_all, lc)
    wend = jnp.where(fcols, jnp.exp(blast + u - mend), 0.0)
    gend = jnp.where(fcols, jnp.exp(blast + mprev - mend), 0.0)
    gint = jnp.where(fcols, jnp.exp(inter - mt_all), 0.0)
    emt = jnp.exp(-mt_all)
    bm = bcum - mt_all

    def head_cols(group, h):
        return slice(group * GROUP_W + h * DH, group * GROUP_W + (h + 1) * DH)

    g_scr[...] = gend
    for h in range(H_A):
        c = GATE_F + h
        kw_scr[h] = pa_ref[:, head_cols(1, h)] * wend[:, c:c + 1]

    def seq_body(j, carry):
        r0 = pl.multiple_of(j * lc, lc)
        grow = g_scr[pl.ds(r0, 1), :]
        for h in range(H_A):
            c = GATE_F + h
            qj = (pa_ref[pl.ds(r0, lc), head_cols(0, h)] * QK_SCALE).astype(BF16)
            vj = pa_ref[pl.ds(r0, lc), head_cols(2, h)].astype(BF16)
            kwj = kw_scr[h, pl.ds(r0, lc), :].astype(BF16)
            c_old = c0_ref[j, h]
            qc_scr[h, pl.ds(r0, lc), :] = _dot(qj, c_old.astype(BF16))
            c_ref[j, h] = grow[:, c:c + 1] * c_old + _dot_tn(kwj, vj)
        return carry

    lax.fori_loop(0, nseq, seq_body, 0, unroll=SEQ_UNROLL)
    n_rows = []
    gfirst = _group_first(gend, lc)
    for h in range(H_A):
        c = GATE_F + h
        n_old = n0_ref[:, h, :]
        n_rows.append(_group_bcast(n_old, lc, rows))
        n_ref[:, h, :] = gfirst[:, c:c + 1] * n_old + _group_sum(kw_scr[h], lc)
    m_ref[...] = _group_first(mend, lc)

    scores = []
    for h in range(H_A):
        qb = (pa_ref[:, head_cols(0, h)] * QK_SCALE).astype(BF16)
        scores.append(_dot_nt(qb, pa_ref[:, head_cols(1, h)].astype(BF16)))
    for h in range(H_A):
        c = GATE_F + h
        qf = pa_ref[:, head_cols(0, h)] * QK_SCALE
        v = pa_ref[:, head_cols(2, h)].astype(BF16)
        og = pa_ref[:, head_cols(3, h)]
        p = jnp.where(causal, jnp.exp(bm[:, c:c + 1] + ut[c:c + 1, :]), 0.0)
        s = scores[h] * p
        g = gint[:, c:c + 1]
        num = _dot(s.astype(BF16), v) + g * qc_scr[h]
        qn = jnp.sum(qf * n_rows[h], axis=1, keepdims=True)
        den = jnp.sum(s, axis=1, keepdims=True) + g * qn
        hh = num / jnp.maximum(jnp.abs(den), emt[:, c:c + 1])
        hn = hh * lax.rsqrt(jnp.mean(hh * hh, axis=1, keepdims=True) + EPS) * norm_ref[:, h * DH:(h + 1) * DH]
        out_ref[:, h * DH:(h + 1) * DH] = (hn * jax.nn.sigmoid(og)).astype(out_ref.dtype)


def _mixer_params(ndims):
    return pltpu.CompilerParams(dimension_semantics=("arbitrary",) * ndims, vmem_limit_bytes=VMEM_LIMIT)


def _seq_spec(cols):
    return pl.BlockSpec((SEQ_PER_STEP, ROWS, cols), lambda b, c: (b, c, 0))


def _seq_state_spec(shape):
    nd = len(shape)
    return pl.BlockSpec((SEQ_PER_STEP,) + shape, lambda b, c: (b,) + (0,) * nd)


def _seq_t_spec(rows):
    return pl.BlockSpec((SEQ_PER_STEP, None, rows, ROWS), lambda b, c: (b, c, 0, 0))


def _seq_col_spec(cols, col_block):
    return pl.BlockSpec((SEQ_PER_STEP, ROWS, cols), lambda b, c: (b, c, col_block))


def _mlstm_prompt(pt, pk, gt, gb8, normb, bsz, nchunk):
    seq = nchunk * ROWS
    const = lambda b, c: (0, 0)
    out, c1, n1, m1 = pl.pallas_call(
        functools.partial(_mlstm_prompt_body, nb=SEQ_PER_STEP),
        grid=(bsz // SEQ_PER_STEP, nchunk),
        in_specs=[_seq_t_spec(3 * GROUP_W), _seq_col_spec(GROUP_W, 0), _seq_t_spec(GATE_ROWS),
                  pl.BlockSpec((SUBLANES, LANES), const), pl.BlockSpec((GROUP_W, LANES), const)],
        out_specs=[_seq_spec(GROUP_W), _seq_state_spec((H_A, DH, DH)), _seq_state_spec((H_A, DH)),
                   _seq_state_spec((SUBLANES, LANES))],
        out_shape=[jax.ShapeDtypeStruct((bsz, seq, GROUP_W), BF16),
                   jax.ShapeDtypeStruct((bsz, H_A, DH, DH), F32),
                   jax.ShapeDtypeStruct((bsz, H_A, DH), F32),
                   jax.ShapeDtypeStruct((bsz, SUBLANES, LANES), F32)],
        scratch_shapes=[pltpu.VMEM((SEQ_PER_STEP, H_A, DH, DH), F32)],
        compiler_params=_mixer_params(2),
        name="mlstm_prompt",
    )(pt.reshape(bsz, nchunk, 3 * GROUP_W, ROWS), pk.reshape(bsz, seq, -1),
      gt.reshape(bsz, nchunk, GATE_ROWS, ROWS), gb8, normb)
    return out.reshape(bsz * seq, GROUP_W), c1, n1, m1


def _layer_state_spec(shape, layer):
    nd = len(shape)
    return pl.BlockSpec((None,) + shape, lambda i: (layer, i) + (0,) * (nd - 1))


_STACK_SPEC = pl.BlockSpec(memory_space=pl.ANY)


def _mlstm_sample(pa, pg, gbias, norm, c_all, n_all, m0rows, lc, layer, c_prev):
    t = pa.shape[0]
    nseq = ROWS // lc
    bsz = c_all.shape[1]
    row = lambda i: (i, 0)
    const = lambda i: (0, 0)
    return pl.pallas_call(
        functools.partial(_mlstm_sample_body, lc=lc, nseq=nseq),
        grid=(t // ROWS,),
        in_specs=[pl.BlockSpec((ROWS, 4 * GROUP_W), row), pl.BlockSpec((ROWS, LANES), row),
                  pl.BlockSpec((1, LANES), const), pl.BlockSpec((1, GROUP_W), const),
                  _layer_state_spec((nseq, H_A, DH, DH), layer),
                  _layer_state_spec((nseq, H_A, DH), layer),
                  pl.BlockSpec((ROWS, LANES), row), _STACK_SPEC],
        out_specs=[pl.BlockSpec((ROWS, GROUP_W), row),
                   _layer_state_spec((nseq, H_A, DH, DH), layer),
                   pl.BlockSpec((nseq, H_A, DH), lambda i: (i, 0, 0)),
                   pl.BlockSpec((nseq, LANES), row)],
        out_shape=[jax.ShapeDtypeStruct((t, GROUP_W), BF16),
                   jax.ShapeDtypeStruct(c_all.shape, F32),
                   jax.ShapeDtypeStruct((bsz, H_A, DH), F32),
                   jax.ShapeDtypeStruct((bsz, LANES), F32)],
        input_output_aliases={7: 1},
        scratch_shapes=[pltpu.VMEM((H_A, ROWS, DH), F32), pltpu.VMEM((H_A, ROWS, DH), F32),
                        pltpu.VMEM((ROWS, LANES), F32)],
        compiler_params=_mixer_params(1),
        name="mlstm_sample",
    )(pa, pg, gbias, norm, c_all, n_all, m0rows, c_prev)


LOG_GAMMA = [math.log(1.0 - 2.0 ** (-5.0 - h)) for h in range(H_C)]


def _ret_prompt_body(pt_ref, k_ref, cost_ref, sint_ref, cos_ref, sin_ref, dec_ref, normb_ref, out_ref, s_ref,
                     st_scr, *, nb):
    @pl.when(pl.program_id(1) == 0)
    def _init():
        st_scr[...] = jnp.zeros_like(st_scr)

    tin = lax.broadcasted_iota(jnp.int32, (1, ROWS), 1).astype(F32)
    cost, sint = cost_ref[...], sint_ref[...]
    cos, sin = cos_ref[...], sin_ref[...]
    units = [(sq, h) for sq in range(nb) for h in range(H_C)]

    st_raw, qs, s_upd = {}, {}, {}
    for sq, h in units:
        lg = LOG_GAMMA[h]
        qt = pt_ref[sq, h * DH:(h + 1) * DH, :]
        vt = pt_ref[sq, GROUP_W + h * DH:GROUP_W + (h + 1) * DH, :]
        kf = k_ref[sq, :, h * DH:(h + 1) * DH]
        qr = (qt * cost + pltpu.roll(qt, DH // 2, axis=0) * sint).astype(BF16)
        kr = ((kf * cos + pltpu.roll(kf, DH // 2, axis=1) * sin) * QK_SCALE).astype(BF16)
        ve = (vt * jnp.exp((ROWS - 1.0 - tin) * lg)).astype(BF16)
        st_raw[sq, h] = _dot(kr, qr)
        qs[sq, h] = _dot(st_scr[sq, h].astype(BF16), qr)
        s_upd[sq, h] = _dot(ve, kr)

    o_all = {}
    for sq, h in units:
        lg = LOG_GAMMA[h]
        vt = pt_ref[sq, GROUP_W + h * DH:GROUP_W + (h + 1) * DH, :].astype(BF16)
        st = (st_raw[sq, h] * dec_ref[h]).astype(BF16)
        o_all[sq, h] = _dot(vt, st) + jnp.exp((tin + 1.0) * lg) * qs[sq, h]
        st_scr[sq, h] = math.exp(ROWS * lg) * st_scr[sq, h] + s_upd[sq, h]

    for sq, h in units:
        hs = slice(h * DH, (h + 1) * DH)
        gt = pt_ref[sq, 2 * GROUP_W + h * DH:2 * GROUP_W + (h + 1) * DH, :]
        o = o_all[sq, h]
        rs = lax.rsqrt(jnp.mean(o * o, axis=0, keepdims=True) + EPS)
        on = o * rs * normb_ref[hs, :] * _silu(gt)
        out_ref[sq, :, hs] = on.T.astype(out_ref.dtype)

    @pl.when(pl.program_id(1) == pl.num_programs(1) - 1)
    def _finish():
        for sq in range(nb):
            for h in range(H_C):
                s_ref[sq, h] = st_scr[sq, h].T


def _ret_sample_body(*refs, lc, nseq):
    _ret_block(*refs[:5], *refs[6:], lc=lc, nseq=nseq)


def _ret_block(pc_ref, cos_ref, sin_ref, norm_ref, s0_ref, out_ref, s_ref, qs_scr, qr_scr, ke_scr, *, lc, nseq):
    rows = lc * nseq
    causal, r, c = _causal_mask(rows, lc)
    diff = (r - c).astype(F32)
    tin = (lax.broadcasted_iota(jnp.int32, (rows, 1), 0) & (lc - 1)).astype(F32)
    cos = cos_ref[...]
    sin = sin_ref[...]

    def head_cols(group, h):
        return slice(group * GROUP_W + h * DH, group * GROUP_W + (h + 1) * DH)

    def rot(x):
        return x * cos + pltpu.roll(x, DH // 2, axis=1) * sin

    for h in range(H_C):
        qr_scr[h] = rot(pc_ref[:, head_cols(0, h)])
        ke_scr[h] = rot(pc_ref[:, head_cols(1, h)]) * QK_SCALE * jnp.exp((lc - 1.0 - tin) * LOG_GAMMA[h])

    def seq_body(j, carry):
        r0 = pl.multiple_of(j * lc, lc)
        for h in range(H_C):
            qj = qr_scr[h, pl.ds(r0, lc), :].astype(BF16)
            kj = ke_scr[h, pl.ds(r0, lc), :].astype(BF16)
            vj = pc_ref[pl.ds(r0, lc), head_cols(2, h)].astype(BF16)
            s_old = s0_ref[j, h]
            qs_scr[h, pl.ds(r0, lc), :] = _dot(qj, s_old.astype(BF16))
            s_ref[j, h] = math.exp(lc * LOG_GAMMA[h]) * s_old + _dot_tn(kj, vj)
        return carry

    lax.fori_loop(0, nseq, seq_body, 0, unroll=SEQ_UNROLL)

    scores = []
    for h in range(H_C):
        kr = (rot(pc_ref[:, head_cols(1, h)]) * QK_SCALE).astype(BF16)
        scores.append(_dot_nt(qr_scr[h].astype(BF16), kr))
    for h in range(H_C):
        lg = LOG_GAMMA[h]
        decay = jnp.where(causal, jnp.exp(diff * lg), 0.0)
        v = pc_ref[:, head_cols(2, h)].astype(BF16)
        gate = pc_ref[:, head_cols(3, h)]
        o = _dot((scores[h] * decay).astype(BF16), v) + jnp.exp((tin + 1.0) * lg) * qs_scr[h]
        on = o * lax.rsqrt(jnp.mean(o * o, axis=1, keepdims=True) + EPS) * norm_ref[:, h * DH:(h + 1) * DH]
        out_ref[:, h * DH:(h + 1) * DH] = (on * _silu(gate)).astype(out_ref.dtype)


def _ret_prompt(pt, pk, cos, sin, normb, bsz, nchunk):
    seq = nchunk * ROWS
    chunk = lambda b, c: (c, 0)
    chunk_t = lambda b, c: (0, c)
    const = lambda b, c: (0, 0)
    idx = jnp.arange(ROWS, dtype=F32)
    diff = idx[None, :] - idx[:, None]
    log_gamma = jnp.asarray(LOG_GAMMA, F32)[:, None, None]
    dec = jnp.where(diff >= 0, jnp.exp(diff * log_gamma), 0.0)
    out, s1 = pl.pallas_call(
        functools.partial(_ret_prompt_body, nb=SEQ_PER_STEP),
        grid=(bsz // SEQ_PER_STEP, nchunk),
        in_specs=[_seq_t_spec(3 * GROUP_W), _seq_col_spec(GROUP_W, 1),
                  pl.BlockSpec((DH, ROWS), chunk_t), pl.BlockSpec((DH, ROWS), chunk_t),
                  pl.BlockSpec((ROWS, DH), chunk), pl.BlockSpec((ROWS, DH), chunk),
                  pl.BlockSpec((H_C, ROWS, ROWS), lambda b, c: (0, 0, 0)),
                  pl.BlockSpec((GROUP_W, LANES), const)],
        out_specs=[_seq_spec(GROUP_W), _seq_state_spec((H_C, DH, DH))],
        out_shape=[jax.ShapeDtypeStruct((bsz, seq, GROUP_W), BF16),
                   jax.ShapeDtypeStruct((bsz, H_C, DH, DH), F32)],
        scratch_shapes=[pltpu.VMEM((SEQ_PER_STEP, H_C, DH, DH), F32)],
        compiler_params=_mixer_params(2),
        name="ret_prompt",
    )(pt.reshape(bsz, nchunk, 3 * GROUP_W, ROWS), pk.reshape(bsz, seq, -1), cos.T, sin.T, cos, sin, dec, normb)
    return out.reshape(bsz * seq, GROUP_W), s1


def _ret_sample(pc, cos, sin, norm, s_all, lc, layer, s_prev):
    t = pc.shape[0]
    nseq = ROWS // lc
    row = lambda i: (i, 0)
    const = lambda i: (0, 0)
    return pl.pallas_call(
        functools.partial(_ret_sample_body, lc=lc, nseq=nseq),
        grid=(t // ROWS,),
        in_specs=[pl.BlockSpec((ROWS, 4 * GROUP_W), row), pl.BlockSpec((ROWS, DH), const),
                  pl.BlockSpec((ROWS, DH), const), pl.BlockSpec((1, GROUP_W), const),
                  _layer_state_spec((nseq, H_C, DH, DH), layer), _STACK_SPEC],
        out_specs=[pl.BlockSpec((ROWS, GROUP_W), row),
                   _layer_state_spec((nseq, H_C, DH, DH), layer)],
        out_shape=[jax.ShapeDtypeStruct((t, GROUP_W), BF16),
                   jax.ShapeDtypeStruct(s_all.shape, F32)],
        input_output_aliases={5: 1},
        scratch_shapes=[pltpu.VMEM((H_C, ROWS, DH), F32)] * 3,
        compiler_params=_mixer_params(1),
        name="ret_sample",
    )(pc, cos, sin, norm, s_all, s_prev)


HEADS_PER_GROUP = H_B // G_B
PAIR_W = 2 * P_B
GROUP_CH = HEADS_PER_GROUP * P_B


def _ssd_prompt_body(pt_ref, gt_ref, gb_ref, alog_ref, dskipb_ref, cwb_ref, cbb_ref, normb_ref,
                     out_ref, h_ref, tail_ref, xc_scr, *, nb):
    @pl.when(pl.program_id(1) == 0)
    def _init():
        h_ref[...] = jnp.zeros_like(h_ref)
        tail_ref[...] = jnp.zeros_like(tail_ref)

    causal_t, _, _ = _source_target_mask()
    lane = lax.broadcasted_iota(jnp.int32, (DH, ROWS), 1)
    rowid = lax.broadcasted_iota(jnp.int32, (GATE_ROWS, LANES), 0)
    drows = rowid >= GATE_DT
    pad = jnp.zeros((ROWS - GATE_ROWS, LANES), F32)

    gates = []
    for sq in range(nb):
        for blk in range(CONV_DIM // DH):
            ch = slice(blk * DH, (blk + 1) * DH)
            new = pt_ref[sq, GROUP_W + blk * DH:GROUP_W + (blk + 1) * DH, :]
            prev = tail_ref[sq, ch, :]
            acc = new * cwb_ref[CONV_W - 1, ch, :]
            for k in range(1, CONV_W):
                shifted = pltpu.roll(jnp.where(lane >= ROWS - k, prev, new), k, axis=1)
                acc = acc + shifted * cwb_ref[CONV_W - 1 - k, ch, :]
            xc_scr[sq, ch, :] = _silu(acc + cbb_ref[ch, :])
            tail_ref[sq, ch, :] = new

        dpre = gt_ref[sq] + gb_ref[...]
        dt = jnp.maximum(dpre, 0.0) + _log1p_exp_neg_abs(dpre)
        adt = dt * (-jnp.exp(alog_ref[...]))
        acum = jnp.where(drows, _dot_exact(adt, jnp.where(causal_t, 1.0, 0.0)), 0.0)
        alast = _last_lane(acum)
        gates.append(dict(dt=dt, acum=acum, at=jnp.concatenate([acum, pad], axis=0).T, exp_a=jnp.exp(acum),
                          wx=jnp.exp(alast - acum), g_a=jnp.exp(alast)))

    def x_rows(hd):
        return slice(hd * P_B, (hd + 1) * P_B)

    def b_rows(g):
        return slice(GROUP_W + g * N_B, GROUP_W + (g + 1) * N_B)

    def c_rows(g):
        return slice(GROUP_W + G_B * N_B + g * N_B, GROUP_W + G_B * N_B + (g + 1) * N_B)

    groups = [(sq, g) for sq in range(nb) for g in range(G_B)]
    cbs, chs, upds, xdts = {}, {}, {}, {}
    for sq, g in groups:
        gs = gates[sq]
        bg = xc_scr[sq, b_rows(g), :].T.astype(BF16)
        ct = xc_scr[sq, c_rows(g), :].astype(BF16)
        xws = []
        for r in range(HEADS_PER_GROUP):
            hd = g * HEADS_PER_GROUP + r
            c = GATE_DT + hd
            xdt = xc_scr[sq, x_rows(hd), :] * gs["dt"][c:c + 1, :]
            xdts[sq, hd] = xdt.astype(BF16)
            xws.append((xdt * gs["wx"][c:c + 1, :]).astype(BF16))
        h_old = jnp.concatenate([h_ref[sq, g * HEADS_PER_GROUP + r] for r in range(HEADS_PER_GROUP)], axis=0)
        cbs[sq, g] = _dot(bg, ct)
        chs[sq, g] = _dot(h_old.astype(BF16), ct)
        upds[sq, g] = _dot(jnp.concatenate(xws, axis=0), bg)

    ys = {}
    for sq, g in groups:
        gs = gates[sq]
        for r in range(HEADS_PER_GROUP):
            hd = g * HEADS_PER_GROUP + r
            c = GATE_DT + hd
            dec = jnp.where(causal_t, jnp.exp(gs["acum"][c:c + 1, :] - gs["at"][:, c:c + 1]), 0.0)
            ys[sq, hd] = _dot(xdts[sq, hd], (cbs[sq, g] * dec).astype(BF16))
            h_ref[sq, hd] = gs["g_a"][c:c + 1, 0:1] * h_ref[sq, hd] + upds[sq, g][r * P_B:(r + 1) * P_B, :]

    for sq, g in groups:
        gs = gates[sq]
        parts = []
        for r in range(HEADS_PER_GROUP):
            hd = g * HEADS_PER_GROUP + r
            c = GATE_DT + hd
            y = ys[sq, hd] + gs["exp_a"][c:c + 1, :] * chs[sq, g][r * P_B:(r + 1) * P_B, :]
            parts.append(y + dskipb_ref[x_rows(hd), :] * xc_scr[sq, x_rows(hd), :])
        grows = slice(g * GROUP_CH, (g + 1) * GROUP_CH)
        yz = jnp.concatenate(parts, axis=0) * _silu(pt_ref[sq, grows, :])
        yn = yz * lax.rsqrt(jnp.mean(yz * yz, axis=0, keepdims=True) + EPS) * normb_ref[grows, :]
        for j in range(GROUP_CH // DH):
            cols = slice(g * GROUP_CH + j * DH, g * GROUP_CH + (j + 1) * DH)
            out_ref[sq, :, cols] = yn[j * DH:(j + 1) * DH, :].T.astype(out_ref.dtype)


def _ssd_sample_body(*refs, lc, nseq):
    _ssd_block(*refs[:10], *refs[11:], lc=lc, nseq=nseq)


def _ssd_block(pb_ref, pg_ref, gb_ref, alog_ref, dskip_ref, cw_ref, cb_ref, norm_ref, h0_ref, hist_ref,
               out_ref, h_ref, buf_ref, ch_scr, xc_scr, xw_scr, g_scr, *, lc, nseq):
    rows = lc * nseq
    causal, _, _ = _causal_mask(rows, lc)
    lane = lax.broadcasted_iota(jnp.int32, (rows, LANES), 1)
    dcols = (lane >= GATE_DT) & (lane < GATE_DT + H_B)
    low = lax.broadcasted_iota(jnp.int32, (rows, PAIR_W), 1) < P_B

    new = pb_ref[:, GROUP_W:GROUP_W + CONV_DIM]
    tin = lax.broadcasted_iota(jnp.int32, (rows, 1), 0) & (lc - 1)
    acc = new * cw_ref[CONV_W - 1:CONV_W, :]
    for k in range(1, CONV_W):
        rolled = pltpu.roll(new, k, axis=0)
        hist = pltpu.roll(hist_ref[...], (rows + k - lc) % rows, axis=0)
        acc = acc + jnp.where(tin >= k, rolled, hist) * cw_ref[CONV_W - 1 - k:CONV_W - k, :]
    xc = _silu(acc + cb_ref[...])
    buf_ref[...] = new.reshape(nseq, lc, CONV_DIM)[:, lc - (CONV_W - 1):lc, :]

    dpre = pg_ref[...] + gb_ref[...]
    dt = jnp.maximum(dpre, 0.0) + _log1p_exp_neg_abs(dpre)
    adt = dt * (-jnp.exp(alog_ref[...]))
    acum = jnp.where(dcols, _dot_exact(jnp.where(causal, 1.0, 0.0), adt), 0.0)
    at = acum.T
    alast = _group_last(acum, lc)
    exp_a = jnp.exp(acum)
    wx = jnp.exp(alast - acum)
    g_a = jnp.exp(alast)

    def pair_bcast(slab, c0):
        return jnp.where(low, slab[:, c0:c0 + 1], slab[:, c0 + 1:c0 + 2])

    def pair_cols(g, p):
        start = g * GROUP_CH + p * PAIR_W
        return slice(start, start + PAIR_W)

    def b_cols(g):
        return slice(GROUP_W + g * N_B, GROUP_W + (g + 1) * N_B)

    def c_cols(g):
        return slice(GROUP_W + G_B * N_B + g * N_B, GROUP_W + G_B * N_B + (g + 1) * N_B)

    xw_pairs = {}
    xdt_pairs = {}
    for g in range(G_B):
        for p in range(HEADS_PER_GROUP // 2):
            c0 = GATE_DT + g * HEADS_PER_GROUP + 2 * p
            xdt = xc[:, pair_cols(g, p)] * pair_bcast(dt, c0)
            xdt_pairs[g, p] = xdt.astype(BF16)
            xw_pairs[g, p] = xdt * pair_bcast(wx, c0)
    xc_scr[...] = xc
    g_scr[...] = g_a
    for g in range(G_B):
        for p in range(HEADS_PER_GROUP // 2):
            xw_scr[:, pair_cols(g, p)] = xw_pairs[g, p]

    def seq_body(j, carry):
        r0 = pl.multiple_of(j * lc, lc)
        grow = g_scr[pl.ds(r0, 1), :]
        for g in range(G_B):
            bj = xc_scr[pl.ds(r0, lc), b_cols(g)].astype(BF16)
            cj = xc_scr[pl.ds(r0, lc), c_cols(g)].astype(BF16)
            xwj = xw_scr[pl.ds(r0, lc), g * GROUP_CH:(g + 1) * GROUP_CH].astype(BF16)
            hs = [h0_ref[j, g * HEADS_PER_GROUP + r] for r in range(HEADS_PER_GROUP)]
            h_old = jnp.concatenate(hs, axis=0)
            ch_scr[g, pl.ds(r0, lc), :] = _dot_nt(cj, h_old.astype(BF16))
            upd = _dot_tn(xwj, bj)
            for r in range(HEADS_PER_GROUP):
                hd = g * HEADS_PER_GROUP + r
                c = GATE_DT + hd
                h_ref[j, hd] = grow[:, c:c + 1] * hs[r] + upd[r * P_B:(r + 1) * P_B, :]
        return carry

    lax.fori_loop(0, nseq, seq_body, 0, unroll=SEQ_UNROLL)

    cbs = [_dot_nt(xc[:, c_cols(g)].astype(BF16), xc[:, b_cols(g)].astype(BF16)) for g in range(G_B)]
    for g in range(G_B):
        cb = cbs[g]
        ys = []
        for p in range(HEADS_PER_GROUP // 2):
            c0 = GATE_DT + g * HEADS_PER_GROUP + 2 * p
            xpair = xc[:, pair_cols(g, p)]
            xdt = xdt_pairs[g, p]
            halves = []
            for c in (c0, c0 + 1):
                dec = jnp.where(causal, jnp.exp(acum[:, c:c + 1] - at[c:c + 1, :]), 0.0)
                halves.append(_dot((cb * dec).astype(BF16), xdt))
            y = (jnp.where(low, halves[0], halves[1])
                 + pair_bcast(exp_a, c0) * ch_scr[g, :, p * PAIR_W:(p + 1) * PAIR_W])
            ys.append(y + dskip_ref[:, pair_cols(g, p)] * xpair)
        yg = jnp.concatenate(ys, axis=1)
        gcols = slice(g * GROUP_CH, (g + 1) * GROUP_CH)
        yz = yg * _silu(pb_ref[:, gcols])
        yn = yz * lax.rsqrt(jnp.mean(yz * yz, axis=1, keepdims=True) + EPS) * norm_ref[:, gcols]
        out_ref[:, gcols] = yn.astype(out_ref.dtype)


def _ssd_prompt(pt, gt, gb16, alog16, dskipb, cwb, cbb, normb, bsz, nchunk):
    seq = nchunk * ROWS
    const = lambda b, c: (0, 0)
    out, h1, tail = pl.pallas_call(
        functools.partial(_ssd_prompt_body, nb=SEQ_PER_STEP),
        grid=(bsz // SEQ_PER_STEP, nchunk),
        in_specs=[_seq_t_spec(GROUP_W + CONV_DIM), _seq_t_spec(GATE_ROWS),
                  pl.BlockSpec((GATE_ROWS, LANES), const), pl.BlockSpec((GATE_ROWS, LANES), const),
                  pl.BlockSpec((GROUP_W, LANES), const),
                  pl.BlockSpec((CONV_W, CONV_DIM, LANES), lambda b, c: (0, 0, 0)),
                  pl.BlockSpec((CONV_DIM, LANES), const), pl.BlockSpec((GROUP_W, LANES), const)],
        out_specs=[_seq_spec(GROUP_W), _seq_state_spec((H_B, P_B, N_B)), _seq_state_spec((CONV_DIM, ROWS))],
        out_shape=[jax.ShapeDtypeStruct((bsz, seq, GROUP_W), BF16),
                   jax.ShapeDtypeStruct((bsz, H_B, P_B, N_B), F32),
                   jax.ShapeDtypeStruct((bsz, CONV_DIM, ROWS), F32)],
        scratch_shapes=[pltpu.VMEM((SEQ_PER_STEP, CONV_DIM, ROWS), F32)],
        compiler_params=_mixer_params(2),
        name="ssd_prompt",
    )(pt.reshape(bsz, nchunk, GROUP_W + CONV_DIM, ROWS), gt.reshape(bsz, nchunk, GATE_ROWS, ROWS),
      gb16, alog16, dskipb, cwb, cbb, normb)
    return out.reshape(bsz * seq, GROUP_W), h1, tail


def _ssd_sample(pb, pg, gbias, alog, dskip, cw, cb, norm, h_all, hist, lc, layer, h_prev):
    t = pb.shape[0]
    nseq = ROWS // lc
    bsz = h_all.shape[1]
    row = lambda i: (i, 0)
    const = lambda i: (0, 0)
    return pl.pallas_call(
        functools.partial(_ssd_sample_body, lc=lc, nseq=nseq),
        grid=(t // ROWS,),
        in_specs=[pl.BlockSpec((ROWS, GROUP_W + CONV_DIM), row), pl.BlockSpec((ROWS, LANES), row),
                  pl.BlockSpec((1, LANES), const), pl.BlockSpec((1, LANES), const),
                  pl.BlockSpec((1, GROUP_W), const), pl.BlockSpec((CONV_W, CONV_DIM), const),
                  pl.BlockSpec((1, CONV_DIM), const), pl.BlockSpec((1, GROUP_W), const),
                  _layer_state_spec((nseq, H_B, P_B, N_B), layer),
                  pl.BlockSpec((ROWS, CONV_DIM), row), _STACK_SPEC],
        out_specs=[pl.BlockSpec((ROWS, GROUP_W), row),
                   _layer_state_spec((nseq, H_B, P_B, N_B), layer),
                   pl.BlockSpec((nseq, CONV_W - 1, CONV_DIM), lambda i: (i, 0, 0))],
        out_shape=[jax.ShapeDtypeStruct((t, GROUP_W), BF16),
                   jax.ShapeDtypeStruct(h_all.shape, F32),
                   jax.ShapeDtypeStruct((bsz, CONV_W - 1, CONV_DIM), F32)],
        input_output_aliases={10: 1},
        scratch_shapes=[pltpu.VMEM((G_B, ROWS, GROUP_CH), F32), pltpu.VMEM((ROWS, CONV_DIM), F32),
                        pltpu.VMEM((ROWS, GROUP_W), F32), pltpu.VMEM((ROWS, LANES), F32)],
        compiler_params=_mixer_params(1),
        name="ssd_sample",
    )(pb, pg, gbias, alog, dskip, cw, cb, norm, h_all, hist, h_prev)


def _rope_tables(pos):
    half = DH // 2
    freqs = ROPE_BASE ** (-jnp.arange(half, dtype=F32) / half)
    ang = pos.astype(F32)[:, None] * freqs
    cos, sin = jnp.cos(ang), jnp.sin(ang)
    return jnp.concatenate([cos, cos], axis=-1), jnp.concatenate([-sin, sin], axis=-1)


def _pad_lanes(parts):
    row = jnp.concatenate([p.astype(F32) for p in parts])
    return jnp.pad(row, (0, LANES - row.shape[0]))[None, :]


def kernel(x_prompt, x_sample, state_mlstm_C, state_mlstm_n, state_mlstm_m, state_ssd, state_conv, state_ret,
           ffn1_norm, ffn1_w1, ffn1_w3, ffn1_w2, mix_norm, w_in, b_igate, b_fgate, mlstm_norm,
           conv_w, conv_b, dt_bias, a_log, d_skip, ssd_norm, ret_norm, w_out,
           ffn2_norm, ffn2_w1, ffn2_w3, ffn2_w2, final_norm):
    depth = w_in.shape[0]
    bsz, seq, _ = x_prompt.shape
    dbsz, dseq, _ = x_sample.shape
    assert seq % CHUNK == 0 and ROWS % dseq == 0 and (dbsz * dseq) % ROWS == 0
    assert dseq >= SUBLANES and dseq & (dseq - 1) == 0
    nchunk = seq // CHUNK
    seq_per_block = ROWS // dseq

    xp = x_prompt.reshape(bsz * seq, D_MODEL)
    xs = x_sample.reshape(dbsz * dseq, D_MODEL)

    cos_p, sin_p = _rope_tables(jnp.arange(seq))
    cos_s, sin_s = _rope_tables(PAST_LEN + jnp.arange(dseq))
    cos_s, sin_s = jnp.tile(cos_s, (seq_per_block, 1)), jnp.tile(sin_s, (seq_per_block, 1))

    a0 = 0
    a_gate = a0 + 4 * GROUP_W
    b0 = a_gate + 2 * H_A
    b_dt = b0 + GROUP_W + CONV_DIM
    c0 = b_dt + H_B

    outs_p = [[] for _ in range(6)]
    outs_s = [[] for _ in range(3)]
    c_stack = jnp.zeros(state_mlstm_C.shape, F32)
    h_stack = jnp.zeros(state_ssd.shape, F32)
    s_stack = jnp.zeros(state_ret.shape, F32)
    w1a, w3a, w2a = ffn1_w1.astype(BF16), ffn1_w3.astype(BF16), ffn1_w2.astype(BF16)
    w1b, w3b, w2b = ffn2_w1.astype(BF16), ffn2_w3.astype(BF16), ffn2_w2.astype(BF16)
    wo = w_out.astype(BF16)
    for l in range(depth):
        wi = w_in[l]
        wa = wi[:, a0:a_gate].astype(BF16)
        wb = wi[:, b0:b_dt].astype(BF16)
        wc = wi[:, c0:c0 + 4 * GROUP_W].astype(BF16)
        wg = jnp.concatenate([wi[:, a_gate:b0], wi[:, b_dt:c0]], axis=1)
        wg = jnp.pad(wg, ((0, 0), (0, LANES - wg.shape[1]))).astype(BF16)
        g1, gm, g2 = ffn1_norm[l][None, :], mix_norm[l][None, :], ffn2_norm[l][None, :]
        gbias = _pad_lanes([b_igate[l], b_fgate[l], dt_bias[l]])
        alog = _pad_lanes([jnp.zeros((GATE_DT,), F32), a_log[l]])
        dskip = jnp.repeat(d_skip[l].astype(F32), P_B)[None, :]
        na, nb, nc = mlstm_norm[l][None, :], ssd_norm[l][None, :], ret_norm[l][None, :]
        cw, cb = conv_w[l], conv_b[l][None, :]
        final = l == depth - 1
        fin = final_norm[None, :]

        lanes_bcast = lambda v: jnp.broadcast_to(v.astype(F32)[..., None], v.shape + (LANES,))
        qa, ka, va, ga = (wi[:, a0 + i * GROUP_W:a0 + (i + 1) * GROUP_W] for i in range(4))
        qc, kc, vc, gc = (wi[:, c0 + i * GROUP_W:c0 + (i + 1) * GROUP_W] for i in range(4))
        wta = jnp.concatenate([qa, va, ga], axis=1).T.astype(BF16)
        wtb = wb.T
        wtc = jnp.concatenate([qc, vc, gc], axis=1).T.astype(BF16)
        wtg = jnp.concatenate([wi[:, a_gate:b0], wi[:, b_dt:c0]], axis=1).T.astype(BF16)
        wk = jnp.concatenate([ka, kc], axis=1).astype(BF16)
        gb8 = lanes_bcast(jnp.concatenate([b_igate[l], b_fgate[l]]))
        gb16 = lanes_bcast(jnp.concatenate([jnp.zeros((GATE_DT,), F32), dt_bias[l].astype(F32)]))
        alog16 = lanes_bcast(jnp.concatenate([jnp.zeros((GATE_DT,), F32), a_log[l].astype(F32)]))
        nab, nbb, ncb = lanes_bcast(mlstm_norm[l]), lanes_bcast(ssd_norm[l]), lanes_bcast(ret_norm[l])
        xp = _ffn_call(xp, g1, w1a, w3a, w2a, l)
        pta, ptb, ptc, gt, pk = _inproj_prompt_call(xp, gm, wta, wtb, wtc, wtg, wk)
        oa, c1, n1, m1 = _mlstm_prompt(pta, pk, gt, gb8, nab, bsz, nchunk)
        ob, h1, tail = _ssd_prompt(ptb, gt, gb16, alog16, lanes_bcast(dskip[0]), lanes_bcast(cw), lanes_bcast(cb[0]),
                                   nbb, bsz, nchunk)
        oc, s1 = _ret_prompt(ptc, pk, cos_p, sin_p, ncb, bsz, nchunk)
        xp = _mix_ffn_call(xp, oa, ob, oc, wo, g2, w1b, w3b, w2b, fin, final, l)
        buf1 = jnp.swapaxes(tail[:, :, ROWS - (CONV_W - 1):], 1, 2)
        for acc, v in zip(outs_p, (c1, n1, m1[:, GATE_F:GATE_F + H_A, 0], h1, buf1, s1)):
            acc.append(v)

        m0rows = jnp.pad(jnp.repeat(state_mlstm_m[l].astype(F32), dseq, axis=0),
                         ((0, 0), (GATE_F, LANES - GATE_F - H_A)))
        hist = jnp.pad(state_conv[l].astype(F32), ((0, 0), (dseq - (CONV_W - 1), 0), (0, 0)))
        hist = hist.reshape(dbsz * dseq, CONV_DIM)
        xs = _ffn_call(xs, g1, w1a, w3a, w2a, l)
        pa, pb, pc, pg = _inproj_call(xs, gm, wa, wb, wc, wg)
        oa, c_stack, n1, m1 = _mlstm_sample(pa, pg, gbias, na, state_mlstm_C, state_mlstm_n, m0rows, dseq,
                                            l, c_stack)
        ob, h_stack, buf1 = _ssd_sample(pb, pg, gbias, alog, dskip, cw, cb, nb, state_ssd, hist, dseq, l, h_stack)
        oc, s_stack = _ret_sample(pc, cos_s, sin_s, nc, state_ret, dseq, l, s_stack)
        xs = _mix_ffn_call(xs, oa, ob, oc, wo, g2, w1b, w3b, w2b, fin, final, l)
        for acc, v in zip(outs_s, (n1, m1[:, GATE_F:GATE_F + H_A], buf1)):
            acc.append(v)

    y_prompt = xp.reshape(bsz, seq, D_MODEL)
    y_sample = xs.reshape(dbsz, dseq, D_MODEL)
    s_n, s_m, s_buf = [jnp.stack(a) for a in outs_s]
    return (y_prompt, y_sample, *[jnp.stack(a) for a in outs_p], c_stack, s_n, s_m, h_stack, s_buf, s_stack)
```

```python
import functools
import math

import jax
import jax.numpy as jnp
from jax import lax
from jax.experimental import pallas as pl
from jax.experimental.pallas import tpu as pltpu

F32 = jnp.float32
BF16 = jnp.bfloat16
HIGHEST = lax.Precision.HIGHEST

D_MODEL = 1024
D_FF = 2816
GROUP_W = 512
H_A = 4
DH = 128
H_B = 8
P_B = 64
N_B = 128
G_B = 2
CONV_W = 4
CONV_DIM = GROUP_W + 2 * G_B * N_B
H_C = 4
CHUNK = 128
PAST_LEN = 16384
GATE_SOFTCAP = 15.0
ROPE_BASE = 10000.0
EPS = 1e-6
QK_SCALE = DH ** -0.5

LANES = 128
SUBLANES = 8

ROWS = 128
SEQ_PER_STEP = 8
TM_FFN = 1024
SEQ_UNROLL = 8
TM = 512
FC = 256
VMEM_LIMIT = 56 * 1024 * 1024

GATE_I = 0
GATE_F = 4
GATE_DT = 8
GATE_ROWS = 16


def _dot(a, b):
    return jnp.dot(a, b, preferred_element_type=F32)


def _dot_nt(a, b):
    return lax.dot_general(a, b, (((1,), (1,)), ((), ())), preferred_element_type=F32)


def _dot_tn(a, b):
    return lax.dot_general(a, b, (((0,), (0,)), ((), ())), preferred_element_type=F32)


def _dot_exact(a, b):
    return jnp.dot(a, b, precision=HIGHEST, preferred_element_type=F32)


def _rms(x, g):
    return x * lax.rsqrt(jnp.mean(x * x, axis=-1, keepdims=True) + EPS) * g


def _silu(x):
    return x * jax.nn.sigmoid(x)


def _log1p_exp_neg_abs(x):
    return jnp.log1p(jnp.exp(-jnp.abs(x)))


def _causal_mask(rows, lc):
    r = lax.broadcasted_iota(jnp.int32, (rows, rows), 0)
    c = lax.broadcasted_iota(jnp.int32, (rows, rows), 1)
    m = c <= r
    if lc != rows:
        shift = lc.bit_length() - 1
        m = m & ((r >> shift) == (c >> shift))
    return m, r, c


def _group_last(x, lc):
    rows, w = x.shape
    x3 = x.reshape(rows // lc, lc, w)
    return jnp.broadcast_to(x3[:, lc - 1:lc, :], x3.shape).reshape(rows, w)


def _group_first(x, lc):
    rows, w = x.shape
    return x.reshape(rows // lc, lc, w)[:, 0, :]


def _group_sum(x, lc):
    rows, w = x.shape
    return jnp.sum(x.reshape(rows // lc, lc, w), axis=1)


def _group_bcast(x, lc, rows):
    n, w = x.shape
    return jnp.broadcast_to(x[:, None, :], (n, lc, w)).reshape(rows, w)


def _ffn_core(x, g_ref, w1_ref, w3_ref, w2_ref, act_ref):
    h = _rms(x, g_ref[...]).astype(BF16)
    for c in range(D_FF // FC):
        cols = slice(c * FC, (c + 1) * FC)
        a = _dot(h, w1_ref[:, cols])
        b = _dot(h, w3_ref[:, cols])
        act_ref[:, cols] = (_silu(a) * b).astype(BF16)
    return x + 0.5 * _dot(act_ref[...], w2_ref[...])


def _ffn_body(x_ref, g_ref, w1_ref, w3_ref, w2_ref, o_ref, act_ref):
    o_ref[...] = _ffn_core(x_ref[...], g_ref, w1_ref, w3_ref, w2_ref, act_ref)


def _mix_ffn_body(x_ref, oa_ref, ob_ref, oc_ref, wo_ref, g_ref, w1_ref, w3_ref, w2_ref, fin_ref,
                  o_ref, act_ref, *, final):
    x = x_ref[...]
    x = x + (_dot(oa_ref[...], wo_ref[0:GROUP_W, :])
             + _dot(ob_ref[...], wo_ref[GROUP_W:2 * GROUP_W, :])
             + _dot(oc_ref[...], wo_ref[2 * GROUP_W:3 * GROUP_W, :]))
    y = _ffn_core(x, g_ref, w1_ref, w3_ref, w2_ref, act_ref)
    if final:
        y = _rms(y, fin_ref[...])
    o_ref[...] = y


def _inproj_body(x_ref, g_ref, wa_ref, wb_ref, wc_ref, wg_ref, pa_ref, pb_ref, pc_ref, pg_ref):
    h = _rms(x_ref[...], g_ref[...]).astype(BF16)
    pa_ref[...] = _dot(h, wa_ref[...])
    pb_ref[...] = _dot(h, wb_ref[...])
    pc_ref[...] = _dot(h, wc_ref[...])
    pg_ref[...] = _dot(h, wg_ref[...])


def _inproj_prompt_body(x_ref, g_ref, wta_ref, wtb_ref, wtc_ref, wtg_ref, wk_ref,
                        pta_ref, ptb_ref, ptc_ref, gt_ref, pk_ref):
    h = _rms(x_ref[...], g_ref[...]).astype(BF16)
    for w_ref, o_ref in ((wta_ref, pta_ref), (wtb_ref, ptb_ref), (wtc_ref, ptc_ref), (wtg_ref, gt_ref)):
        pt = _dot_nt(w_ref[...], h)
        for j in range(TM // ROWS):
            o_ref[j] = pt[:, j * ROWS:(j + 1) * ROWS]
    pk_ref[...] = _dot(h, wk_ref[...])


def _const_spec(shape):
    nd = len(shape)
    return pl.BlockSpec(shape, lambda *_: (0,) * nd, pipeline_mode=pl.Buffered(1))


def _layer_weight_spec(shape, layer):
    nd = len(shape)
    return pl.BlockSpec((None,) + shape, lambda *_: (layer,) + (0,) * nd, pipeline_mode=pl.Buffered(1))


def _row_spec(rows, cols):
    return pl.BlockSpec((rows, cols), lambda i: (i, 0))


def _dense_params():
    return pltpu.CompilerParams(dimension_semantics=("arbitrary",), vmem_limit_bytes=VMEM_LIMIT)


def _ffn_call(x, g, w1, w3, w2, layer):
    t = x.shape[0]
    return pl.pallas_call(
        _ffn_body,
        grid=(t // TM_FFN,),
        in_specs=[_row_spec(TM_FFN, D_MODEL), _layer_weight_spec((1, D_MODEL), layer),
                  _layer_weight_spec((D_MODEL, D_FF), layer), _layer_weight_spec((D_MODEL, D_FF), layer),
                  _layer_weight_spec((D_FF, D_MODEL), layer)],
        out_specs=_row_spec(TM_FFN, D_MODEL),
        out_shape=jax.ShapeDtypeStruct((t, D_MODEL), F32),
        scratch_shapes=[pltpu.VMEM((TM_FFN, D_FF), BF16)],
        compiler_params=_dense_params(),
        name="ffn",
    )(x, g, w1, w3, w2)


def _mix_ffn_call(x, oa, ob, oc, wo, g, w1, w3, w2, fin, final, layer):
    t = x.shape[0]
    return pl.pallas_call(
        functools.partial(_mix_ffn_body, final=final),
        grid=(t // TM_FFN,),
        in_specs=[_row_spec(TM_FFN, D_MODEL)] + [_row_spec(TM_FFN, GROUP_W)] * 3 + [
                  _layer_weight_spec((3 * GROUP_W, D_MODEL), layer), _layer_weight_spec((1, D_MODEL), layer),
                  _layer_weight_spec((D_MODEL, D_FF), layer), _layer_weight_spec((D_MODEL, D_FF), layer),
                  _layer_weight_spec((D_FF, D_MODEL), layer), _const_spec((1, D_MODEL))],
        out_specs=_row_spec(TM_FFN, D_MODEL),
        out_shape=jax.ShapeDtypeStruct((t, D_MODEL), F32),
        scratch_shapes=[pltpu.VMEM((TM_FFN, D_FF), BF16)],
        compiler_params=_dense_params(),
        name="mix_ffn",
    )(x, oa, ob, oc, wo, g, w1, w3, w2, fin)


def _inproj_call(x, g, wa, wb, wc, wg, layer):
    t = x.shape[0]
    widths = (wa.shape[2], wb.shape[2], wc.shape[2], wg.shape[2])
    return pl.pallas_call(
        _inproj_body,
        grid=(t // TM,),
        in_specs=([_row_spec(TM, D_MODEL), _layer_weight_spec((1, D_MODEL), layer)]
                  + [_layer_weight_spec((D_MODEL, w), layer) for w in widths]),
        out_specs=[_row_spec(TM, w) for w in widths],
        out_shape=[jax.ShapeDtypeStruct((t, w), F32) for w in widths],
        compiler_params=_dense_params(),
        name="inproj",
    )(x, g, wa, wb, wc, wg)


def _inproj_prompt_call(x, g, wta, wtb, wtc, wtg, wk, layer):
    t = x.shape[0]
    t_rows = (wta.shape[1], wtb.shape[1], wtc.shape[1], wtg.shape[1])
    n_cols = (wk.shape[2],)
    chunk_spec = lambda r: pl.BlockSpec((TM // ROWS, r, ROWS), lambda i: (i, 0, 0))
    return pl.pallas_call(
        _inproj_prompt_body,
        grid=(t // TM,),
        in_specs=([_row_spec(TM, D_MODEL), _layer_weight_spec((1, D_MODEL), layer)]
                  + [_layer_weight_spec((r, D_MODEL), layer) for r in t_rows]
                  + [_layer_weight_spec((D_MODEL, w), layer) for w in n_cols]),
        out_specs=[chunk_spec(r) for r in t_rows] + [_row_spec(TM, w) for w in n_cols],
        out_shape=([jax.ShapeDtypeStruct((t // ROWS, r, ROWS), F32) for r in t_rows]
                   + [jax.ShapeDtypeStruct((t, w), F32) for w in n_cols]),
        compiler_params=_dense_params(),
        name="inproj_prompt",
    )(x, g, wta, wtb, wtc, wtg, wk)


def _source_target_mask():
    src = lax.broadcasted_iota(jnp.int32, (ROWS, ROWS), 0)
    tgt = lax.broadcasted_iota(jnp.int32, (ROWS, ROWS), 1)
    return src <= tgt, src, tgt


def _last_lane(x):
    return jnp.broadcast_to(x[:, LANES - 1:LANES], x.shape)


def _mlstm_prompt_body(pt_ref, k_ref, gt_ref, gb_ref, normb_ref, out_ref, c_ref, n_ref, m_ref, ct_scr, *, nb):
    @pl.when(pl.program_id(1) == 0)
    def _init():
        ct_scr[...] = jnp.zeros_like(ct_scr)
        n_ref[...] = jnp.zeros_like(n_ref)
        m_ref[...] = jnp.zeros_like(m_ref)

    causal_t, _, _ = _source_target_mask()
    gates = [_mlstm_prompt_gates(gt_ref.at[sq], gb_ref, m_ref.at[sq], causal_t) for sq in range(nb)]
    units = [(sq, h) for sq in range(nb) for h in range(H_A)]

    st_raw, cq, qn, c_upd, n_upd, p_all = {}, {}, {}, {}, {}, {}
    for sq, h in units:
        c = GATE_F + h
        gs = gates[sq]
        qt = (pt_ref[sq, h * DH:(h + 1) * DH, :] * QK_SCALE).astype(BF16)
        vt = pt_ref[sq, GROUP_W + h * DH:GROUP_W + (h + 1) * DH, :]
        kf = k_ref[sq, :, h * DH:(h + 1) * DH]
        kb = kf.astype(BF16)
        n8 = jnp.concatenate([n_ref[sq, h:h + 1, :], jnp.zeros((SUBLANES - 1, DH), F32)], axis=0)
        st_raw[sq, h] = _dot(kb, qt)
        cq[sq, h] = _dot(ct_scr[sq, h].astype(BF16), qt)
        qn[sq, h] = _dot(n8.astype(BF16), qt)[0:1, :]
        c_upd[sq, h] = _dot((vt * gs["wend"][c:c + 1, :]).astype(BF16), kb)
        n_upd[sq, h] = _dot_exact(gs["wend"], kf)[c:c + 1, :]
        p_all[sq, h] = jnp.where(causal_t, jnp.exp(gs["ut"][:, c:c + 1] + gs["bm"][c:c + 1, :]), 0.0)

    num, den = {}, {}
    for sq, h in units:
        c = GATE_F + h
        gs = gates[sq]
        st = st_raw[sq, h] * p_all[sq, h]
        vt = pt_ref[sq, GROUP_W + h * DH:GROUP_W + (h + 1) * DH, :].astype(BF16)
        g = gs["gint"][c:c + 1, :]
        num[sq, h] = _dot(vt, st.astype(BF16)) + g * cq[sq, h]
        den[sq, h] = jnp.sum(st, axis=0, keepdims=True) + g * qn[sq, h]
        g1 = gs["gend"][c:c + 1, 0:1]
        ct_scr[sq, h] = g1 * ct_scr[sq, h] + c_upd[sq, h]
        n_ref[sq, h:h + 1, :] = g1 * n_ref[sq, h:h + 1, :] + n_upd[sq, h]

    for sq, h in units:
        c = GATE_F + h
        hs = slice(h * DH, (h + 1) * DH)
        ot = pt_ref[sq, 2 * GROUP_W + h * DH:2 * GROUP_W + (h + 1) * DH, :]
        hh = num[sq, h] * (1.0 / jnp.maximum(jnp.abs(den[sq, h]), gates[sq]["emt"][c:c + 1, :]))
        rs = lax.rsqrt(jnp.mean(hh * hh, axis=0, keepdims=True) + EPS)
        o = hh * rs * normb_ref[hs, :] * jax.nn.sigmoid(ot)
        out_ref[sq, :, hs] = o.T.astype(out_ref.dtype)

    @pl.when(pl.program_id(1) == pl.num_programs(1) - 1)
    def _finish():
        for sq in range(nb):
            for h in range(H_A):
                c_ref[sq, h] = ct_scr[sq, h].T


def _mlstm_prompt_gates(gt_ref, gb_ref, m_ref, causal_t):
    rowid = lax.broadcasted_iota(jnp.int32, (SUBLANES, LANES), 0)
    frows = rowid >= GATE_F

    pre = gt_ref[0:SUBLANES, :] + gb_ref[...]
    cap = GATE_SOFTCAP * jnp.tanh(pre / GATE_SOFTCAP)
    logf = jnp.minimum(cap, 0.0) - _log1p_exp_neg_abs(cap)
    bcum = _dot_exact(logf, jnp.where(causal_t, 1.0, 0.0))
    mprev = m_ref[...]
    inter = bcum + mprev
    u = pltpu.roll(cap, GATE_F - GATE_I, axis=0) - bcum
    ut = jnp.concatenate([u, jnp.zeros((ROWS - SUBLANES, LANES), F32)], axis=0).T
    mt_all = jnp.zeros((SUBLANES, LANES), F32)
    for h in range(H_A):
        c = GATE_F + h
        d = jnp.where(causal_t, ut[:, c:c + 1] + bcum[c:c + 1, :], -jnp.inf)
        mt = jnp.maximum(inter[c:c + 1, :], jnp.max(d, axis=0, keepdims=True))
        mt_all = jnp.where(rowid == c, mt, mt_all)
    blast = _last_lane(bcum)
    mend = _last_lane(mt_all)
    wend = jnp.where(frows, jnp.exp(blast + u - mend), 0.0)
    gend = jnp.where(frows, jnp.exp(blast + mprev - mend), 0.0)
    gint = jnp.where(frows, jnp.exp(inter - mt_all), 0.0)
    emt = jnp.exp(-mt_all)
    bm = bcum - mt_all
    m_ref[...] = jnp.where(frows, mend, 0.0)
    return dict(ut=ut, bm=bm, wend=wend, gend=gend, gint=gint, emt=emt)


def _mlstm_sample_body(*refs, lc, nseq):
    _mlstm_block(*refs[:7], *refs[8:], lc=lc, nseq=nseq)


def _mlstm_block(pa_ref, pg_ref, gb_ref, norm_ref, c0_ref, n0_ref, m0_ref,
                 out_ref, c_ref, n_ref, m_ref, qc_scr, kw_scr, g_scr, *, lc, nseq):
    rows = lc * nseq
    mprev = m0_ref[...]
    causal, _, _ = _causal_mask(rows, lc)
    lane = lax.broadcasted_iota(jnp.int32, (rows, LANES), 1)
    fcols = (lane >= GATE_F) & (lane < GATE_F + H_A)

    pre = pg_ref[...] + gb_ref[...]
    cap = GATE_SOFTCAP * jnp.tanh(pre / GATE_SOFTCAP)
    logf = jnp.minimum(cap, 0.0) - _log1p_exp_neg_abs(cap)
    bcum = _dot_exact(jnp.where(causal, 1.0, 0.0), logf)
    inter = bcum + mprev
    u = pltpu.roll(cap, GATE_F - GATE_I, axis=1) - bcum
    ut = u.T
    mt_all = jnp.zeros((rows, LANES), F32)
    for h in range(H_A):
        c = GATE_F + h
        d = jnp.where(causal, bcum[:, c:c + 1] + ut[c:c + 1, :], -jnp.inf)
        mt = jnp.maximum(inter[:, c:c + 1], jnp.max(d, axis=1, keepdims=True))
        mt_all = jnp.where(lane == c, mt, mt_all)
    blast = _group_last(bcum, lc)
    mend = _group_last(mt_all, lc)
    wend = jnp.where(fcols, jnp.exp(blast + u - mend), 0.0)
    gend = jnp.where(fcols, jnp.exp(blast + mprev - mend), 0.0)
    gint = jnp.where(fcols, jnp.exp(inter - mt_all), 0.0)
    emt = jnp.exp(-mt_all)
    bm = bcum - mt_all

    def head_cols(group, h):
        return slice(group * GROUP_W + h * DH, group * GROUP_W + (h + 1) * DH)

    g_scr[...] = gend
    for h in range(H_A):
        c = GATE_F + h
        kw_scr[h] = pa_ref[:, head_cols(1, h)] * wend[:, c:c + 1]

    def seq_body(j, carry):
        r0 = pl.multiple_of(j * lc, lc)
        grow = g_scr[pl.ds(r0, 1), :]
        for h in range(H_A):
            c = GATE_F + h
            qj = (pa_ref[pl.ds(r0, lc), head_cols(0, h)] * QK_SCALE).astype(BF16)
            vj = pa_ref[pl.ds(r0, lc), head_cols(2, h)].astype(BF16)
            kwj = kw_scr[h, pl.ds(r0, lc), :].astype(BF16)
            c_old = c0_ref[j, h]
            qc_scr[h, pl.ds(r0, lc), :] = _dot(qj, c_old.astype(BF16))
            c_ref[j, h] = grow[:, c:c + 1] * c_old + _dot_tn(kwj, vj)
        return carry

    lax.fori_loop(0, nseq, seq_body, 0, unroll=SEQ_UNROLL)
    n_rows = []
    gfirst = _group_first(gend, lc)
    for h in range(H_A):
        c = GATE_F + h
        n_old = n0_ref[:, h, :]
        n_rows.append(_group_bcast(n_old, lc, rows))
        n_ref[:, h, :] = gfirst[:, c:c + 1] * n_old + _group_sum(kw_scr[h], lc)
    m_ref[...] = _group_first(mend, lc)

    scores = []
    for h in range(H_A):
        qb = (pa_ref[:, head_cols(0, h)] * QK_SCALE).astype(BF16)
        scores.append(_dot_nt(qb, pa_ref[:, head_cols(1, h)].astype(BF16)))
    for h in range(H_A):
        c = GATE_F + h
        qf = pa_ref[:, head_cols(0, h)] * QK_SCALE
        v = pa_ref[:, head_cols(2, h)].astype(BF16)
        og = pa_ref[:, head_cols(3, h)]
        p = jnp.where(causal, jnp.exp(bm[:, c:c + 1] + ut[c:c + 1, :]), 0.0)
        s = scores[h] * p
        g = gint[:, c:c + 1]
        num = _dot(s.astype(BF16), v) + g * qc_scr[h]
        qn = jnp.sum(qf * n_rows[h], axis=1, keepdims=True)
        den = jnp.sum(s, axis=1, keepdims=True) + g * qn
        hh = num / jnp.maximum(jnp.abs(den), emt[:, c:c + 1])
        hn = hh * lax.rsqrt(jnp.mean(hh * hh, axis=1, keepdims=True) + EPS) * norm_ref[:, h * DH:(h + 1) * DH]
        out_ref[:, h * DH:(h + 1) * DH] = (hn * jax.nn.sigmoid(og)).astype(out_ref.dtype)


def _mixer_params(ndims):
    return pltpu.CompilerParams(dimension_semantics=("arbitrary",) * ndims, vmem_limit_bytes=VMEM_LIMIT)


def _seq_spec(cols):
    return pl.BlockSpec((SEQ_PER_STEP, ROWS, cols), lambda b, c: (b, c, 0))


def _seq_state_spec(shape):
    nd = len(shape)
    return pl.BlockSpec((SEQ_PER_STEP,) + shape, lambda b, c: (b,) + (0,) * nd)


def _seq_t_spec(rows):
    return pl.BlockSpec((SEQ_PER_STEP, None, rows, ROWS), lambda b, c: (b, c, 0, 0))


def _seq_col_spec(cols, col_block):
    return pl.BlockSpec((SEQ_PER_STEP, ROWS, cols), lambda b, c: (b, c, col_block))


def _mlstm_prompt(pt, pk, gt, gb8, normb, bsz, nchunk, layer):
    seq = nchunk * ROWS
    out, c1, n1, m1 = pl.pallas_call(
        functools.partial(_mlstm_prompt_body, nb=SEQ_PER_STEP),
        grid=(bsz // SEQ_PER_STEP, nchunk),
        in_specs=[_seq_t_spec(3 * GROUP_W), _seq_col_spec(GROUP_W, 0), _seq_t_spec(GATE_ROWS),
                  _layer_weight_spec((SUBLANES, LANES), layer), _layer_weight_spec((GROUP_W, LANES), layer)],
        out_specs=[_seq_spec(GROUP_W), _seq_state_spec((H_A, DH, DH)), _seq_state_spec((H_A, DH)),
                   _seq_state_spec((SUBLANES, LANES))],
        out_shape=[jax.ShapeDtypeStruct((bsz, seq, GROUP_W), BF16),
                   jax.ShapeDtypeStruct((bsz, H_A, DH, DH), F32),
                   jax.ShapeDtypeStruct((bsz, H_A, DH), F32),
                   jax.ShapeDtypeStruct((bsz, SUBLANES, LANES), F32)],
        scratch_shapes=[pltpu.VMEM((SEQ_PER_STEP, H_A, DH, DH), F32)],
        compiler_params=_mixer_params(2),
        name="mlstm_prompt",
    )(pt.reshape(bsz, nchunk, 3 * GROUP_W, ROWS), pk.reshape(bsz, seq, -1),
      gt.reshape(bsz, nchunk, GATE_ROWS, ROWS), gb8, normb)
    return out.reshape(bsz * seq, GROUP_W), c1, n1, m1


def _layer_state_spec(shape, layer):
    nd = len(shape)
    return pl.BlockSpec((None,) + shape, lambda i: (layer, i) + (0,) * (nd - 1))


_STACK_SPEC = pl.BlockSpec(memory_space=pl.ANY)


def _mlstm_sample(pa, pg, gbias, norm, c_all, n_all, m0rows, lc, layer, c_prev):
    t = pa.shape[0]
    nseq = ROWS // lc
    bsz = c_all.shape[1]
    row = lambda i: (i, 0)
    return pl.pallas_call(
        functools.partial(_mlstm_sample_body, lc=lc, nseq=nseq),
        grid=(t // ROWS,),
        in_specs=[pl.BlockSpec((ROWS, 4 * GROUP_W), row), pl.BlockSpec((ROWS, LANES), row),
                  _layer_weight_spec((1, LANES), layer), _layer_weight_spec((1, GROUP_W), layer),
                  _layer_state_spec((nseq, H_A, DH, DH), layer),
                  _layer_state_spec((nseq, H_A, DH), layer),
                  _layer_state_spec((ROWS, LANES), layer), _STACK_SPEC],
        out_specs=[pl.BlockSpec((ROWS, GROUP_W), row),
                   _layer_state_spec((nseq, H_A, DH, DH), layer),
                   pl.BlockSpec((nseq, H_A, DH), lambda i: (i, 0, 0)),
                   pl.BlockSpec((nseq, LANES), row)],
        out_shape=[jax.ShapeDtypeStruct((t, GROUP_W), BF16),
                   jax.ShapeDtypeStruct(c_all.shape, F32),
                   jax.ShapeDtypeStruct((bsz, H_A, DH), F32),
                   jax.ShapeDtypeStruct((bsz, LANES), F32)],
        input_output_aliases={7: 1},
        scratch_shapes=[pltpu.VMEM((H_A, ROWS, DH), F32), pltpu.VMEM((H_A, ROWS, DH), F32),
                        pltpu.VMEM((ROWS, LANES), F32)],
        compiler_params=_mixer_params(1),
        name="mlstm_sample",
    )(pa, pg, gbias, norm, c_all, n_all, m0rows, c_prev)


LOG_GAMMA = [math.log(1.0 - 2.0 ** (-5.0 - h)) for h in range(H_C)]


def _ret_prompt_body(pt_ref, k_ref, cost_ref, sint_ref, cos_ref, sin_ref, dec_ref, normb_ref, out_ref, s_ref,
                     st_scr, *, nb):
    @pl.when(pl.program_id(1) == 0)
    def _init():
        st_scr[...] = jnp.zeros_like(st_scr)

    tin = lax.broadcasted_iota(jnp.int32, (1, ROWS), 1).astype(F32)
    cost, sint = cost_ref[...], sint_ref[...]
    cos, sin = cos_ref[...], sin_ref[...]
    units = [(sq, h) for sq in range(nb) for h in range(H_C)]

    st_raw, qs, s_upd = {}, {}, {}
    for sq, h in units:
        lg = LOG_GAMMA[h]
        qt = pt_ref[sq, h * DH:(h + 1) * DH, :]
        vt = pt_ref[sq, GROUP_W + h * DH:GROUP_W + (h + 1) * DH, :]
        kf = k_ref[sq, :, h * DH:(h + 1) * DH]
        qr = (qt * cost + pltpu.roll(qt, DH // 2, axis=0) * sint).astype(BF16)
        kr = ((kf * cos + pltpu.roll(kf, DH // 2, axis=1) * sin) * QK_SCALE).astype(BF16)
        ve = (vt * jnp.exp((ROWS - 1.0 - tin) * lg)).astype(BF16)
        st_raw[sq, h] = _dot(kr, qr)
        qs[sq, h] = _dot(st_scr[sq, h].astype(BF16), qr)
        s_upd[sq, h] = _dot(ve, kr)

    o_all = {}
    for sq, h in units:
        lg = LOG_GAMMA[h]
        vt = pt_ref[sq, GROUP_W + h * DH:GROUP_W + (h + 1) * DH, :].astype(BF16)
        st = (st_raw[sq, h] * dec_ref[h]).astype(BF16)
        o_all[sq, h] = _dot(vt, st) + jnp.exp((tin + 1.0) * lg) * qs[sq, h]
        st_scr[sq, h] = math.exp(ROWS * lg) * st_scr[sq, h] + s_upd[sq, h]

    for sq, h in units:
        hs = slice(h * DH, (h + 1) * DH)
        gt = pt_ref[sq, 2 * GROUP_W + h * DH:2 * GROUP_W + (h + 1) * DH, :]
        o = o_all[sq, h]
        rs = lax.rsqrt(jnp.mean(o * o, axis=0, keepdims=True) + EPS)
        on = o * rs * normb_ref[hs, :] * _silu(gt)
        out_ref[sq, :, hs] = on.T.astype(out_ref.dtype)

    @pl.when(pl.program_id(1) == pl.num_programs(1) - 1)
    def _finish():
        for sq in range(nb):
            for h in range(H_C):
                s_ref[sq, h] = st_scr[sq, h].T


def _ret_sample_body(*refs, lc, nseq):
    _ret_block(*refs[:5], *refs[6:], lc=lc, nseq=nseq)


def _ret_block(pc_ref, cos_ref, sin_ref, norm_ref, s0_ref, out_ref, s_ref, qs_scr, qr_scr, ke_scr, *, lc, nseq):
    rows = lc * nseq
    causal, r, c = _causal_mask(rows, lc)
    diff = (r - c).astype(F32)
    tin = (lax.broadcasted_iota(jnp.int32, (rows, 1), 0) & (lc - 1)).astype(F32)
    cos = cos_ref[...]
    sin = sin_ref[...]

    def head_cols(group, h):
        return slice(group * GROUP_W + h * DH, group * GROUP_W + (h + 1) * DH)

    def rot(x):
        return x * cos + pltpu.roll(x, DH // 2, axis=1) * sin

    for h in range(H_C):
        qr_scr[h] = rot(pc_ref[:, head_cols(0, h)])
        ke_scr[h] = rot(pc_ref[:, head_cols(1, h)]) * QK_SCALE * jnp.exp((lc - 1.0 - tin) * LOG_GAMMA[h])

    def seq_body(j, carry):
        r0 = pl.multiple_of(j * lc, lc)
        for h in range(H_C):
            qj = qr_scr[h, pl.ds(r0, lc), :].astype(BF16)
            kj = ke_scr[h, pl.ds(r0, lc), :].astype(BF16)
            vj = pc_ref[pl.ds(r0, lc), head_cols(2, h)].astype(BF16)
            s_old = s0_ref[j, h]
            qs_scr[h, pl.ds(r0, lc), :] = _dot(qj, s_old.astype(BF16))
            s_ref[j, h] = math.exp(lc * LOG_GAMMA[h]) * s_old + _dot_tn(kj, vj)
        return carry

    lax.fori_loop(0, nseq, seq_body, 0, unroll=SEQ_UNROLL)

    scores = []
    for h in range(H_C):
        kr = (rot(pc_ref[:, head_cols(1, h)]) * QK_SCALE).astype(BF16)
        scores.append(_dot_nt(qr_scr[h].astype(BF16), kr))
    for h in range(H_C):
        lg = LOG_GAMMA[h]
        decay = jnp.where(causal, jnp.exp(diff * lg), 0.0)
        v = pc_ref[:, head_cols(2, h)].astype(BF16)
        gate = pc_ref[:, head_cols(3, h)]
        o = _dot((scores[h] * decay).astype(BF16), v) + jnp.exp((tin + 1.0) * lg) * qs_scr[h]
        on = o * lax.rsqrt(jnp.mean(o * o, axis=1, keepdims=True) + EPS) * norm_ref[:, h * DH:(h + 1) * DH]
        out_ref[:, h * DH:(h + 1) * DH] = (on * _silu(gate)).astype(out_ref.dtype)


def _ret_prompt(pt, pk, cos, sin, normb, bsz, nchunk, layer):
    seq = nchunk * ROWS
    chunk = lambda b, c: (c, 0)
    chunk_t = lambda b, c: (0, c)
    idx = jnp.arange(ROWS, dtype=F32)
    diff = idx[None, :] - idx[:, None]
    log_gamma = jnp.asarray(LOG_GAMMA, F32)[:, None, None]
    dec = jnp.where(diff >= 0, jnp.exp(diff * log_gamma), 0.0)
    out, s1 = pl.pallas_call(
        functools.partial(_ret_prompt_body, nb=SEQ_PER_STEP),
        grid=(bsz // SEQ_PER_STEP, nchunk),
        in_specs=[_seq_t_spec(3 * GROUP_W), _seq_col_spec(GROUP_W, 1),
                  pl.BlockSpec((DH, ROWS), chunk_t), pl.BlockSpec((DH, ROWS), chunk_t),
                  pl.BlockSpec((ROWS, DH), chunk), pl.BlockSpec((ROWS, DH), chunk),
                  pl.BlockSpec((H_C, ROWS, ROWS), lambda b, c: (0, 0, 0)),
                  _layer_weight_spec((GROUP_W, LANES), layer)],
        out_specs=[_seq_spec(GROUP_W), _seq_state_spec((H_C, DH, DH))],
        out_shape=[jax.ShapeDtypeStruct((bsz, seq, GROUP_W), BF16),
                   jax.ShapeDtypeStruct((bsz, H_C, DH, DH), F32)],
        scratch_shapes=[pltpu.VMEM((SEQ_PER_STEP, H_C, DH, DH), F32)],
        compiler_params=_mixer_params(2),
        name="ret_prompt",
    )(pt.reshape(bsz, nchunk, 3 * GROUP_W, ROWS), pk.reshape(bsz, seq, -1), cos.T, sin.T, cos, sin, dec, normb)
    return out.reshape(bsz * seq, GROUP_W), s1


def _ret_sample(pc, cos, sin, norm, s_all, lc, layer, s_prev):
    t = pc.shape[0]
    nseq = ROWS // lc
    row = lambda i: (i, 0)
    const = lambda i: (0, 0)
    return pl.pallas_call(
        functools.partial(_ret_sample_body, lc=lc, nseq=nseq),
        grid=(t // ROWS,),
        in_specs=[pl.BlockSpec((ROWS, 4 * GROUP_W), row), pl.BlockSpec((ROWS, DH), const),
                  pl.BlockSpec((ROWS, DH), const), _layer_weight_spec((1, GROUP_W), layer),
                  _layer_state_spec((nseq, H_C, DH, DH), layer), _STACK_SPEC],
        out_specs=[pl.BlockSpec((ROWS, GROUP_W), row),
                   _layer_state_spec((nseq, H_C, DH, DH), layer)],
        out_shape=[jax.ShapeDtypeStruct((t, GROUP_W), BF16),
                   jax.ShapeDtypeStruct(s_all.shape, F32)],
        input_output_aliases={5: 1},
        scratch_shapes=[pltpu.VMEM((H_C, ROWS, DH), F32)] * 3,
        compiler_params=_mixer_params(1),
        name="ret_sample",
    )(pc, cos, sin, norm, s_all, s_prev)


HEADS_PER_GROUP = H_B // G_B
PAIR_W = 2 * P_B
GROUP_CH = HEADS_PER_GROUP * P_B


def _ssd_prompt_body(pt_ref, gt_ref, gb_ref, alog_ref, dskipb_ref, cwb_ref, cbb_ref, normb_ref,
                     out_ref, h_ref, tail_ref, xc_scr, *, nb):
    @pl.when(pl.program_id(1) == 0)
    def _init():
        h_ref[...] = jnp.zeros_like(h_ref)
        tail_ref[...] = jnp.zeros_like(tail_ref)

    causal_t, _, _ = _source_target_mask()
    lane = lax.broadcasted_iota(jnp.int32, (DH, ROWS), 1)
    rowid = lax.broadcasted_iota(jnp.int32, (GATE_ROWS, LANES), 0)
    drows = rowid >= GATE_DT
    pad = jnp.zeros((ROWS - GATE_ROWS, LANES), F32)

    gates = []
    for sq in range(nb):
        for blk in range(CONV_DIM // DH):
            ch = slice(blk * DH, (blk + 1) * DH)
            new = pt_ref[sq, GROUP_W + blk * DH:GROUP_W + (blk + 1) * DH, :]
            prev = tail_ref[sq, ch, :]
            acc = new * cwb_ref[CONV_W - 1, ch, :]
            for k in range(1, CONV_W):
                shifted = pltpu.roll(jnp.where(lane >= ROWS - k, prev, new), k, axis=1)
                acc = acc + shifted * cwb_ref[CONV_W - 1 - k, ch, :]
            xc_scr[sq, ch, :] = _silu(acc + cbb_ref[ch, :])
            tail_ref[sq, ch, :] = new

        dpre = gt_ref[sq] + gb_ref[...]
        dt = jnp.maximum(dpre, 0.0) + _log1p_exp_neg_abs(dpre)
        adt = dt * (-jnp.exp(alog_ref[...]))
        acum = jnp.where(drows, _dot_exact(adt, jnp.where(causal_t, 1.0, 0.0)), 0.0)
        alast = _last_lane(acum)
        gates.append(dict(dt=dt, acum=acum, at=jnp.concatenate([acum, pad], axis=0).T, exp_a=jnp.exp(acum),
                          wx=jnp.exp(alast - acum), g_a=jnp.exp(alast)))

    def x_rows(hd):
        return slice(hd * P_B, (hd + 1) * P_B)

    def b_rows(g):
        return slice(GROUP_W + g * N_B, GROUP_W + (g + 1) * N_B)

    def c_rows(g):
        return slice(GROUP_W + G_B * N_B + g * N_B, GROUP_W + G_B * N_B + (g + 1) * N_B)

    groups = [(sq, g) for sq in range(nb) for g in range(G_B)]
    cbs, chs, upds, xdts = {}, {}, {}, {}
    for sq, g in groups:
        gs = gates[sq]
        bg = xc_scr[sq, b_rows(g), :].T.astype(BF16)
        ct = xc_scr[sq, c_rows(g), :].astype(BF16)
        xws = []
        for r in range(HEADS_PER_GROUP):
            hd = g * HEADS_PER_GROUP + r
            c = GATE_DT + hd
            xdt = xc_scr[sq, x_rows(hd), :] * gs["dt"][c:c + 1, :]
            xdts[sq, hd] = xdt.astype(BF16)
            xws.append((xdt * gs["wx"][c:c + 1, :]).astype(BF16))
        h_old = jnp.concatenate([h_ref[sq, g * HEADS_PER_GROUP + r] for r in range(HEADS_PER_GROUP)], axis=0)
        cbs[sq, g] = _dot(bg, ct)
        chs[sq, g] = _dot(h_old.astype(BF16), ct)
        upds[sq, g] = _dot(jnp.concatenate(xws, axis=0), bg)

    ys = {}
    for sq, g in groups:
        gs = gates[sq]
        for r in range(HEADS_PER_GROUP):
            hd = g * HEADS_PER_GROUP + r
            c = GATE_DT + hd
            dec = jnp.where(causal_t, jnp.exp(gs["acum"][c:c + 1, :] - gs["at"][:, c:c + 1]), 0.0)
            ys[sq, hd] = _dot(xdts[sq, hd], (cbs[sq, g] * dec).astype(BF16))
            h_ref[sq, hd] = gs["g_a"][c:c + 1, 0:1] * h_ref[sq, hd] + upds[sq, g][r * P_B:(r + 1) * P_B, :]

    for sq, g in groups:
        gs = gates[sq]
        parts = []
        for r in range(HEADS_PER_GROUP):
            hd = g * HEADS_PER_GROUP + r
            c = GATE_DT + hd
            y = ys[sq, hd] + gs["exp_a"][c:c + 1, :] * chs[sq, g][r * P_B:(r + 1) * P_B, :]
            parts.append(y + dskipb_ref[x_rows(hd), :] * xc_scr[sq, x_rows(hd), :])
        grows = slice(g * GROUP_CH, (g + 1) * GROUP_CH)
        yz = jnp.concatenate(parts, axis=0) * _silu(pt_ref[sq, grows, :])
        yn = yz * lax.rsqrt(jnp.mean(yz * yz, axis=0, keepdims=True) + EPS) * normb_ref[grows, :]
        for j in range(GROUP_CH // DH):
            cols = slice(g * GROUP_CH + j * DH, g * GROUP_CH + (j + 1) * DH)
            out_ref[sq, :, cols] = yn[j * DH:(j + 1) * DH, :].T.astype(out_ref.dtype)


def _ssd_sample_body(*refs, lc, nseq):
    _ssd_block(*refs[:10], *refs[11:], lc=lc, nseq=nseq)


def _ssd_block(pb_ref, pg_ref, gb_ref, alog_ref, dskip_ref, cw_ref, cb_ref, norm_ref, h0_ref, hist_ref,
               out_ref, h_ref, buf_ref, ch_scr, xc_scr, xw_scr, g_scr, *, lc, nseq):
    rows = lc * nseq
    causal, _, _ = _causal_mask(rows, lc)
    lane = lax.broadcasted_iota(jnp.int32, (rows, LANES), 1)
    dcols = (lane >= GATE_DT) & (lane < GATE_DT + H_B)
    low = lax.broadcasted_iota(jnp.int32, (rows, PAIR_W), 1) < P_B

    new = pb_ref[:, GROUP_W:GROUP_W + CONV_DIM]
    tin = lax.broadcasted_iota(jnp.int32, (rows, 1), 0) & (lc - 1)
    acc = new * cw_ref[CONV_W - 1:CONV_W, :]
    for k in range(1, CONV_W):
        rolled = pltpu.roll(new, k, axis=0)
        hist = pltpu.roll(hist_ref[...], (rows + k - lc) % rows, axis=0)
        acc = acc + jnp.where(tin >= k, rolled, hist) * cw_ref[CONV_W - 1 - k:CONV_W - k, :]
    xc = _silu(acc + cb_ref[...])
    buf_ref[...] = new.reshape(nseq, lc, CONV_DIM)[:, lc - (CONV_W - 1):lc, :]

    dpre = pg_ref[...] + gb_ref[...]
    dt = jnp.maximum(dpre, 0.0) + _log1p_exp_neg_abs(dpre)
    adt = dt * (-jnp.exp(alog_ref[...]))
    acum = jnp.where(dcols, _dot_exact(jnp.where(causal, 1.0, 0.0), adt), 0.0)
    at = acum.T
    alast = _group_last(acum, lc)
    exp_a = jnp.exp(acum)
    wx = jnp.exp(alast - acum)
    g_a = jnp.exp(alast)

    def pair_bcast(slab, c0):
        return jnp.where(low, slab[:, c0:c0 + 1], slab[:, c0 + 1:c0 + 2])

    def pair_cols(g, p):
        start = g * GROUP_CH + p * PAIR_W
        return slice(start, start + PAIR_W)

    def b_cols(g):
        return slice(GROUP_W + g * N_B, GROUP_W + (g + 1) * N_B)

    def c_cols(g):
        return slice(GROUP_W + G_B * N_B + g * N_B, GROUP_W + G_B * N_B + (g + 1) * N_B)

    xw_pairs = {}
    xdt_pairs = {}
    for g in range(G_B):
        for p in range(HEADS_PER_GROUP // 2):
            c0 = GATE_DT + g * HEADS_PER_GROUP + 2 * p
            xdt = xc[:, pair_cols(g, p)] * pair_bcast(dt, c0)
            xdt_pairs[g, p] = xdt.astype(BF16)
            xw_pairs[g, p] = xdt * pair_bcast(wx, c0)
    xc_scr[...] = xc
    g_scr[...] = g_a
    for g in range(G_B):
        for p in range(HEADS_PER_GROUP // 2):
            xw_scr[:, pair_cols(g, p)] = xw_pairs[g, p]

    def seq_body(j, carry):
        r0 = pl.multiple_of(j * lc, lc)
        grow = g_scr[pl.ds(r0, 1), :]
        for g in range(G_B):
            bj = xc_scr[pl.ds(r0, lc), b_cols(g)].astype(BF16)
            cj = xc_scr[pl.ds(r0, lc), c_cols(g)].astype(BF16)
            xwj = xw_scr[pl.ds(r0, lc), g * GROUP_CH:(g + 1) * GROUP_CH].astype(BF16)
            hs = [h0_ref[j, g * HEADS_PER_GROUP + r] for r in range(HEADS_PER_GROUP)]
            h_old = jnp.concatenate(hs, axis=0)
            ch_scr[g, pl.ds(r0, lc), :] = _dot_nt(cj, h_old.astype(BF16))
            upd = _dot_tn(xwj, bj)
            for r in range(HEADS_PER_GROUP):
                hd = g * HEADS_PER_GROUP + r
                c = GATE_DT + hd
                h_ref[j, hd] = grow[:, c:c + 1] * hs[r] + upd[r * P_B:(r + 1) * P_B, :]
        return carry

    lax.fori_loop(0, nseq, seq_body, 0, unroll=SEQ_UNROLL)

    cbs = [_dot_nt(xc[:, c_cols(g)].astype(BF16), xc[:, b_cols(g)].astype(BF16)) for g in range(G_B)]
    for g in range(G_B):
        cb = cbs[g]
        ys = []
        for p in range(HEADS_PER_GROUP // 2):
            c0 = GATE_DT + g * HEADS_PER_GROUP + 2 * p
            xpair = xc[:, pair_cols(g, p)]
            xdt = xdt_pairs[g, p]
            halves = []
            for c in (c0, c0 + 1):
                dec = jnp.where(causal, jnp.exp(acum[:, c:c + 1] - at[c:c + 1, :]), 0.0)
                halves.append(_dot((cb * dec).astype(BF16), xdt))
            y = (jnp.where(low, halves[0], halves[1])
                 + pair_bcast(exp_a, c0) * ch_scr[g, :, p * PAIR_W:(p + 1) * PAIR_W])
            ys.append(y + dskip_ref[:, pair_cols(g, p)] * xpair)
        yg = jnp.concatenate(ys, axis=1)
        gcols = slice(g * GROUP_CH, (g + 1) * GROUP_CH)
        yz = yg * _silu(pb_ref[:, gcols])
        yn = yz * lax.rsqrt(jnp.mean(yz * yz, axis=1, keepdims=True) + EPS) * norm_ref[:, gcols]
        out_ref[:, gcols] = yn.astype(out_ref.dtype)


def _ssd_prompt(pt, gt, gb16, alog16, dskipb, cwb, cbb, normb, bsz, nchunk, layer):
    seq = nchunk * ROWS
    out, h1, tail = pl.pallas_call(
        functools.partial(_ssd_prompt_body, nb=SEQ_PER_STEP),
        grid=(bsz // SEQ_PER_STEP, nchunk),
        in_specs=[_seq_t_spec(GROUP_W + CONV_DIM), _seq_t_spec(GATE_ROWS),
                  _layer_weight_spec((GATE_ROWS, LANES), layer), _layer_weight_spec((GATE_ROWS, LANES), layer),
                  _layer_weight_spec((GROUP_W, LANES), layer), _layer_weight_spec((CONV_W, CONV_DIM, LANES), layer),
                  _layer_weight_spec((CONV_DIM, LANES), layer), _layer_weight_spec((GROUP_W, LANES), layer)],
        out_specs=[_seq_spec(GROUP_W), _seq_state_spec((H_B, P_B, N_B)), _seq_state_spec((CONV_DIM, ROWS))],
        out_shape=[jax.ShapeDtypeStruct((bsz, seq, GROUP_W), BF16),
                   jax.ShapeDtypeStruct((bsz, H_B, P_B, N_B), F32),
                   jax.ShapeDtypeStruct((bsz, CONV_DIM, ROWS), F32)],
        scratch_shapes=[pltpu.VMEM((SEQ_PER_STEP, CONV_DIM, ROWS), F32)],
        compiler_params=_mixer_params(2),
        name="ssd_prompt",
    )(pt.reshape(bsz, nchunk, GROUP_W + CONV_DIM, ROWS), gt.reshape(bsz, nchunk, GATE_ROWS, ROWS),
      gb16, alog16, dskipb, cwb, cbb, normb)
    return out.reshape(bsz * seq, GROUP_W), h1, tail


def _ssd_sample(pb, pg, gbias, alog, dskip, cw, cb, norm, h_all, hist, lc, layer, h_prev):
    t = pb.shape[0]
    nseq = ROWS // lc
    bsz = h_all.shape[1]
    row = lambda i: (i, 0)
    return pl.pallas_call(
        functools.partial(_ssd_sample_body, lc=lc, nseq=nseq),
        grid=(t // ROWS,),
        in_specs=[pl.BlockSpec((ROWS, GROUP_W + CONV_DIM), row), pl.BlockSpec((ROWS, LANES), row),
                  _layer_weight_spec((1, LANES), layer), _layer_weight_spec((1, LANES), layer),
                  _layer_weight_spec((1, GROUP_W), layer), _layer_weight_spec((CONV_W, CONV_DIM), layer),
                  _layer_weight_spec((1, CONV_DIM), layer), _layer_weight_spec((1, GROUP_W), layer),
                  _layer_state_spec((nseq, H_B, P_B, N_B), layer),
                  _layer_state_spec((ROWS, CONV_DIM), layer), _STACK_SPEC],
        out_specs=[pl.BlockSpec((ROWS, GROUP_W), row),
                   _layer_state_spec((nseq, H_B, P_B, N_B), layer),
                   pl.BlockSpec((nseq, CONV_W - 1, CONV_DIM), lambda i: (i, 0, 0))],
        out_shape=[jax.ShapeDtypeStruct((t, GROUP_W), BF16),
                   jax.ShapeDtypeStruct(h_all.shape, F32),
                   jax.ShapeDtypeStruct((bsz, CONV_W - 1, CONV_DIM), F32)],
        input_output_aliases={10: 1},
        scratch_shapes=[pltpu.VMEM((G_B, ROWS, GROUP_CH), F32), pltpu.VMEM((ROWS, CONV_DIM), F32),
                        pltpu.VMEM((ROWS, GROUP_W), F32), pltpu.VMEM((ROWS, LANES), F32)],
        compiler_params=_mixer_params(1),
        name="ssd_sample",
    )(pb, pg, gbias, alog, dskip, cw, cb, norm, h_all, hist, h_prev)


def _rope_tables(pos):
    half = DH // 2
    freqs = ROPE_BASE ** (-jnp.arange(half, dtype=F32) / half)
    ang = pos.astype(F32)[:, None] * freqs
    cos, sin = jnp.cos(ang), jnp.sin(ang)
    return jnp.concatenate([cos, cos], axis=-1), jnp.concatenate([-sin, sin], axis=-1)


def kernel(x_prompt, x_sample, state_mlstm_C, state_mlstm_n, state_mlstm_m, state_ssd, state_conv, state_ret,
           ffn1_norm, ffn1_w1, ffn1_w3, ffn1_w2, mix_norm, w_in, b_igate, b_fgate, mlstm_norm,
           conv_w, conv_b, dt_bias, a_log, d_skip, ssd_norm, ret_norm, w_out,
           ffn2_norm, ffn2_w1, ffn2_w3, ffn2_w2, final_norm):
    depth = w_in.shape[0]
    bsz, seq, _ = x_prompt.shape
    dbsz, dseq, _ = x_sample.shape
    assert seq % CHUNK == 0 and ROWS % dseq == 0 and (dbsz * dseq) % ROWS == 0
    assert dseq >= SUBLANES and dseq & (dseq - 1) == 0
    nchunk = seq // CHUNK
    seq_per_block = ROWS // dseq

    xp = x_prompt.reshape(bsz * seq, D_MODEL)
    xs = x_sample.reshape(dbsz * dseq, D_MODEL)

    cos_p, sin_p = _rope_tables(jnp.arange(seq))
    cos_s, sin_s = _rope_tables(PAST_LEN + jnp.arange(dseq))
    cos_s, sin_s = jnp.tile(cos_s, (seq_per_block, 1)), jnp.tile(sin_s, (seq_per_block, 1))

    a0 = 0
    a_gate = a0 + 4 * GROUP_W
    b0 = a_gate + 2 * H_A
    b_dt = b0 + GROUP_W + CONV_DIM
    c0 = b_dt + H_B

    outs_p = [[] for _ in range(6)]
    outs_s = [[] for _ in range(3)]
    c_stack = jnp.zeros(state_mlstm_C.shape, F32)
    h_stack = jnp.zeros(state_ssd.shape, F32)
    s_stack = jnp.zeros(state_ret.shape, F32)
    w1a, w3a, w2a = ffn1_w1.astype(BF16), ffn1_w3.astype(BF16), ffn1_w2.astype(BF16)
    w1b, w3b, w2b = ffn2_w1.astype(BF16), ffn2_w3.astype(BF16), ffn2_w2.astype(BF16)
    wo = w_out.astype(BF16)
    fin = final_norm[None, :]

    row3 = lambda v: v.astype(F32)[:, None, :]
    lanes_bcast = lambda v: jnp.broadcast_to(v.astype(F32)[..., None], v.shape + (LANES,))
    transposed = lambda w: jnp.swapaxes(w, 1, 2).astype(BF16)
    cols = lambda lo, hi: w_in[:, :, lo:hi]
    qa, ka, va, ga = (cols(a0 + i * GROUP_W, a0 + (i + 1) * GROUP_W) for i in range(4))
    qc, kc, vc, gc = (cols(c0 + i * GROUP_W, c0 + (i + 1) * GROUP_W) for i in range(4))
    w_gates = jnp.concatenate([cols(a_gate, b0), cols(b_dt, c0)], axis=2)
    n_gates = w_gates.shape[2]
    zeros_dt = jnp.zeros((depth, GATE_DT), F32)
    g1, gm, g2 = row3(ffn1_norm), row3(mix_norm), row3(ffn2_norm)
    wa, wb, wc = cols(a0, a_gate).astype(BF16), cols(b0, b_dt).astype(BF16), cols(c0, c0 + 4 * GROUP_W).astype(BF16)
    wg = jnp.pad(w_gates, ((0, 0), (0, 0), (0, LANES - n_gates))).astype(BF16)
    gate_row = lambda parts: row3(jnp.pad(jnp.concatenate(parts, axis=1), ((0, 0), (0, LANES - n_gates))))
    gbias = gate_row([b_igate, b_fgate, dt_bias])
    alog = gate_row([zeros_dt, a_log])
    dskip_ch = jnp.repeat(d_skip.astype(F32), P_B, axis=1)
    dskip, na, nb, nc = row3(dskip_ch), row3(mlstm_norm), row3(ssd_norm), row3(ret_norm)
    cw, cb = conv_w.astype(F32), row3(conv_b)
    m0rows = jnp.pad(jnp.repeat(state_mlstm_m.astype(F32), dseq, axis=1),
                     ((0, 0), (0, 0), (GATE_F, LANES - GATE_F - H_A)))
    hist = jnp.pad(state_conv.astype(F32), ((0, 0), (0, 0), (dseq - (CONV_W - 1), 0), (0, 0)))
    hist = hist.reshape(depth, dbsz * dseq, CONV_DIM)
    wta = transposed(jnp.concatenate([qa, va, ga], axis=2))
    wtb = transposed(cols(b0, b_dt))
    wtc = transposed(jnp.concatenate([qc, vc, gc], axis=2))
    wtg = transposed(w_gates)
    wk = jnp.concatenate([ka, kc], axis=2).astype(BF16)
    gb8 = lanes_bcast(jnp.concatenate([b_igate, b_fgate], axis=1))
    gb16 = lanes_bcast(jnp.concatenate([zeros_dt, dt_bias], axis=1))
    alog16 = lanes_bcast(jnp.concatenate([zeros_dt, a_log], axis=1))
    nab, nbb, ncb = lanes_bcast(mlstm_norm), lanes_bcast(ssd_norm), lanes_bcast(ret_norm)
    dskipb, cwb, cbb = lanes_bcast(dskip_ch), lanes_bcast(conv_w), lanes_bcast(conv_b)

    for l in range(depth):
        final = l == depth - 1

        xp = _ffn_call(xp, g1, w1a, w3a, w2a, l)
        pta, ptb, ptc, gt, pk = _inproj_prompt_call(xp, gm, wta, wtb, wtc, wtg, wk, l)
        oa, c1, n1, m1 = _mlstm_prompt(pta, pk, gt, gb8, nab, bsz, nchunk, l)
        ob, h1, tail = _ssd_prompt(ptb, gt, gb16, alog16, dskipb, cwb, cbb, nbb, bsz, nchunk, l)
        oc, s1 = _ret_prompt(ptc, pk, cos_p, sin_p, ncb, bsz, nchunk, l)
        xp = _mix_ffn_call(xp, oa, ob, oc, wo, g2, w1b, w3b, w2b, fin, final, l)
        buf1 = jnp.swapaxes(tail[:, :, ROWS - (CONV_W - 1):], 1, 2)
        for acc, v in zip(outs_p, (c1, n1, m1[:, GATE_F:GATE_F + H_A, 0], h1, buf1, s1)):
            acc.append(v)

        xs = _ffn_call(xs, g1, w1a, w3a, w2a, l)
        pa, pb, pc, pg = _inproj_call(xs, gm, wa, wb, wc, wg, l)
        oa, c_stack, n1, m1 = _mlstm_sample(pa, pg, gbias, na, state_mlstm_C, state_mlstm_n, m0rows, dseq,
                                            l, c_stack)
        ob, h_stack, buf1 = _ssd_sample(pb, pg, gbias, alog, dskip, cw, cb, nb, state_ssd, hist, dseq, l, h_stack)
        oc, s_stack = _ret_sample(pc, cos_s, sin_s, nc, state_ret, dseq, l, s_stack)
        xs = _mix_ffn_call(xs, oa, ob, oc, wo, g2, w1b, w3b, w2b, fin, final, l)
        for acc, v in zip(outs_s, (n1, m1[:, GATE_F:GATE_F + H_A], buf1)):
            acc.append(v)

    y_prompt = xp.reshape(bsz, seq, D_MODEL)
    y_sample = xs.reshape(dbsz, dseq, D_MODEL)
    s_n, s_m, s_buf = [jnp.stack(a) for a in outs_s]
    return (y_prompt, y_sample, *[jnp.stack(a) for a in outs_p], c_stack, s_n, s_m, h_stack, s_buf, s_stack)
```

```python
import functools
import math

import jax
import jax.numpy as jnp
from jax import lax
from jax.experimental import pallas as pl
from jax.experimental.pallas import tpu as pltpu

F32 = jnp.float32
BF16 = jnp.bfloat16
HIGHEST = lax.Precision.HIGHEST

D_MODEL = 1024
D_FF = 2816
GROUP_W = 512
H_A = 4
DH = 128
H_B = 8
P_B = 64
N_B = 128
G_B = 2
CONV_W = 4
CONV_DIM = GROUP_W + 2 * G_B * N_B
H_C = 4
CHUNK = 128
PAST_LEN = 16384
GATE_SOFTCAP = 15.0
ROPE_BASE = 10000.0
EPS = 1e-6
QK_SCALE = DH ** -0.5

LANES = 128
SUBLANES = 8

ROWS = 128
SEQ_PER_STEP = 8
TM_FFN = 1024
SEQ_UNROLL = 8
TM = 512
FC = 256
VMEM_LIMIT = 56 * 1024 * 1024

GATE_I = 0
GATE_F = 4
GATE_DT = 8
GATE_ROWS = 16


def _dot(a, b):
    return jnp.dot(a, b, preferred_element_type=F32)


def _dot_nt(a, b):
    return lax.dot_general(a, b, (((1,), (1,)), ((), ())), preferred_element_type=F32)


def _dot_tn(a, b):
    return lax.dot_general(a, b, (((0,), (0,)), ((), ())), preferred_element_type=F32)


def _dot_exact(a, b):
    return jnp.dot(a, b, precision=HIGHEST, preferred_element_type=F32)


def _rms(x, g):
    return x * lax.rsqrt(jnp.mean(x * x, axis=-1, keepdims=True) + EPS) * g


def _silu(x):
    return x * jax.nn.sigmoid(x)


def _log1p_exp_neg_abs(x):
    return jnp.log1p(jnp.exp(-jnp.abs(x)))


def _causal_mask(rows, lc):
    r = lax.broadcasted_iota(jnp.int32, (rows, rows), 0)
    c = lax.broadcasted_iota(jnp.int32, (rows, rows), 1)
    m = c <= r
    if lc != rows:
        shift = lc.bit_length() - 1
        m = m & ((r >> shift) == (c >> shift))
    return m, r, c


def _group_last(x, lc):
    rows, w = x.shape
    x3 = x.reshape(rows // lc, lc, w)
    return jnp.broadcast_to(x3[:, lc - 1:lc, :], x3.shape).reshape(rows, w)


def _group_first(x, lc):
    rows, w = x.shape
    return x.reshape(rows // lc, lc, w)[:, 0, :]


def _group_sum(x, lc):
    rows, w = x.shape
    return jnp.sum(x.reshape(rows // lc, lc, w), axis=1)


def _group_bcast(x, lc, rows):
    n, w = x.shape
    return jnp.broadcast_to(x[:, None, :], (n, lc, w)).reshape(rows, w)


def _ffn_core(x, g_ref, w1_ref, w3_ref, w2_ref, act_ref):
    h = _rms(x, g_ref[...]).astype(BF16)
    for c in range(D_FF // FC):
        cols = slice(c * FC, (c + 1) * FC)
        a = _dot(h, w1_ref[:, cols])
        b = _dot(h, w3_ref[:, cols])
        act_ref[:, cols] = (_silu(a) * b).astype(BF16)
    return x + 0.5 * _dot(act_ref[...], w2_ref[...])


def _ffn_body(x_ref, g_ref, w1_ref, w3_ref, w2_ref, o_ref, act_ref):
    o_ref[...] = _ffn_core(x_ref[...], g_ref, w1_ref, w3_ref, w2_ref, act_ref)


def _mix_ffn_body(x_ref, oa_ref, ob_ref, oc_ref, wo_ref, g_ref, w1_ref, w3_ref, w2_ref, fin_ref,
                  o_ref, act_ref, *, final):
    x = x_ref[...]
    x = x + (_dot(oa_ref[...], wo_ref[0:GROUP_W, :])
             + _dot(ob_ref[...], wo_ref[GROUP_W:2 * GROUP_W, :])
             + _dot(oc_ref[...], wo_ref[2 * GROUP_W:3 * GROUP_W, :]))
    y = _ffn_core(x, g_ref, w1_ref, w3_ref, w2_ref, act_ref)
    if final:
        y = _rms(y, fin_ref[...])
    o_ref[...] = y


def _inproj_body(x_ref, g_ref, wa_ref, wb_ref, wc_ref, wg_ref, pa_ref, pb_ref, pc_ref, pg_ref):
    h = _rms(x_ref[...], g_ref[...]).astype(BF16)
    pa_ref[...] = _dot(h, wa_ref[...])
    pb_ref[...] = _dot(h, wb_ref[...])
    pc_ref[...] = _dot(h, wc_ref[...])
    pg_ref[...] = _dot(h, wg_ref[...])


def _inproj_prompt_body(x_ref, g_ref, wta_ref, wtb_ref, wtc_ref, wtg_ref, wk_ref,
                        pta_ref, ptb_ref, ptc_ref, gt_ref, pk_ref):
    h = _rms(x_ref[...], g_ref[...]).astype(BF16)
    for w_ref, o_ref in ((wta_ref, pta_ref), (wtb_ref, ptb_ref), (wtc_ref, ptc_ref), (wtg_ref, gt_ref)):
        pt = _dot_nt(w_ref[...], h)
        for j in range(TM // ROWS):
            o_ref[j] = pt[:, j * ROWS:(j + 1) * ROWS]
    pk_ref[...] = _dot(h, wk_ref[...])


def _const_spec(shape):
    nd = len(shape)
    return pl.BlockSpec(shape, lambda *_: (0,) * nd, pipeline_mode=pl.Buffered(1))


def _layer_weight_spec(shape, layer):
    nd = len(shape)
    return pl.BlockSpec((None,) + shape, lambda *_: (layer,) + (0,) * nd, pipeline_mode=pl.Buffered(1))


def _row_spec(rows, cols):
    return pl.BlockSpec((rows, cols), lambda i: (i, 0))


def _dense_params():
    return pltpu.CompilerParams(dimension_semantics=("arbitrary",), vmem_limit_bytes=VMEM_LIMIT)


def _ffn_call(x, g, w1, w3, w2, layer):
    t = x.shape[0]
    return pl.pallas_call(
        _ffn_body,
        grid=(t // TM_FFN,),
        in_specs=[_row_spec(TM_FFN, D_MODEL), _layer_weight_spec((1, D_MODEL), layer),
                  _layer_weight_spec((D_MODEL, D_FF), layer), _layer_weight_spec((D_MODEL, D_FF), layer),
                  _layer_weight_spec((D_FF, D_MODEL), layer)],
        out_specs=_row_spec(TM_FFN, D_MODEL),
        out_shape=jax.ShapeDtypeStruct((t, D_MODEL), F32),
        scratch_shapes=[pltpu.VMEM((TM_FFN, D_FF), BF16)],
        compiler_params=_dense_params(),
        name="ffn",
    )(x, g, w1, w3, w2)


def _mix_ffn_call(x, oa, ob, oc, wo, g, w1, w3, w2, fin, final, layer):
    t = x.shape[0]
    return pl.pallas_call(
        functools.partial(_mix_ffn_body, final=final),
        grid=(t // TM_FFN,),
        in_specs=[_row_spec(TM_FFN, D_MODEL)] + [_row_spec(TM_FFN, GROUP_W)] * 3 + [
                  _layer_weight_spec((3 * GROUP_W, D_MODEL), layer), _layer_weight_spec((1, D_MODEL), layer),
                  _layer_weight_spec((D_MODEL, D_FF), layer), _layer_weight_spec((D_MODEL, D_FF), layer),
                  _layer_weight_spec((D_FF, D_MODEL), layer), _const_spec((1, D_MODEL))],
        out_specs=_row_spec(TM_FFN, D_MODEL),
        out_shape=jax.ShapeDtypeStruct((t, D_MODEL), F32),
        scratch_shapes=[pltpu.VMEM((TM_FFN, D_FF), BF16)],
        compiler_params=_dense_params(),
        name="mix_ffn",
    )(x, oa, ob, oc, wo, g, w1, w3, w2, fin)


def _inproj_call(x, g, wa, wb, wc, wg, layer):
    t = x.shape[0]
    widths = (wa.shape[2], wb.shape[2], wc.shape[2], wg.shape[2])
    return pl.pallas_call(
        _inproj_body,
        grid=(t // TM,),
        in_specs=([_row_spec(TM, D_MODEL), _layer_weight_spec((1, D_MODEL), layer)]
                  + [_layer_weight_spec((D_MODEL, w), layer) for w in widths]),
        out_specs=[_row_spec(TM, w) for w in widths],
        out_shape=[jax.ShapeDtypeStruct((t, w), F32) for w in widths],
        compiler_params=_dense_params(),
        name="inproj",
    )(x, g, wa, wb, wc, wg)


def _inproj_prompt_call(x, g, wta, wtb, wtc, wtg, wk, layer):
    t = x.shape[0]
    t_rows = (wta.shape[1], wtb.shape[1], wtc.shape[1], wtg.shape[1])
    n_cols = (wk.shape[2],)
    chunk_spec = lambda r: pl.BlockSpec((TM // ROWS, r, ROWS), lambda i: (i, 0, 0))
    return pl.pallas_call(
        _inproj_prompt_body,
        grid=(t // TM,),
        in_specs=([_row_spec(TM, D_MODEL), _layer_weight_spec((1, D_MODEL), layer)]
                  + [_layer_weight_spec((r, D_MODEL), layer) for r in t_rows]
                  + [_layer_weight_spec((D_MODEL, w), layer) for w in n_cols]),
        out_specs=[chunk_spec(r) for r in t_rows] + [_row_spec(TM, w) for w in n_cols],
        out_shape=([jax.ShapeDtypeStruct((t // ROWS, r, ROWS), F32) for r in t_rows]
                   + [jax.ShapeDtypeStruct((t, w), F32) for w in n_cols]),
        compiler_params=_dense_params(),
        name="inproj_prompt",
    )(x, g, wta, wtb, wtc, wtg, wk)


def _source_target_mask():
    src = lax.broadcasted_iota(jnp.int32, (ROWS, ROWS), 0)
    tgt = lax.broadcasted_iota(jnp.int32, (ROWS, ROWS), 1)
    return src <= tgt, src, tgt


def _last_lane(x):
    return jnp.broadcast_to(x[:, LANES - 1:LANES], x.shape)


def _cumsum_lanes(x):
    lane = lax.broadcasted_iota(jnp.int32, x.shape, 1)
    k = 1
    while k < x.shape[1]:
        x = x + jnp.where(lane >= k, pltpu.roll(x, k, axis=1), 0.0)
        k *= 2
    return x


def _mlstm_prompt_body(pt_ref, k_ref, gt_ref, gb_ref, normb_ref, out_ref, c_ref, n_ref, m_ref, ct_scr, *, nb):
    @pl.when(pl.program_id(1) == 0)
    def _init():
        ct_scr[...] = jnp.zeros_like(ct_scr)
        n_ref[...] = jnp.zeros_like(n_ref)
        m_ref[...] = jnp.zeros_like(m_ref)

    causal_t, _, _ = _source_target_mask()
    pre = (gt_ref[:, 0:SUBLANES, :] + gb_ref[...][None]).reshape(nb * SUBLANES, LANES)
    cap_all = GATE_SOFTCAP * jnp.tanh(pre / GATE_SOFTCAP)
    bcum_all = _cumsum_lanes(jnp.minimum(cap_all, 0.0) - _log1p_exp_neg_abs(cap_all))
    gates = []
    for sq in range(nb):
        rows8 = slice(sq * SUBLANES, (sq + 1) * SUBLANES)
        gates.append(_mlstm_prompt_gates(cap_all[rows8], bcum_all[rows8], m_ref.at[sq], causal_t))
    units = [(sq, h) for sq in range(nb) for h in range(H_A)]

    st_raw, cq, qn, c_upd, n_upd, p_all = {}, {}, {}, {}, {}, {}
    for sq, h in units:
        c = GATE_F + h
        gs = gates[sq]
        qt = (pt_ref[sq, h * DH:(h + 1) * DH, :] * QK_SCALE).astype(BF16)
        vt = pt_ref[sq, GROUP_W + h * DH:GROUP_W + (h + 1) * DH, :]
        kf = k_ref[sq, :, h * DH:(h + 1) * DH]
        kb = kf.astype(BF16)
        n8 = jnp.concatenate([n_ref[sq, h:h + 1, :], jnp.zeros((SUBLANES - 1, DH), F32)], axis=0)
        st_raw[sq, h] = _dot(kb, qt)
        cq[sq, h] = _dot(ct_scr[sq, h].astype(BF16), qt)
        qn[sq, h] = _dot(n8.astype(BF16), qt)[0:1, :]
        c_upd[sq, h] = _dot((vt * gs["wend"][c:c + 1, :]).astype(BF16), kb)
        n_upd[sq, h] = jnp.sum(kf * gs["wt"][:, c:c + 1], axis=0, keepdims=True)
        p_all[sq, h] = jnp.where(causal_t, jnp.exp(gs["ut"][:, c:c + 1] + gs["bm"][c:c + 1, :]), 0.0)

    num, den = {}, {}
    for sq, h in units:
        c = GATE_F + h
        gs = gates[sq]
        st = st_raw[sq, h] * p_all[sq, h]
        vt = pt_ref[sq, GROUP_W + h * DH:GROUP_W + (h + 1) * DH, :].astype(BF16)
        g = gs["gint"][c:c + 1, :]
        num[sq, h] = _dot(vt, st.astype(BF16)) + g * cq[sq, h]
        den[sq, h] = jnp.sum(st, axis=0, keepdims=True) + g * qn[sq, h]
        g1 = gs["gend"][c:c + 1, 0:1]
        ct_scr[sq, h] = g1 * ct_scr[sq, h] + c_upd[sq, h]
        n_ref[sq, h:h + 1, :] = g1 * n_ref[sq, h:h + 1, :] + n_upd[sq, h]

    for sq, h in units:
        c = GATE_F + h
        hs = slice(h * DH, (h + 1) * DH)
        ot = pt_ref[sq, 2 * GROUP_W + h * DH:2 * GROUP_W + (h + 1) * DH, :]
        hh = num[sq, h] * (1.0 / jnp.maximum(jnp.abs(den[sq, h]), gates[sq]["emt"][c:c + 1, :]))
        rs = lax.rsqrt(jnp.mean(hh * hh, axis=0, keepdims=True) + EPS)
        o = hh * rs * normb_ref[hs, :] * jax.nn.sigmoid(ot)
        out_ref[sq, :, hs] = o.T.astype(out_ref.dtype)

    @pl.when(pl.program_id(1) == pl.num_programs(1) - 1)
    def _finish():
        for sq in range(nb):
            for h in range(H_A):
                c_ref[sq, h] = ct_scr[sq, h].T


def _mlstm_prompt_gates(cap, bcum, m_ref, causal_t):
    rowid = lax.broadcasted_iota(jnp.int32, (SUBLANES, LANES), 0)
    frows = rowid >= GATE_F
    mprev = m_ref[...]
    inter = bcum + mprev
    u = pltpu.roll(cap, GATE_F - GATE_I, axis=0) - bcum
    ut = jnp.concatenate([u, jnp.zeros((ROWS - SUBLANES, LANES), F32)], axis=0).T
    mt_all = jnp.zeros((SUBLANES, LANES), F32)
    for h in range(H_A):
        c = GATE_F + h
        d = jnp.where(causal_t, ut[:, c:c + 1] + bcum[c:c + 1, :], -jnp.inf)
        mt = jnp.maximum(inter[c:c + 1, :], jnp.max(d, axis=0, keepdims=True))
        mt_all = jnp.where(rowid == c, mt, mt_all)
    blast = _last_lane(bcum)
    mend = _last_lane(mt_all)
    wend = jnp.where(frows, jnp.exp(blast + u - mend), 0.0)
    gend = jnp.where(frows, jnp.exp(blast + mprev - mend), 0.0)
    gint = jnp.where(frows, jnp.exp(inter - mt_all), 0.0)
    emt = jnp.exp(-mt_all)
    bm = bcum - mt_all
    m_ref[...] = jnp.where(frows, mend, 0.0)
    wt = jnp.concatenate([wend, jnp.zeros((ROWS - SUBLANES, LANES), F32)], axis=0).T
    return dict(ut=ut, bm=bm, wend=wend, wt=wt, gend=gend, gint=gint, emt=emt)


def _mlstm_sample_body(*refs, lc, nseq):
    _mlstm_block(*refs[:7], *refs[8:], lc=lc, nseq=nseq)


def _mlstm_block(pa_ref, pg_ref, gb_ref, norm_ref, c0_ref, n0_ref, m0_ref,
                 out_ref, c_ref, n_ref, m_ref, qc_scr, kw_scr, g_scr, *, lc, nseq):
    rows = lc * nseq
    mprev = m0_ref[...]
    causal, _, _ = _causal_mask(rows, lc)
    lane = lax.broadcasted_iota(jnp.int32, (rows, LANES), 1)
    fcols = (lane >= GATE_F) & (lane < GATE_F + H_A)

    pre = pg_ref[...] + gb_ref[...]
    cap = GATE_SOFTCAP * jnp.tanh(pre / GATE_SOFTCAP)
    logf = jnp.minimum(cap, 0.0) - _log1p_exp_neg_abs(cap)
    bcum = _dot_exact(jnp.where(causal, 1.0, 0.0), logf)
    inter = bcum + mprev
    u = pltpu.roll(cap, GATE_F - GATE_I, axis=1) - bcum
    ut = u.T
    mt_all = jnp.zeros((rows, LANES), F32)
    for h in range(H_A):
        c = GATE_F + h
        d = jnp.where(causal, bcum[:, c:c + 1] + ut[c:c + 1, :], -jnp.inf)
        mt = jnp.maximum(inter[:, c:c + 1], jnp.max(d, axis=1, keepdims=True))
        mt_all = jnp.where(lane == c, mt, mt_all)
    blast = _group_last(bcum, lc)
    mend = _group_last(mt_all, lc)
    wend = jnp.where(fcols, jnp.exp(blast + u - mend), 0.0)
    gend = jnp.where(fcols, jnp.exp(blast + mprev - mend), 0.0)
    gint = jnp.where(fcols, jnp.exp(inter - mt_all), 0.0)
    emt = jnp.exp(-mt_all)
    bm = bcum - mt_all

    def head_cols(group, h):
        return slice(group * GROUP_W + h * DH, group * GROUP_W + (h + 1) * DH)

    g_scr[...] = gend
    for h in range(H_A):
        c = GATE_F + h
        kw_scr[h] = pa_ref[:, head_cols(1, h)] * wend[:, c:c + 1]

    def seq_body(j, carry):
        r0 = pl.multiple_of(j * lc, lc)
        grow = g_scr[pl.ds(r0, 1), :]
        for h in range(H_A):
            c = GATE_F + h
            qj = (pa_ref[pl.ds(r0, lc), head_cols(0, h)] * QK_SCALE).astype(BF16)
            vj = pa_ref[pl.ds(r0, lc), head_cols(2, h)].astype(BF16)
            kwj = kw_scr[h, pl.ds(r0, lc), :].astype(BF16)
            c_old = c0_ref[j, h]
            qc_scr[h, pl.ds(r0, lc), :] = _dot(qj, c_old.astype(BF16))
            c_ref[j, h] = grow[:, c:c + 1] * c_old + _dot_tn(kwj, vj)
        return carry

    lax.fori_loop(0, nseq, seq_body, 0, unroll=SEQ_UNROLL)
    n_rows = []
    gfirst = _group_first(gend, lc)
    for h in range(H_A):
        c = GATE_F + h
        n_old = n0_ref[:, h, :]
        n_rows.append(_group_bcast(n_old, lc, rows))
        n_ref[:, h, :] = gfirst[:, c:c + 1] * n_old + _group_sum(kw_scr[h], lc)
    m_ref[...] = _group_first(mend, lc)

    scores = []
    for h in range(H_A):
        qb = (pa_ref[:, head_cols(0, h)] * QK_SCALE).astype(BF16)
        scores.append(_dot_nt(qb, pa_ref[:, head_cols(1, h)].astype(BF16)))
    for h in range(H_A):
        c = GATE_F + h
        qf = pa_ref[:, head_cols(0, h)] * QK_SCALE
        v = pa_ref[:, head_cols(2, h)].astype(BF16)
        og = pa_ref[:, head_cols(3, h)]
        p = jnp.where(causal, jnp.exp(bm[:, c:c + 1] + ut[c:c + 1, :]), 0.0)
        s = scores[h] * p
        g = gint[:, c:c + 1]
        num = _dot(s.astype(BF16), v) + g * qc_scr[h]
        qn = jnp.sum(qf * n_rows[h], axis=1, keepdims=True)
        den = jnp.sum(s, axis=1, keepdims=True) + g * qn
        hh = num / jnp.maximum(jnp.abs(den), emt[:, c:c + 1])
        hn = hh * lax.rsqrt(jnp.mean(hh * hh, axis=1, keepdims=True) + EPS) * norm_ref[:, h * DH:(h + 1) * DH]
        out_ref[:, h * DH:(h + 1) * DH] = (hn * jax.nn.sigmoid(og)).astype(out_ref.dtype)


def _mixer_params(ndims):
    return pltpu.CompilerParams(dimension_semantics=("arbitrary",) * ndims, vmem_limit_bytes=VMEM_LIMIT)


def _seq_spec(cols):
    return pl.BlockSpec((SEQ_PER_STEP, ROWS, cols), lambda b, c: (b, c, 0))


def _seq_state_spec(shape):
    nd = len(shape)
    return pl.BlockSpec((SEQ_PER_STEP,) + shape, lambda b, c: (b,) + (0,) * nd)


def _seq_t_spec(rows):
    return pl.BlockSpec((SEQ_PER_STEP, None, rows, ROWS), lambda b, c: (b, c, 0, 0))


def _seq_col_spec(cols, col_block):
    return pl.BlockSpec((SEQ_PER_STEP, ROWS, cols), lambda b, c: (b, c, col_block))


def _mlstm_prompt(pt, pk, gt, gb8, normb, bsz, nchunk, layer):
    seq = nchunk * ROWS
    out, c1, n1, m1 = pl.pallas_call(
        functools.partial(_mlstm_prompt_body, nb=SEQ_PER_STEP),
        grid=(bsz // SEQ_PER_STEP, nchunk),
        in_specs=[_seq_t_spec(3 * GROUP_W), _seq_col_spec(GROUP_W, 0), _seq_t_spec(GATE_ROWS),
                  _layer_weight_spec((SUBLANES, LANES), layer), _layer_weight_spec((GROUP_W, LANES), layer)],
        out_specs=[_seq_spec(GROUP_W), _seq_state_spec((H_A, DH, DH)), _seq_state_spec((H_A, DH)),
                   _seq_state_spec((SUBLANES, LANES))],
        out_shape=[jax.ShapeDtypeStruct((bsz, seq, GROUP_W), BF16),
                   jax.ShapeDtypeStruct((bsz, H_A, DH, DH), F32),
                   jax.ShapeDtypeStruct((bsz, H_A, DH), F32),
                   jax.ShapeDtypeStruct((bsz, SUBLANES, LANES), F32)],
        scratch_shapes=[pltpu.VMEM((SEQ_PER_STEP, H_A, DH, DH), F32)],
        compiler_params=_mixer_params(2),
        name="mlstm_prompt",
    )(pt.reshape(bsz, nchunk, 3 * GROUP_W, ROWS), pk.reshape(bsz, seq, -1),
      gt.reshape(bsz, nchunk, GATE_ROWS, ROWS), gb8, normb)
    return out.reshape(bsz * seq, GROUP_W), c1, n1, m1


def _layer_state_spec(shape, layer):
    nd = len(shape)
    return pl.BlockSpec((None,) + shape, lambda i: (layer, i) + (0,) * (nd - 1))


_STACK_SPEC = pl.BlockSpec(memory_space=pl.ANY)


def _mlstm_sample(pa, pg, gbias, norm, c_all, n_all, m0rows, lc, layer, c_prev):
    t = pa.shape[0]
    nseq = ROWS // lc
    bsz = c_all.shape[1]
    row = lambda i: (i, 0)
    return pl.pallas_call(
        functools.partial(_mlstm_sample_body, lc=lc, nseq=nseq),
        grid=(t // ROWS,),
        in_specs=[pl.BlockSpec((ROWS, 4 * GROUP_W), row), pl.BlockSpec((ROWS, LANES), row),
                  _layer_weight_spec((1, LANES), layer), _layer_weight_spec((1, GROUP_W), layer),
                  _layer_state_spec((nseq, H_A, DH, DH), layer),
                  _layer_state_spec((nseq, H_A, DH), layer),
                  _layer_state_spec((ROWS, LANES), layer), _STACK_SPEC],
        out_specs=[pl.BlockSpec((ROWS, GROUP_W), row),
                   _layer_state_spec((nseq, H_A, DH, DH), layer),
                   pl.BlockSpec((nseq, H_A, DH), lambda i: (i, 0, 0)),
                   pl.BlockSpec((nseq, LANES), row)],
        out_shape=[jax.ShapeDtypeStruct((t, GROUP_W), BF16),
                   jax.ShapeDtypeStruct(c_all.shape, F32),
                   jax.ShapeDtypeStruct((bsz, H_A, DH), F32),
                   jax.ShapeDtypeStruct((bsz, LANES), F32)],
        input_output_aliases={7: 1},
        scratch_shapes=[pltpu.VMEM((H_A, ROWS, DH), F32), pltpu.VMEM((H_A, ROWS, DH), F32),
                        pltpu.VMEM((ROWS, LANES), F32)],
        compiler_params=_mixer_params(1),
        name="mlstm_sample",
    )(pa, pg, gbias, norm, c_all, n_all, m0rows, c_prev)


LOG_GAMMA = [math.log(1.0 - 2.0 ** (-5.0 - h)) for h in range(H_C)]


def _ret_prompt_body(pt_ref, k_ref, cost_ref, sint_ref, cos_ref, sin_ref, dec_ref, normb_ref, out_ref, s_ref,
                     st_scr, *, nb):
    @pl.when(pl.program_id(1) == 0)
    def _init():
        st_scr[...] = jnp.zeros_like(st_scr)

    tin = lax.broadcasted_iota(jnp.int32, (1, ROWS), 1).astype(F32)
    cost, sint = cost_ref[...], sint_ref[...]
    cos, sin = cos_ref[...], sin_ref[...]
    units = [(sq, h) for sq in range(nb) for h in range(H_C)]

    st_raw, qs, s_upd = {}, {}, {}
    for sq, h in units:
        lg = LOG_GAMMA[h]
        qt = pt_ref[sq, h * DH:(h + 1) * DH, :]
        vt = pt_ref[sq, GROUP_W + h * DH:GROUP_W + (h + 1) * DH, :]
        kf = k_ref[sq, :, h * DH:(h + 1) * DH]
        qr = (qt * cost + pltpu.roll(qt, DH // 2, axis=0) * sint).astype(BF16)
        kr = ((kf * cos + pltpu.roll(kf, DH // 2, axis=1) * sin) * QK_SCALE).astype(BF16)
        ve = (vt * jnp.exp((ROWS - 1.0 - tin) * lg)).astype(BF16)
        st_raw[sq, h] = _dot(kr, qr)
        qs[sq, h] = _dot(st_scr[sq, h].astype(BF16), qr)
        s_upd[sq, h] = _dot(ve, kr)

    o_all = {}
    for sq, h in units:
        lg = LOG_GAMMA[h]
        vt = pt_ref[sq, GROUP_W + h * DH:GROUP_W + (h + 1) * DH, :].astype(BF16)
        st = (st_raw[sq, h] * dec_ref[h]).astype(BF16)
        o_all[sq, h] = _dot(vt, st) + jnp.exp((tin + 1.0) * lg) * qs[sq, h]
        st_scr[sq, h] = math.exp(ROWS * lg) * st_scr[sq, h] + s_upd[sq, h]

    for sq, h in units:
        hs = slice(h * DH, (h + 1) * DH)
        gt = pt_ref[sq, 2 * GROUP_W + h * DH:2 * GROUP_W + (h + 1) * DH, :]
        o = o_all[sq, h]
        rs = lax.rsqrt(jnp.mean(o * o, axis=0, keepdims=True) + EPS)
        on = o * rs * normb_ref[hs, :] * _silu(gt)
        out_ref[sq, :, hs] = on.T.astype(out_ref.dtype)

    @pl.when(pl.program_id(1) == pl.num_programs(1) - 1)
    def _finish():
        for sq in range(nb):
            for h in range(H_C):
                s_ref[sq, h] = st_scr[sq, h].T


def _ret_sample_body(*refs, lc, nseq):
    _ret_block(*refs[:5], *refs[6:], lc=lc, nseq=nseq)


def _ret_block(pc_ref, cos_ref, sin_ref, norm_ref, s0_ref, out_ref, s_ref, qs_scr, qr_scr, ke_scr, *, lc, nseq):
    rows = lc * nseq
    causal, r, c = _causal_mask(rows, lc)
    diff = (r - c).astype(F32)
    tin = (lax.broadcasted_iota(jnp.int32, (rows, 1), 0) & (lc - 1)).astype(F32)
    cos = cos_ref[...]
    sin = sin_ref[...]

    def head_cols(group, h):
        return slice(group * GROUP_W + h * DH, group * GROUP_W + (h + 1) * DH)

    def rot(x):
        return x * cos + pltpu.roll(x, DH // 2, axis=1) * sin

    for h in range(H_C):
        qr_scr[h] = rot(pc_ref[:, head_cols(0, h)])
        ke_scr[h] = rot(pc_ref[:, head_cols(1, h)]) * QK_SCALE * jnp.exp((lc - 1.0 - tin) * LOG_GAMMA[h])

    def seq_body(j, carry):
        r0 = pl.multiple_of(j * lc, lc)
        for h in range(H_C):
            qj = qr_scr[h, pl.ds(r0, lc), :].astype(BF16)
            kj = ke_scr[h, pl.ds(r0, lc), :].astype(BF16)
            vj = pc_ref[pl.ds(r0, lc), head_cols(2, h)].astype(BF16)
            s_old = s0_ref[j, h]
            qs_scr[h, pl.ds(r0, lc), :] = _dot(qj, s_old.astype(BF16))
            s_ref[j, h] = math.exp(lc * LOG_GAMMA[h]) * s_old + _dot_tn(kj, vj)
        return carry

    lax.fori_loop(0, nseq, seq_body, 0, unroll=SEQ_UNROLL)

    scores = []
    for h in range(H_C):
        kr = (rot(pc_ref[:, head_cols(1, h)]) * QK_SCALE).astype(BF16)
        scores.append(_dot_nt(qr_scr[h].astype(BF16), kr))
    for h in range(H_C):
        lg = LOG_GAMMA[h]
        decay = jnp.where(causal, jnp.exp(diff * lg), 0.0)
        v = pc_ref[:, head_cols(2, h)].astype(BF16)
        gate = pc_ref[:, head_cols(3, h)]
        o = _dot((scores[h] * decay).astype(BF16), v) + jnp.exp((tin + 1.0) * lg) * qs_scr[h]
        on = o * lax.rsqrt(jnp.mean(o * o, axis=1, keepdims=True) + EPS) * norm_ref[:, h * DH:(h + 1) * DH]
        out_ref[:, h * DH:(h + 1) * DH] = (on * _silu(gate)).astype(out_ref.dtype)


def _ret_prompt(pt, pk, cos, sin, normb, bsz, nchunk, layer):
    seq = nchunk * ROWS
    chunk = lambda b, c: (c, 0)
    chunk_t = lambda b, c: (0, c)
    idx = jnp.arange(ROWS, dtype=F32)
    diff = idx[None, :] - idx[:, None]
    log_gamma = jnp.asarray(LOG_GAMMA, F32)[:, None, None]
    dec = jnp.where(diff >= 0, jnp.exp(diff * log_gamma), 0.0)
    out, s1 = pl.pallas_call(
        functools.partial(_ret_prompt_body, nb=SEQ_PER_STEP),
        grid=(bsz // SEQ_PER_STEP, nchunk),
        in_specs=[_seq_t_spec(3 * GROUP_W), _seq_col_spec(GROUP_W, 1),
                  pl.BlockSpec((DH, ROWS), chunk_t), pl.BlockSpec((DH, ROWS), chunk_t),
                  pl.BlockSpec((ROWS, DH), chunk), pl.BlockSpec((ROWS, DH), chunk),
                  pl.BlockSpec((H_C, ROWS, ROWS), lambda b, c: (0, 0, 0)),
                  _layer_weight_spec((GROUP_W, LANES), layer)],
        out_specs=[_seq_spec(GROUP_W), _seq_state_spec((H_C, DH, DH))],
        out_shape=[jax.ShapeDtypeStruct((bsz, seq, GROUP_W), BF16),
                   jax.ShapeDtypeStruct((bsz, H_C, DH, DH), F32)],
        scratch_shapes=[pltpu.VMEM((SEQ_PER_STEP, H_C, DH, DH), F32)],
        compiler_params=_mixer_params(2),
        name="ret_prompt",
    )(pt.reshape(bsz, nchunk, 3 * GROUP_W, ROWS), pk.reshape(bsz, seq, -1), cos.T, sin.T, cos, sin, dec, normb)
    return out.reshape(bsz * seq, GROUP_W), s1


def _ret_sample(pc, cos, sin, norm, s_all, lc, layer, s_prev):
    t = pc.shape[0]
    nseq = ROWS // lc
    row = lambda i: (i, 0)
    const = lambda i: (0, 0)
    return pl.pallas_call(
        functools.partial(_ret_sample_body, lc=lc, nseq=nseq),
        grid=(t // ROWS,),
        in_specs=[pl.BlockSpec((ROWS, 4 * GROUP_W), row), pl.BlockSpec((ROWS, DH), const),
                  pl.BlockSpec((ROWS, DH), const), _layer_weight_spec((1, GROUP_W), layer),
                  _layer_state_spec((nseq, H_C, DH, DH), layer), _STACK_SPEC],
        out_specs=[pl.BlockSpec((ROWS, GROUP_W), row),
                   _layer_state_spec((nseq, H_C, DH, DH), layer)],
        out_shape=[jax.ShapeDtypeStruct((t, GROUP_W), BF16),
                   jax.ShapeDtypeStruct(s_all.shape, F32)],
        input_output_aliases={5: 1},
        scratch_shapes=[pltpu.VMEM((H_C, ROWS, DH), F32)] * 3,
        compiler_params=_mixer_params(1),
        name="ret_sample",
    )(pc, cos, sin, norm, s_all, s_prev)


HEADS_PER_GROUP = H_B // G_B
PAIR_W = 2 * P_B
GROUP_CH = HEADS_PER_GROUP * P_B


def _ssd_prompt_body(pt_ref, gt_ref, gb_ref, alog_ref, dskipb_ref, cwb_ref, cbb_ref, normb_ref,
                     out_ref, h_ref, tail_ref, xc_scr, *, nb):
    @pl.when(pl.program_id(1) == 0)
    def _init():
        h_ref[...] = jnp.zeros_like(h_ref)
        tail_ref[...] = jnp.zeros_like(tail_ref)

    causal_t, _, _ = _source_target_mask()
    lane = lax.broadcasted_iota(jnp.int32, (DH, ROWS), 1)
    rowid = lax.broadcasted_iota(jnp.int32, (GATE_ROWS, LANES), 0)
    drows = rowid >= GATE_DT
    pad = jnp.zeros((ROWS - GATE_ROWS, LANES), F32)

    dpre = (gt_ref[...] + gb_ref[...][None]).reshape(nb * GATE_ROWS, LANES)
    dt_all = jnp.maximum(dpre, 0.0) + _log1p_exp_neg_abs(dpre)
    a_all = jnp.broadcast_to(-jnp.exp(alog_ref[...])[None], (nb, GATE_ROWS, LANES)).reshape(nb * GATE_ROWS, LANES)
    acum_all = _cumsum_lanes(dt_all * a_all)
    gates = []
    for sq in range(nb):
        rows16 = slice(sq * GATE_ROWS, (sq + 1) * GATE_ROWS)
        dt = dt_all[rows16]
        acum = jnp.where(drows, acum_all[rows16], 0.0)
        alast = _last_lane(acum)
        gates.append(dict(dt=dt, acum=acum, at=jnp.concatenate([acum, pad], axis=0).T, exp_a=jnp.exp(acum),
                          wx=jnp.exp(alast - acum), g_a=jnp.exp(alast)))

    for sq in range(nb):
        for blk in range(CONV_DIM // DH):
            ch = slice(blk * DH, (blk + 1) * DH)
            new = pt_ref[sq, GROUP_W + blk * DH:GROUP_W + (blk + 1) * DH, :]
            prev = tail_ref[sq, ch, :]
            acc = new * cwb_ref[CONV_W - 1, ch, :]
            for k in range(1, CONV_W):
                shifted = pltpu.roll(jnp.where(lane >= ROWS - k, prev, new), k, axis=1)
                acc = acc + shifted * cwb_ref[CONV_W - 1 - k, ch, :]
            xc_scr[sq, ch, :] = _silu(acc + cbb_ref[ch, :])
            tail_ref[sq, ch, :] = new

    def x_rows(hd):
        return slice(hd * P_B, (hd + 1) * P_B)

    def b_rows(g):
        return slice(GROUP_W + g * N_B, GROUP_W + (g + 1) * N_B)

    def c_rows(g):
        return slice(GROUP_W + G_B * N_B + g * N_B, GROUP_W + G_B * N_B + (g + 1) * N_B)

    groups = [(sq, g) for sq in range(nb) for g in range(G_B)]
    cbs, chs, upds, xdts = {}, {}, {}, {}
    for sq, g in groups:
        gs = gates[sq]
        bg = xc_scr[sq, b_rows(g), :].T.astype(BF16)
        ct = xc_scr[sq, c_rows(g), :].astype(BF16)
        xws = []
        for r in range(HEADS_PER_GROUP):
            hd = g * HEADS_PER_GROUP + r
            c = GATE_DT + hd
            xdt = xc_scr[sq, x_rows(hd), :] * gs["dt"][c:c + 1, :]
            xdts[sq, hd] = xdt.astype(BF16)
            xws.append((xdt * gs["wx"][c:c + 1, :]).astype(BF16))
        h_old = jnp.concatenate([h_ref[sq, g * HEADS_PER_GROUP + r] for r in range(HEADS_PER_GROUP)], axis=0)
        cbs[sq, g] = _dot(bg, ct)
        chs[sq, g] = _dot(h_old.astype(BF16), ct)
        upds[sq, g] = _dot(jnp.concatenate(xws, axis=0), bg)

    ys = {}
    for sq, g in groups:
        gs = gates[sq]
        for r in range(HEADS_PER_GROUP):
            hd = g * HEADS_PER_GROUP + r
            c = GATE_DT + hd
            dec = jnp.where(causal_t, jnp.exp(gs["acum"][c:c + 1, :] - gs["at"][:, c:c + 1]), 0.0)
            ys[sq, hd] = _dot(xdts[sq, hd], (cbs[sq, g] * dec).astype(BF16))
            h_ref[sq, hd] = gs["g_a"][c:c + 1, 0:1] * h_ref[sq, hd] + upds[sq, g][r * P_B:(r + 1) * P_B, :]

    for sq, g in groups:
        gs = gates[sq]
        parts = []
        for r in range(HEADS_PER_GROUP):
            hd = g * HEADS_PER_GROUP + r
            c = GATE_DT + hd
            y = ys[sq, hd] + gs["exp_a"][c:c + 1, :] * chs[sq, g][r * P_B:(r + 1) * P_B, :]
            parts.append(y + dskipb_ref[x_rows(hd), :] * xc_scr[sq, x_rows(hd), :])
        grows = slice(g * GROUP_CH, (g + 1) * GROUP_CH)
        yz = jnp.concatenate(parts, axis=0) * _silu(pt_ref[sq, grows, :])
        yn = yz * lax.rsqrt(jnp.mean(yz * yz, axis=0, keepdims=True) + EPS) * normb_ref[grows, :]
        for j in range(GROUP_CH // DH):
            cols = slice(g * GROUP_CH + j * DH, g * GROUP_CH + (j + 1) * DH)
            out_ref[sq, :, cols] = yn[j * DH:(j + 1) * DH, :].T.astype(out_ref.dtype)


def _ssd_sample_body(*refs, lc, nseq):
    _ssd_block(*refs[:10], *refs[11:], lc=lc, nseq=nseq)


def _ssd_block(pb_ref, pg_ref, gb_ref, alog_ref, dskip_ref, cw_ref, cb_ref, norm_ref, h0_ref, hist_ref,
               out_ref, h_ref, buf_ref, ch_scr, xc_scr, xw_scr, g_scr, *, lc, nseq):
    rows = lc * nseq
    causal, _, _ = _causal_mask(rows, lc)
    lane = lax.broadcasted_iota(jnp.int32, (rows, LANES), 1)
    dcols = (lane >= GATE_DT) & (lane < GATE_DT + H_B)
    low = lax.broadcasted_iota(jnp.int32, (rows, PAIR_W), 1) < P_B

    new = pb_ref[:, GROUP_W:GROUP_W + CONV_DIM]
    tin = lax.broadcasted_iota(jnp.int32, (rows, 1), 0) & (lc - 1)
    acc = new * cw_ref[CONV_W - 1:CONV_W, :]
    for k in range(1, CONV_W):
        rolled = pltpu.roll(new, k, axis=0)
        hist = pltpu.roll(hist_ref[...], (rows + k - lc) % rows, axis=0)
        acc = acc + jnp.where(tin >= k, rolled, hist) * cw_ref[CONV_W - 1 - k:CONV_W - k, :]
    xc = _silu(acc + cb_ref[...])
    buf_ref[...] = new.reshape(nseq, lc, CONV_DIM)[:, lc - (CONV_W - 1):lc, :]

    dpre = pg_ref[...] + gb_ref[...]
    dt = jnp.maximum(dpre, 0.0) + _log1p_exp_neg_abs(dpre)
    adt = dt * (-jnp.exp(alog_ref[...]))
    acum = jnp.where(dcols, _dot_exact(jnp.where(causal, 1.0, 0.0), adt), 0.0)
    at = acum.T
    alast = _group_last(acum, lc)
    exp_a = jnp.exp(acum)
    wx = jnp.exp(alast - acum)
    g_a = jnp.exp(alast)

    def pair_bcast(slab, c0):
        return jnp.where(low, slab[:, c0:c0 + 1], slab[:, c0 + 1:c0 + 2])

    def pair_cols(g, p):
        start = g * GROUP_CH + p * PAIR_W
        return slice(start, start + PAIR_W)

    def b_cols(g):
        return slice(GROUP_W + g * N_B, GROUP_W + (g + 1) * N_B)

    def c_cols(g):
        return slice(GROUP_W + G_B * N_B + g * N_B, GROUP_W + G_B * N_B + (g + 1) * N_B)

    xw_pairs = {}
    xdt_pairs = {}
    for g in range(G_B):
        for p in range(HEADS_PER_GROUP // 2):
            c0 = GATE_DT + g * HEADS_PER_GROUP + 2 * p
            xdt = xc[:, pair_cols(g, p)] * pair_bcast(dt, c0)
            xdt_pairs[g, p] = xdt.astype(BF16)
            xw_pairs[g, p] = xdt * pair_bcast(wx, c0)
    xc_scr[...] = xc
    g_scr[...] = g_a
    for g in range(G_B):
        for p in range(HEADS_PER_GROUP // 2):
            xw_scr[:, pair_cols(g, p)] = xw_pairs[g, p]

    def seq_body(j, carry):
        r0 = pl.multiple_of(j * lc, lc)
        grow = g_scr[pl.ds(r0, 1), :]
        for g in range(G_B):
            bj = xc_scr[pl.ds(r0, lc), b_cols(g)].astype(BF16)
            cj = xc_scr[pl.ds(r0, lc), c_cols(g)].astype(BF16)
            xwj = xw_scr[pl.ds(r0, lc), g * GROUP_CH:(g + 1) * GROUP_CH].astype(BF16)
            hs = [h0_ref[j, g * HEADS_PER_GROUP + r] for r in range(HEADS_PER_GROUP)]
            h_old = jnp.concatenate(hs, axis=0)
            ch_scr[g, pl.ds(r0, lc), :] = _dot_nt(cj, h_old.astype(BF16))
            upd = _dot_tn(xwj, bj)
            for r in range(HEADS_PER_GROUP):
                hd = g * HEADS_PER_GROUP + r
                c = GATE_DT + hd
                h_ref[j, hd] = grow[:, c:c + 1] * hs[r] + upd[r * P_B:(r + 1) * P_B, :]
        return carry

    lax.fori_loop(0, nseq, seq_body, 0, unroll=SEQ_UNROLL)

    cbs = [_dot_nt(xc[:, c_cols(g)].astype(BF16), xc[:, b_cols(g)].astype(BF16)) for g in range(G_B)]
    for g in range(G_B):
        cb = cbs[g]
        ys = []
        for p in range(HEADS_PER_GROUP // 2):
            c0 = GATE_DT + g * HEADS_PER_GROUP + 2 * p
            xpair = xc[:, pair_cols(g, p)]
            xdt = xdt_pairs[g, p]
            halves = []
            for c in (c0, c0 + 1):
                dec = jnp.where(causal, jnp.exp(acum[:, c:c + 1] - at[c:c + 1, :]), 0.0)
                halves.append(_dot((cb * dec).astype(BF16), xdt))
            y = (jnp.where(low, halves[0], halves[1])
                 + pair_bcast(exp_a, c0) * ch_scr[g, :, p * PAIR_W:(p + 1) * PAIR_W])
            ys.append(y + dskip_ref[:, pair_cols(g, p)] * xpair)
        yg = jnp.concatenate(ys, axis=1)
        gcols = slice(g * GROUP_CH, (g + 1) * GROUP_CH)
        yz = yg * _silu(pb_ref[:, gcols])
        yn = yz * lax.rsqrt(jnp.mean(yz * yz, axis=1, keepdims=True) + EPS) * norm_ref[:, gcols]
        out_ref[:, gcols] = yn.astype(out_ref.dtype)


def _ssd_prompt(pt, gt, gb16, alog16, dskipb, cwb, cbb, normb, bsz, nchunk, layer):
    seq = nchunk * ROWS
    out, h1, tail = pl.pallas_call(
        functools.partial(_ssd_prompt_body, nb=SEQ_PER_STEP),
        grid=(bsz // SEQ_PER_STEP, nchunk),
        in_specs=[_seq_t_spec(GROUP_W + CONV_DIM), _seq_t_spec(GATE_ROWS),
                  _layer_weight_spec((GATE_ROWS, LANES), layer), _layer_weight_spec((GATE_ROWS, LANES), layer),
                  _layer_weight_spec((GROUP_W, LANES), layer), _layer_weight_spec((CONV_W, CONV_DIM, LANES), layer),
                  _layer_weight_spec((CONV_DIM, LANES), layer), _layer_weight_spec((GROUP_W, LANES), layer)],
        out_specs=[_seq_spec(GROUP_W), _seq_state_spec((H_B, P_B, N_B)), _seq_state_spec((CONV_DIM, ROWS))],
        out_shape=[jax.ShapeDtypeStruct((bsz, seq, GROUP_W), BF16),
                   jax.ShapeDtypeStruct((bsz, H_B, P_B, N_B), F32),
                   jax.ShapeDtypeStruct((bsz, CONV_DIM, ROWS), F32)],
        scratch_shapes=[pltpu.VMEM((SEQ_PER_STEP, CONV_DIM, ROWS), F32)],
        compiler_params=_mixer_params(2),
        name="ssd_prompt",
    )(pt.reshape(bsz, nchunk, GROUP_W + CONV_DIM, ROWS), gt.reshape(bsz, nchunk, GATE_ROWS, ROWS),
      gb16, alog16, dskipb, cwb, cbb, normb)
    return out.reshape(bsz * seq, GROUP_W), h1, tail


def _ssd_sample(pb, pg, gbias, alog, dskip, cw, cb, norm, h_all, hist, lc, layer, h_prev):
    t = pb.shape[0]
    nseq = ROWS // lc
    bsz = h_all.shape[1]
    row = lambda i: (i, 0)
    return pl.pallas_call(
        functools.partial(_ssd_sample_body, lc=lc, nseq=nseq),
        grid=(t // ROWS,),
        in_specs=[pl.BlockSpec((ROWS, GROUP_W + CONV_DIM), row), pl.BlockSpec((ROWS, LANES), row),
                  _layer_weight_spec((1, LANES), layer), _layer_weight_spec((1, LANES), layer),
                  _layer_weight_spec((1, GROUP_W), layer), _layer_weight_spec((CONV_W, CONV_DIM), layer),
                  _layer_weight_spec((1, CONV_DIM), layer), _layer_weight_spec((1, GROUP_W), layer),
                  _layer_state_spec((nseq, H_B, P_B, N_B), layer),
                  _layer_state_spec((ROWS, CONV_DIM), layer), _STACK_SPEC],
        out_specs=[pl.BlockSpec((ROWS, GROUP_W), row),
                   _layer_state_spec((nseq, H_B, P_B, N_B), layer),
                   pl.BlockSpec((nseq, CONV_W - 1, CONV_DIM), lambda i: (i, 0, 0))],
        out_shape=[jax.ShapeDtypeStruct((t, GROUP_W), BF16),
                   jax.ShapeDtypeStruct(h_all.shape, F32),
                   jax.ShapeDtypeStruct((bsz, CONV_W - 1, CONV_DIM), F32)],
        input_output_aliases={10: 1},
        scratch_shapes=[pltpu.VMEM((G_B, ROWS, GROUP_CH), F32), pltpu.VMEM((ROWS, CONV_DIM), F32),
                        pltpu.VMEM((ROWS, GROUP_W), F32), pltpu.VMEM((ROWS, LANES), F32)],
        compiler_params=_mixer_params(1),
        name="ssd_sample",
    )(pb, pg, gbias, alog, dskip, cw, cb, norm, h_all, hist, h_prev)


def _rope_tables(pos):
    half = DH // 2
    freqs = ROPE_BASE ** (-jnp.arange(half, dtype=F32) / half)
    ang = pos.astype(F32)[:, None] * freqs
    cos, sin = jnp.cos(ang), jnp.sin(ang)
    return jnp.concatenate([cos, cos], axis=-1), jnp.concatenate([-sin, sin], axis=-1)


def kernel(x_prompt, x_sample, state_mlstm_C, state_mlstm_n, state_mlstm_m, state_ssd, state_conv, state_ret,
           ffn1_norm, ffn1_w1, ffn1_w3, ffn1_w2, mix_norm, w_in, b_igate, b_fgate, mlstm_norm,
           conv_w, conv_b, dt_bias, a_log, d_skip, ssd_norm, ret_norm, w_out,
           ffn2_norm, ffn2_w1, ffn2_w3, ffn2_w2, final_norm):
    depth = w_in.shape[0]
    bsz, seq, _ = x_prompt.shape
    dbsz, dseq, _ = x_sample.shape
    assert seq % CHUNK == 0 and ROWS % dseq == 0 and (dbsz * dseq) % ROWS == 0
    assert dseq >= SUBLANES and dseq & (dseq - 1) == 0
    nchunk = seq // CHUNK
    seq_per_block = ROWS // dseq

    xp = x_prompt.reshape(bsz * seq, D_MODEL)
    xs = x_sample.reshape(dbsz * dseq, D_MODEL)

    cos_p, sin_p = _rope_tables(jnp.arange(seq))
    cos_s, sin_s = _rope_tables(PAST_LEN + jnp.arange(dseq))
    cos_s, sin_s = jnp.tile(cos_s, (seq_per_block, 1)), jnp.tile(sin_s, (seq_per_block, 1))

    a0 = 0
    a_gate = a0 + 4 * GROUP_W
    b0 = a_gate + 2 * H_A
    b_dt = b0 + GROUP_W + CONV_DIM
    c0 = b_dt + H_B

    outs_p = [[] for _ in range(6)]
    outs_s = [[] for _ in range(3)]
    c_stack = jnp.zeros(state_mlstm_C.shape, F32)
    h_stack = jnp.zeros(state_ssd.shape, F32)
    s_stack = jnp.zeros(state_ret.shape, F32)
    w1a, w3a, w2a = ffn1_w1.astype(BF16), ffn1_w3.astype(BF16), ffn1_w2.astype(BF16)
    w1b, w3b, w2b = ffn2_w1.astype(BF16), ffn2_w3.astype(BF16), ffn2_w2.astype(BF16)
    wo = w_out.astype(BF16)
    fin = final_norm[None, :]

    row3 = lambda v: v.astype(F32)[:, None, :]
    lanes_bcast = lambda v: jnp.broadcast_to(v.astype(F32)[..., None], v.shape + (LANES,))
    transposed = lambda w: jnp.swapaxes(w, 1, 2)
    w_in_b = w_in.astype(BF16)
    cols = lambda lo, hi: w_in_b[:, :, lo:hi]
    qa, ka, va, ga = (cols(a0 + i * GROUP_W, a0 + (i + 1) * GROUP_W) for i in range(4))
    qc, kc, vc, gc = (cols(c0 + i * GROUP_W, c0 + (i + 1) * GROUP_W) for i in range(4))
    w_gates = jnp.concatenate([cols(a_gate, b0), cols(b_dt, c0)], axis=2)
    n_gates = w_gates.shape[2]
    zeros_dt = jnp.zeros((depth, GATE_DT), F32)
    g1, gm, g2 = row3(ffn1_norm), row3(mix_norm), row3(ffn2_norm)
    wa, wb, wc = cols(a0, a_gate), cols(b0, b_dt), cols(c0, c0 + 4 * GROUP_W)
    wg = jnp.pad(w_gates, ((0, 0), (0, 0), (0, LANES - n_gates)))
    gate_row = lambda parts: row3(jnp.pad(jnp.concatenate(parts, axis=1), ((0, 0), (0, LANES - n_gates))))
    gbias = gate_row([b_igate, b_fgate, dt_bias])
    alog = gate_row([zeros_dt, a_log])
    dskip_ch = jnp.repeat(d_skip.astype(F32), P_B, axis=1)
    dskip, na, nb, nc = row3(dskip_ch), row3(mlstm_norm), row3(ssd_norm), row3(ret_norm)
    cw, cb = conv_w.astype(F32), row3(conv_b)
    m0rows = jnp.pad(jnp.repeat(state_mlstm_m.astype(F32), dseq, axis=1),
                     ((0, 0), (0, 0), (GATE_F, LANES - GATE_F - H_A)))
    hist = jnp.pad(state_conv.astype(F32), ((0, 0), (0, 0), (dseq - (CONV_W - 1), 0), (0, 0)))
    hist = hist.reshape(depth, dbsz * dseq, CONV_DIM)
    wta = transposed(jnp.concatenate([qa, va, ga], axis=2))
    wtb = transposed(cols(b0, b_dt))
    wtc = transposed(jnp.concatenate([qc, vc, gc], axis=2))
    wtg = transposed(w_gates)
    wk = jnp.concatenate([ka, kc], axis=2)
    gb8 = lanes_bcast(jnp.concatenate([b_igate, b_fgate], axis=1))
    gb16 = lanes_bcast(jnp.concatenate([zeros_dt, dt_bias], axis=1))
    alog16 = lanes_bcast(jnp.concatenate([zeros_dt, a_log], axis=1))
    nab, nbb, ncb = lanes_bcast(mlstm_norm), lanes_bcast(ssd_norm), lanes_bcast(ret_norm)
    dskipb, cwb, cbb = lanes_bcast(dskip_ch), lanes_bcast(conv_w), lanes_bcast(conv_b)

    for l in range(depth):
        final = l == depth - 1

        xp = _ffn_call(xp, g1, w1a, w3a, w2a, l)
        pta, ptb, ptc, gt, pk = _inproj_prompt_call(xp, gm, wta, wtb, wtc, wtg, wk, l)
        oa, c1, n1, m1 = _mlstm_prompt(pta, pk, gt, gb8, nab, bsz, nchunk, l)
        ob, h1, tail = _ssd_prompt(ptb, gt, gb16, alog16, dskipb, cwb, cbb, nbb, bsz, nchunk, l)
        oc, s1 = _ret_prompt(ptc, pk, cos_p, sin_p, ncb, bsz, nchunk, l)
        xp = _mix_ffn_call(xp, oa, ob, oc, wo, g2, w1b, w3b, w2b, fin, final, l)
        buf1 = jnp.swapaxes(tail[:, :, ROWS - (CONV_W - 1):], 1, 2)
        for acc, v in zip(outs_p, (c1, n1, m1[:, GATE_F:GATE_F + H_A, 0], h1, buf1, s1)):
            acc.append(v)

        xs = _ffn_call(xs, g1, w1a, w3a, w2a, l)
        pa, pb, pc, pg = _inproj_call(xs, gm, wa, wb, wc, wg, l)
        oa, c_stack, n1, m1 = _mlstm_sample(pa, pg, gbias, na, state_mlstm_C, state_mlstm_n, m0rows, dseq,
                                            l, c_stack)
        ob, h_stack, buf1 = _ssd_sample(pb, pg, gbias, alog, dskip, cw, cb, nb, state_ssd, hist, dseq, l, h_stack)
        oc, s_stack = _ret_sample(pc, cos_s, sin_s, nc, state_ret, dseq, l, s_stack)
        xs = _mix_ffn_call(xs, oa, ob, oc, wo, g2, w1b, w3b, w2b, fin, final, l)
        for acc, v in zip(outs_s, (n1, m1[:, GATE_F:GATE_F + H_A], buf1)):
            acc.append(v)

    y_prompt = xp.reshape(bsz, seq, D_MODEL)
    y_sample = xs.reshape(dbsz, dseq, D_MODEL)
    s_n, s_m, s_buf = [jnp.stack(a) for a in outs_s]
    return (y_prompt, y_sample, *[jnp.stack(a) for a in outs_p], c_stack, s_n, s_m, h_stack, s_buf, s_stack)
```

```python
import functools
import math

import jax
import jax.numpy as jnp
from jax import lax
from jax.experimental import pallas as pl
from jax.experimental.pallas import tpu as pltpu

F32 = jnp.float32
BF16 = jnp.bfloat16
HIGHEST = lax.Precision.HIGHEST

D_MODEL = 1024
D_FF = 2816
GROUP_W = 512
H_A = 4
DH = 128
H_B = 8
P_B = 64
N_B = 128
G_B = 2
CONV_W = 4
CONV_DIM = GROUP_W + 2 * G_B * N_B
H_C = 4
CHUNK = 128
PAST_LEN = 16384
GATE_SOFTCAP = 15.0
ROPE_BASE = 10000.0
EPS = 1e-6
QK_SCALE = DH ** -0.5

LANES = 128
SUBLANES = 8

ROWS = 128
SEQ_PER_STEP = 8
TM_FFN = 1024
SEQ_UNROLL = 8
TM = 512
FC = 256
VMEM_LIMIT = 56 * 1024 * 1024

GATE_I = 0
GATE_F = 4
GATE_DT = 8
GATE_ROWS = 16


def _dot(a, b):
    return jnp.dot(a, b, preferred_element_type=F32)


def _dot_nt(a, b):
    return lax.dot_general(a, b, (((1,), (1,)), ((), ())), preferred_element_type=F32)


def _dot_tn(a, b):
    return lax.dot_general(a, b, (((0,), (0,)), ((), ())), preferred_element_type=F32)


def _dot_exact(a, b):
    return jnp.dot(a, b, precision=HIGHEST, preferred_element_type=F32)


def _rms(x, g):
    return x * lax.rsqrt(jnp.mean(x * x, axis=-1, keepdims=True) + EPS) * g


def _silu(x):
    return x * jax.nn.sigmoid(x)


def _log1p_exp_neg_abs(x):
    return jnp.log1p(jnp.exp(-jnp.abs(x)))


def _causal_mask(rows, lc):
    r = lax.broadcasted_iota(jnp.int32, (rows, rows), 0)
    c = lax.broadcasted_iota(jnp.int32, (rows, rows), 1)
    m = c <= r
    if lc != rows:
        shift = lc.bit_length() - 1
        m = m & ((r >> shift) == (c >> shift))
    return m, r, c


def _group_last(x, lc):
    rows, w = x.shape
    x3 = x.reshape(rows // lc, lc, w)
    return jnp.broadcast_to(x3[:, lc - 1:lc, :], x3.shape).reshape(rows, w)


def _group_first(x, lc):
    rows, w = x.shape
    return x.reshape(rows // lc, lc, w)[:, 0, :]


def _group_sum(x, lc):
    rows, w = x.shape
    return jnp.sum(x.reshape(rows // lc, lc, w), axis=1)


def _group_bcast(x, lc, rows):
    n, w = x.shape
    return jnp.broadcast_to(x[:, None, :], (n, lc, w)).reshape(rows, w)


def _ffn_core(x, g_ref, w1_ref, w3_ref, w2_ref, act_ref):
    h = _rms(x, g_ref[...]).astype(BF16)
    for c in range(D_FF // FC):
        cols = slice(c * FC, (c + 1) * FC)
        a = _dot(h, w1_ref[:, cols])
        b = _dot(h, w3_ref[:, cols])
        act_ref[:, cols] = (_silu(a) * b).astype(BF16)
    return x + 0.5 * _dot(act_ref[...], w2_ref[...])


def _ffn_body(x_ref, g_ref, w1_ref, w3_ref, w2_ref, o_ref, act_ref):
    o_ref[...] = _ffn_core(x_ref[...], g_ref, w1_ref, w3_ref, w2_ref, act_ref)


def _mix_ffn_body(x_ref, oa_ref, ob_ref, oc_ref, wo_ref, g_ref, w1_ref, w3_ref, w2_ref, fin_ref,
                  o_ref, act_ref, *, final):
    x = x_ref[...]
    x = x + (_dot(oa_ref[...], wo_ref[0:GROUP_W, :])
             + _dot(ob_ref[...], wo_ref[GROUP_W:2 * GROUP_W, :])
             + _dot(oc_ref[...], wo_ref[2 * GROUP_W:3 * GROUP_W, :]))
    y = _ffn_core(x, g_ref, w1_ref, w3_ref, w2_ref, act_ref)
    if final:
        y = _rms(y, fin_ref[...])
    o_ref[...] = y


def _inproj_body(x_ref, g_ref, wa_ref, wb_ref, wc_ref, wg_ref, pa_ref, pb_ref, pc_ref, pg_ref):
    h = _rms(x_ref[...], g_ref[...]).astype(BF16)
    pa_ref[...] = _dot_nt(h, wa_ref[...])
    pb_ref[...] = _dot_nt(h, wb_ref[...])
    pc_ref[...] = _dot_nt(h, wc_ref[...])
    pg_ref[...] = _dot_nt(h, wg_ref[...])


def _inproj_prompt_body(x_ref, g_ref, wta_ref, wtb_ref, wtc_ref, wtg_ref, wk_ref,
                        pta_ref, ptb_ref, ptc_ref, gt_ref, pk_ref):
    h = _rms(x_ref[...], g_ref[...]).astype(BF16)
    for w_ref, o_ref in ((wta_ref, pta_ref), (wtb_ref, ptb_ref), (wtc_ref, ptc_ref), (wtg_ref, gt_ref)):
        pt = _dot_nt(w_ref[...], h)
        for j in range(TM // ROWS):
            o_ref[j] = pt[:, j * ROWS:(j + 1) * ROWS]
    pk_ref[...] = _dot_nt(h, wk_ref[...])


def _const_spec(shape):
    nd = len(shape)
    return pl.BlockSpec(shape, lambda *_: (0,) * nd, pipeline_mode=pl.Buffered(1))


def _layer_weight_spec(shape, layer):
    nd = len(shape)
    return pl.BlockSpec((None,) + shape, lambda *_: (layer,) + (0,) * nd, pipeline_mode=pl.Buffered(1))


def _row_spec(rows, cols):
    return pl.BlockSpec((rows, cols), lambda i: (i, 0))


def _dense_params():
    return pltpu.CompilerParams(dimension_semantics=("arbitrary",), vmem_limit_bytes=VMEM_LIMIT)


def _ffn_call(x, g, w1, w3, w2, layer):
    t = x.shape[0]
    return pl.pallas_call(
        _ffn_body,
        grid=(t // TM_FFN,),
        in_specs=[_row_spec(TM_FFN, D_MODEL), _layer_weight_spec((1, D_MODEL), layer),
                  _layer_weight_spec((D_MODEL, D_FF), layer), _layer_weight_spec((D_MODEL, D_FF), layer),
                  _layer_weight_spec((D_FF, D_MODEL), layer)],
        out_specs=_row_spec(TM_FFN, D_MODEL),
        out_shape=jax.ShapeDtypeStruct((t, D_MODEL), F32),
        scratch_shapes=[pltpu.VMEM((TM_FFN, D_FF), BF16)],
        compiler_params=_dense_params(),
        name="ffn",
    )(x, g, w1, w3, w2)


def _mix_ffn_call(x, oa, ob, oc, wo, g, w1, w3, w2, fin, final, layer):
    t = x.shape[0]
    return pl.pallas_call(
        functools.partial(_mix_ffn_body, final=final),
        grid=(t // TM_FFN,),
        in_specs=[_row_spec(TM_FFN, D_MODEL)] + [_row_spec(TM_FFN, GROUP_W)] * 3 + [
                  _layer_weight_spec((3 * GROUP_W, D_MODEL), layer), _layer_weight_spec((1, D_MODEL), layer),
                  _layer_weight_spec((D_MODEL, D_FF), layer), _layer_weight_spec((D_MODEL, D_FF), layer),
                  _layer_weight_spec((D_FF, D_MODEL), layer), _const_spec((1, D_MODEL))],
        out_specs=_row_spec(TM_FFN, D_MODEL),
        out_shape=jax.ShapeDtypeStruct((t, D_MODEL), F32),
        scratch_shapes=[pltpu.VMEM((TM_FFN, D_FF), BF16)],
        compiler_params=_dense_params(),
        name="mix_ffn",
    )(x, oa, ob, oc, wo, g, w1, w3, w2, fin)


def _inproj_call(x, g, wa, wb, wc, wg, layer):
    t = x.shape[0]
    widths = (wa.shape[1], wb.shape[1], wc.shape[1], wg.shape[1])
    return pl.pallas_call(
        _inproj_body,
        grid=(t // TM,),
        in_specs=([_row_spec(TM, D_MODEL), _layer_weight_spec((1, D_MODEL), layer)]
                  + [_layer_weight_spec((w, D_MODEL), layer) for w in widths]),
        out_specs=[_row_spec(TM, w) for w in widths],
        out_shape=[jax.ShapeDtypeStruct((t, w), F32) for w in widths],
        compiler_params=_dense_params(),
        name="inproj",
    )(x, g, wa, wb, wc, wg)


def _inproj_prompt_call(x, g, wta, wtb, wtc, wtg, wk, layer):
    t = x.shape[0]
    t_rows = (wta.shape[1], wtb.shape[1], wtc.shape[1], wtg.shape[1])
    n_cols = (wk.shape[1],)
    chunk_spec = lambda r: pl.BlockSpec((TM // ROWS, r, ROWS), lambda i: (i, 0, 0))
    return pl.pallas_call(
        _inproj_prompt_body,
        grid=(t // TM,),
        in_specs=([_row_spec(TM, D_MODEL), _layer_weight_spec((1, D_MODEL), layer)]
                  + [_layer_weight_spec((r, D_MODEL), layer) for r in t_rows + n_cols]),
        out_specs=[chunk_spec(r) for r in t_rows] + [_row_spec(TM, w) for w in n_cols],
        out_shape=([jax.ShapeDtypeStruct((t // ROWS, r, ROWS), F32) for r in t_rows]
                   + [jax.ShapeDtypeStruct((t, w), F32) for w in n_cols]),
        compiler_params=_dense_params(),
        name="inproj_prompt",
    )(x, g, wta, wtb, wtc, wtg, wk)


def _source_target_mask():
    src = lax.broadcasted_iota(jnp.int32, (ROWS, ROWS), 0)
    tgt = lax.broadcasted_iota(jnp.int32, (ROWS, ROWS), 1)
    return src <= tgt, src, tgt


def _last_lane(x):
    return jnp.broadcast_to(x[:, LANES - 1:LANES], x.shape)


def _cumsum_lanes(x):
    lane = lax.broadcasted_iota(jnp.int32, x.shape, 1)
    k = 1
    while k < x.shape[1]:
        x = x + jnp.where(lane >= k, pltpu.roll(x, k, axis=1), 0.0)
        k *= 2
    return x


def _mlstm_prompt_body(pt_ref, k_ref, gt_ref, gb_ref, normb_ref, out_ref, c_ref, n_ref, m_ref, ct_scr, *, nb):
    @pl.when(pl.program_id(1) == 0)
    def _init():
        ct_scr[...] = jnp.zeros_like(ct_scr)
        n_ref[...] = jnp.zeros_like(n_ref)
        m_ref[...] = jnp.zeros_like(m_ref)

    causal_t, _, _ = _source_target_mask()
    pre = (gt_ref[:, 0:SUBLANES, :] + gb_ref[...][None]).reshape(nb * SUBLANES, LANES)
    cap_all = GATE_SOFTCAP * jnp.tanh(pre / GATE_SOFTCAP)
    bcum_all = _cumsum_lanes(jnp.minimum(cap_all, 0.0) - _log1p_exp_neg_abs(cap_all))
    gates = []
    for sq in range(nb):
        rows8 = slice(sq * SUBLANES, (sq + 1) * SUBLANES)
        gates.append(_mlstm_prompt_gates(cap_all[rows8], bcum_all[rows8], m_ref.at[sq], causal_t))
    units = [(sq, h) for sq in range(nb) for h in range(H_A)]

    st_raw, cq, qn, c_upd, n_upd, p_all = {}, {}, {}, {}, {}, {}
    for sq, h in units:
        c = GATE_F + h
        gs = gates[sq]
        qt = (pt_ref[sq, h * DH:(h + 1) * DH, :] * QK_SCALE).astype(BF16)
        vt = pt_ref[sq, GROUP_W + h * DH:GROUP_W + (h + 1) * DH, :]
        kf = k_ref[sq, :, h * DH:(h + 1) * DH]
        kb = kf.astype(BF16)
        n8 = jnp.concatenate([n_ref[sq, h:h + 1, :], jnp.zeros((SUBLANES - 1, DH), F32)], axis=0)
        st_raw[sq, h] = _dot(kb, qt)
        cq[sq, h] = _dot(ct_scr[sq, h].astype(BF16), qt)
        qn[sq, h] = _dot(n8.astype(BF16), qt)[0:1, :]
        c_upd[sq, h] = _dot((vt * gs["wend"][c:c + 1, :]).astype(BF16), kb)
        n_upd[sq, h] = jnp.sum(kf * gs["wt"][:, c:c + 1], axis=0, keepdims=True)
        p_all[sq, h] = jnp.where(causal_t, jnp.exp(gs["ut"][:, c:c + 1] + gs["bm"][c:c + 1, :]), 0.0)

    num, den = {}, {}
    for sq, h in units:
        c = GATE_F + h
        gs = gates[sq]
        st = st_raw[sq, h] * p_all[sq, h]
        vt = pt_ref[sq, GROUP_W + h * DH:GROUP_W + (h + 1) * DH, :].astype(BF16)
        g = gs["gint"][c:c + 1, :]
        num[sq, h] = _dot(vt, st.astype(BF16)) + g * cq[sq, h]
        den[sq, h] = jnp.sum(st, axis=0, keepdims=True) + g * qn[sq, h]
        g1 = gs["gend"][c:c + 1, 0:1]
        ct_scr[sq, h] = g1 * ct_scr[sq, h] + c_upd[sq, h]
        n_ref[sq, h:h + 1, :] = g1 * n_ref[sq, h:h + 1, :] + n_upd[sq, h]

    for sq, h in units:
        c = GATE_F + h
        hs = slice(h * DH, (h + 1) * DH)
        ot = pt_ref[sq, 2 * GROUP_W + h * DH:2 * GROUP_W + (h + 1) * DH, :]
        hh = num[sq, h] * (1.0 / jnp.maximum(jnp.abs(den[sq, h]), gates[sq]["emt"][c:c + 1, :]))
        rs = lax.rsqrt(jnp.mean(hh * hh, axis=0, keepdims=True) + EPS)
        o = hh * rs * normb_ref[hs, :] * jax.nn.sigmoid(ot)
        out_ref[sq, :, hs] = o.T.astype(out_ref.dtype)

    @pl.when(pl.program_id(1) == pl.num_programs(1) - 1)
    def _finish():
        for sq in range(nb):
            for h in range(H_A):
                c_ref[sq, h] = ct_scr[sq, h].T


def _mlstm_prompt_gates(cap, bcum, m_ref, causal_t):
    rowid = lax.broadcasted_iota(jnp.int32, (SUBLANES, LANES), 0)
    frows = rowid >= GATE_F
    mprev = m_ref[...]
    inter = bcum + mprev
    u = pltpu.roll(cap, GATE_F - GATE_I, axis=0) - bcum
    ut = jnp.concatenate([u, jnp.zeros((ROWS - SUBLANES, LANES), F32)], axis=0).T
    mt_all = jnp.zeros((SUBLANES, LANES), F32)
    for h in range(H_A):
        c = GATE_F + h
        d = jnp.where(causal_t, ut[:, c:c + 1] + bcum[c:c + 1, :], -jnp.inf)
        mt = jnp.maximum(inter[c:c + 1, :], jnp.max(d, axis=0, keepdims=True))
        mt_all = jnp.where(rowid == c, mt, mt_all)
    blast = _last_lane(bcum)
    mend = _last_lane(mt_all)
    wend = jnp.where(frows, jnp.exp(blast + u - mend), 0.0)
    gend = jnp.where(frows, jnp.exp(blast + mprev - mend), 0.0)
    gint = jnp.where(frows, jnp.exp(inter - mt_all), 0.0)
    emt = jnp.exp(-mt_all)
    bm = bcum - mt_all
    m_ref[...] = jnp.where(frows, mend, 0.0)
    wt = jnp.concatenate([wend, jnp.zeros((ROWS - SUBLANES, LANES), F32)], axis=0).T
    return dict(ut=ut, bm=bm, wend=wend, wt=wt, gend=gend, gint=gint, emt=emt)


def _mlstm_sample_body(*refs, lc, nseq):
    _mlstm_block(*refs[:7], *refs[8:], lc=lc, nseq=nseq)


def _mlstm_block(pa_ref, pg_ref, gb_ref, norm_ref, c0_ref, n0_ref, m0_ref,
                 out_ref, c_ref, n_ref, m_ref, qc_scr, kw_scr, g_scr, *, lc, nseq):
    rows = lc * nseq
    mprev = m0_ref[...]
    causal, _, _ = _causal_mask(rows, lc)
    lane = lax.broadcasted_iota(jnp.int32, (rows, LANES), 1)
    fcols = (lane >= GATE_F) & (lane < GATE_F + H_A)

    pre = pg_ref[...] + gb_ref[...]
    cap = GATE_SOFTCAP * jnp.tanh(pre / GATE_SOFTCAP)
    logf = jnp.minimum(cap, 0.0) - _log1p_exp_neg_abs(cap)
    bcum = _dot_exact(jnp.where(causal, 1.0, 0.0), logf)
    inter = bcum + mprev
    u = pltpu.roll(cap, GATE_F - GATE_I, axis=1) - bcum
    ut = u.T
    mt_all = jnp.zeros((rows, LANES), F32)
    for h in range(H_A):
        c = GATE_F + h
        d = jnp.where(causal, bcum[:, c:c + 1] + ut[c:c + 1, :], -jnp.inf)
        mt = jnp.maximum(inter[:, c:c + 1], jnp.max(d, axis=1, keepdims=True))
        mt_all = jnp.where(lane == c, mt, mt_all)
    blast = _group_last(bcum, lc)
    mend = _group_last(mt_all, lc)
    wend = jnp.where(fcols, jnp.exp(blast + u - mend), 0.0)
    gend = jnp.where(fcols, jnp.exp(blast + mprev - mend), 0.0)
    gint = jnp.where(fcols, jnp.exp(inter - mt_all), 0.0)
    emt = jnp.exp(-mt_all)
    bm = bcum - mt_all

    def head_cols(group, h):
        return slice(group * GROUP_W + h * DH, group * GROUP_W + (h + 1) * DH)

    g_scr[...] = gend
    for h in range(H_A):
        c = GATE_F + h
        kw_scr[h] = pa_ref[:, head_cols(1, h)] * wend[:, c:c + 1]

    def seq_body(j, carry):
        r0 = pl.multiple_of(j * lc, lc)
        grow = g_scr[pl.ds(r0, 1), :]
        for h in range(H_A):
            c = GATE_F + h
            qj = (pa_ref[pl.ds(r0, lc), head_cols(0, h)] * QK_SCALE).astype(BF16)
            vj = pa_ref[pl.ds(r0, lc), head_cols(2, h)].astype(BF16)
            kwj = kw_scr[h, pl.ds(r0, lc), :].astype(BF16)
            c_old = c0_ref[j, h]
            qc_scr[h, pl.ds(r0, lc), :] = _dot(qj, c_old.astype(BF16))
            c_ref[j, h] = grow[:, c:c + 1] * c_old + _dot_tn(kwj, vj)
        return carry

    lax.fori_loop(0, nseq, seq_body, 0, unroll=SEQ_UNROLL)
    n_rows = []
    gfirst = _group_first(gend, lc)
    for h in range(H_A):
        c = GATE_F + h
        n_old = n0_ref[:, h, :]
        n_rows.append(_group_bcast(n_old, lc, rows))
        n_ref[:, h, :] = gfirst[:, c:c + 1] * n_old + _group_sum(kw_scr[h], lc)
    m_ref[...] = _group_first(mend, lc)

    scores = []
    for h in range(H_A):
        qb = (pa_ref[:, head_cols(0, h)] * QK_SCALE).astype(BF16)
        scores.append(_dot_nt(qb, pa_ref[:, head_cols(1, h)].astype(BF16)))
    for h in range(H_A):
        c = GATE_F + h
        qf = pa_ref[:, head_cols(0, h)] * QK_SCALE
        v = pa_ref[:, head_cols(2, h)].astype(BF16)
        og = pa_ref[:, head_cols(3, h)]
        p = jnp.where(causal, jnp.exp(bm[:, c:c + 1] + ut[c:c + 1, :]), 0.0)
        s = scores[h] * p
        g = gint[:, c:c + 1]
        num = _dot(s.astype(BF16), v) + g * qc_scr[h]
        qn = jnp.sum(qf * n_rows[h], axis=1, keepdims=True)
        den = jnp.sum(s, axis=1, keepdims=True) + g * qn
        hh = num / jnp.maximum(jnp.abs(den), emt[:, c:c + 1])
        hn = hh * lax.rsqrt(jnp.mean(hh * hh, axis=1, keepdims=True) + EPS) * norm_ref[:, h * DH:(h + 1) * DH]
        out_ref[:, h * DH:(h + 1) * DH] = (hn * jax.nn.sigmoid(og)).astype(out_ref.dtype)


def _mixer_params(ndims):
    return pltpu.CompilerParams(dimension_semantics=("arbitrary",) * ndims, vmem_limit_bytes=VMEM_LIMIT)


def _seq_spec(cols):
    return pl.BlockSpec((SEQ_PER_STEP, ROWS, cols), lambda b, c: (b, c, 0))


def _seq_state_spec(shape):
    nd = len(shape)
    return pl.BlockSpec((SEQ_PER_STEP,) + shape, lambda b, c: (b,) + (0,) * nd)


def _seq_t_spec(rows):
    return pl.BlockSpec((SEQ_PER_STEP, None, rows, ROWS), lambda b, c: (b, c, 0, 0))


def _seq_col_spec(cols, col_block):
    return pl.BlockSpec((SEQ_PER_STEP, ROWS, cols), lambda b, c: (b, c, col_block))


def _mlstm_prompt(pt, pk, gt, gb8, normb, bsz, nchunk, layer):
    seq = nchunk * ROWS
    out, c1, n1, m1 = pl.pallas_call(
        functools.partial(_mlstm_prompt_body, nb=SEQ_PER_STEP),
        grid=(bsz // SEQ_PER_STEP, nchunk),
        in_specs=[_seq_t_spec(3 * GROUP_W), _seq_col_spec(GROUP_W, 0), _seq_t_spec(GATE_ROWS),
                  _layer_weight_spec((SUBLANES, LANES), layer), _layer_weight_spec((GROUP_W, LANES), layer)],
        out_specs=[_seq_spec(GROUP_W), _seq_state_spec((H_A, DH, DH)), _seq_state_spec((H_A, DH)),
                   _seq_state_spec((SUBLANES, LANES))],
        out_shape=[jax.ShapeDtypeStruct((bsz, seq, GROUP_W), BF16),
                   jax.ShapeDtypeStruct((bsz, H_A, DH, DH), F32),
                   jax.ShapeDtypeStruct((bsz, H_A, DH), F32),
                   jax.ShapeDtypeStruct((bsz, SUBLANES, LANES), F32)],
        scratch_shapes=[pltpu.VMEM((SEQ_PER_STEP, H_A, DH, DH), F32)],
        compiler_params=_mixer_params(2),
        name="mlstm_prompt",
    )(pt.reshape(bsz, nchunk, 3 * GROUP_W, ROWS), pk.reshape(bsz, seq, -1),
      gt.reshape(bsz, nchunk, GATE_ROWS, ROWS), gb8, normb)
    return out.reshape(bsz * seq, GROUP_W), c1, n1, m1


def _layer_state_spec(shape, layer):
    nd = len(shape)
    return pl.BlockSpec((None,) + shape, lambda i: (layer, i) + (0,) * (nd - 1))


_STACK_SPEC = pl.BlockSpec(memory_space=pl.ANY)


def _mlstm_sample(pa, pg, gbias, norm, c_all, n_all, m0rows, lc, layer, c_prev):
    t = pa.shape[0]
    nseq = ROWS // lc
    bsz = c_all.shape[1]
    row = lambda i: (i, 0)
    return pl.pallas_call(
        functools.partial(_mlstm_sample_body, lc=lc, nseq=nseq),
        grid=(t // ROWS,),
        in_specs=[pl.BlockSpec((ROWS, 4 * GROUP_W), row), pl.BlockSpec((ROWS, LANES), row),
                  _layer_weight_spec((1, LANES), layer), _layer_weight_spec((1, GROUP_W), layer),
                  _layer_state_spec((nseq, H_A, DH, DH), layer),
                  _layer_state_spec((nseq, H_A, DH), layer),
                  _layer_state_spec((ROWS, LANES), layer), _STACK_SPEC],
        out_specs=[pl.BlockSpec((ROWS, GROUP_W), row),
                   _layer_state_spec((nseq, H_A, DH, DH), layer),
                   pl.BlockSpec((nseq, H_A, DH), lambda i: (i, 0, 0)),
                   pl.BlockSpec((nseq, LANES), row)],
        out_shape=[jax.ShapeDtypeStruct((t, GROUP_W), BF16),
                   jax.ShapeDtypeStruct(c_all.shape, F32),
                   jax.ShapeDtypeStruct((bsz, H_A, DH), F32),
                   jax.ShapeDtypeStruct((bsz, LANES), F32)],
        input_output_aliases={7: 1},
        scratch_shapes=[pltpu.VMEM((H_A, ROWS, DH), F32), pltpu.VMEM((H_A, ROWS, DH), F32),
                        pltpu.VMEM((ROWS, LANES), F32)],
        compiler_params=_mixer_params(1),
        name="mlstm_sample",
    )(pa, pg, gbias, norm, c_all, n_all, m0rows, c_prev)


LOG_GAMMA = [math.log(1.0 - 2.0 ** (-5.0 - h)) for h in range(H_C)]


def _ret_prompt_body(pt_ref, k_ref, cost_ref, sint_ref, cos_ref, sin_ref, dec_ref, normb_ref, out_ref, s_ref,
                     st_scr, *, nb):
    @pl.when(pl.program_id(1) == 0)
    def _init():
        st_scr[...] = jnp.zeros_like(st_scr)

    tin = lax.broadcasted_iota(jnp.int32, (1, ROWS), 1).astype(F32)
    cost, sint = cost_ref[...], sint_ref[...]
    cos, sin = cos_ref[...], sin_ref[...]
    units = [(sq, h) for sq in range(nb) for h in range(H_C)]

    st_raw, qs, s_upd = {}, {}, {}
    for sq, h in units:
        lg = LOG_GAMMA[h]
        qt = pt_ref[sq, h * DH:(h + 1) * DH, :]
        vt = pt_ref[sq, GROUP_W + h * DH:GROUP_W + (h + 1) * DH, :]
        kf = k_ref[sq, :, h * DH:(h + 1) * DH]
        qr = (qt * cost + pltpu.roll(qt, DH // 2, axis=0) * sint).astype(BF16)
        kr = ((kf * cos + pltpu.roll(kf, DH // 2, axis=1) * sin) * QK_SCALE).astype(BF16)
        ve = (vt * jnp.exp((ROWS - 1.0 - tin) * lg)).astype(BF16)
        st_raw[sq, h] = _dot(kr, qr)
        qs[sq, h] = _dot(st_scr[sq, h].astype(BF16), qr)
        s_upd[sq, h] = _dot(ve, kr)

    o_all = {}
    for sq, h in units:
        lg = LOG_GAMMA[h]
        vt = pt_ref[sq, GROUP_W + h * DH:GROUP_W + (h + 1) * DH, :].astype(BF16)
        st = (st_raw[sq, h] * dec_ref[h]).astype(BF16)
        o_all[sq, h] = _dot(vt, st) + jnp.exp((tin + 1.0) * lg) * qs[sq, h]
        st_scr[sq, h] = math.exp(ROWS * lg) * st_scr[sq, h] + s_upd[sq, h]

    for sq, h in units:
        hs = slice(h * DH, (h + 1) * DH)
        gt = pt_ref[sq, 2 * GROUP_W + h * DH:2 * GROUP_W + (h + 1) * DH, :]
        o = o_all[sq, h]
        rs = lax.rsqrt(jnp.mean(o * o, axis=0, keepdims=True) + EPS)
        on = o * rs * normb_ref[hs, :] * _silu(gt)
        out_ref[sq, :, hs] = on.T.astype(out_ref.dtype)

    @pl.when(pl.program_id(1) == pl.num_programs(1) - 1)
    def _finish():
        for sq in range(nb):
            for h in range(H_C):
                s_ref[sq, h] = st_scr[sq, h].T


def _ret_sample_body(*refs, lc, nseq):
    _ret_block(*refs[:5], *refs[6:], lc=lc, nseq=nseq)


def _ret_block(pc_ref, cos_ref, sin_ref, norm_ref, s0_ref, out_ref, s_ref, qs_scr, qr_scr, ke_scr, *, lc, nseq):
    rows = lc * nseq
    causal, r, c = _causal_mask(rows, lc)
    diff = (r - c).astype(F32)
    tin = (lax.broadcasted_iota(jnp.int32, (rows, 1), 0) & (lc - 1)).astype(F32)
    cos = cos_ref[...]
    sin = sin_ref[...]

    def head_cols(group, h):
        return slice(group * GROUP_W + h * DH, group * GROUP_W + (h + 1) * DH)

    def rot(x):
        return x * cos + pltpu.roll(x, DH // 2, axis=1) * sin

    for h in range(H_C):
        qr_scr[h] = rot(pc_ref[:, head_cols(0, h)])
        ke_scr[h] = rot(pc_ref[:, head_cols(1, h)]) * QK_SCALE * jnp.exp((lc - 1.0 - tin) * LOG_GAMMA[h])

    def seq_body(j, carry):
        r0 = pl.multiple_of(j * lc, lc)
        for h in range(H_C):
            qj = qr_scr[h, pl.ds(r0, lc), :].astype(BF16)
            kj = ke_scr[h, pl.ds(r0, lc), :].astype(BF16)
            vj = pc_ref[pl.ds(r0, lc), head_cols(2, h)].astype(BF16)
            s_old = s0_ref[j, h]
            qs_scr[h, pl.ds(r0, lc), :] = _dot(qj, s_old.astype(BF16))
            s_ref[j, h] = math.exp(lc * LOG_GAMMA[h]) * s_old + _dot_tn(kj, vj)
        return carry

    lax.fori_loop(0, nseq, seq_body, 0, unroll=SEQ_UNROLL)

    scores = []
    for h in range(H_C):
        kr = (rot(pc_ref[:, head_cols(1, h)]) * QK_SCALE).astype(BF16)
        scores.append(_dot_nt(qr_scr[h].astype(BF16), kr))
    for h in range(H_C):
        lg = LOG_GAMMA[h]
        decay = jnp.where(causal, jnp.exp(diff * lg), 0.0)
        v = pc_ref[:, head_cols(2, h)].astype(BF16)
        gate = pc_ref[:, head_cols(3, h)]
        o = _dot((scores[h] * decay).astype(BF16), v) + jnp.exp((tin + 1.0) * lg) * qs_scr[h]
        on = o * lax.rsqrt(jnp.mean(o * o, axis=1, keepdims=True) + EPS) * norm_ref[:, h * DH:(h + 1) * DH]
        out_ref[:, h * DH:(h + 1) * DH] = (on * _silu(gate)).astype(out_ref.dtype)


def _ret_prompt(pt, pk, cos, sin, normb, bsz, nchunk, layer):
    seq = nchunk * ROWS
    chunk = lambda b, c: (c, 0)
    chunk_t = lambda b, c: (0, c)
    idx = jnp.arange(ROWS, dtype=F32)
    diff = idx[None, :] - idx[:, None]
    log_gamma = jnp.asarray(LOG_GAMMA, F32)[:, None, None]
    dec = jnp.where(diff >= 0, jnp.exp(diff * log_gamma), 0.0)
    out, s1 = pl.pallas_call(
        functools.partial(_ret_prompt_body, nb=SEQ_PER_STEP),
        grid=(bsz // SEQ_PER_STEP, nchunk),
        in_specs=[_seq_t_spec(3 * GROUP_W), _seq_col_spec(GROUP_W, 1),
                  pl.BlockSpec((DH, ROWS), chunk_t), pl.BlockSpec((DH, ROWS), chunk_t),
                  pl.BlockSpec((ROWS, DH), chunk), pl.BlockSpec((ROWS, DH), chunk),
                  pl.BlockSpec((H_C, ROWS, ROWS), lambda b, c: (0, 0, 0)),
                  _layer_weight_spec((GROUP_W, LANES), layer)],
        out_specs=[_seq_spec(GROUP_W), _seq_state_spec((H_C, DH, DH))],
        out_shape=[jax.ShapeDtypeStruct((bsz, seq, GROUP_W), BF16),
                   jax.ShapeDtypeStruct((bsz, H_C, DH, DH), F32)],
        scratch_shapes=[pltpu.VMEM((SEQ_PER_STEP, H_C, DH, DH), F32)],
        compiler_params=_mixer_params(2),
        name="ret_prompt",
    )(pt.reshape(bsz, nchunk, 3 * GROUP_W, ROWS), pk.reshape(bsz, seq, -1), cos.T, sin.T, cos, sin, dec, normb)
    return out.reshape(bsz * seq, GROUP_W), s1


def _ret_sample(pc, cos, sin, norm, s_all, lc, layer, s_prev):
    t = pc.shape[0]
    nseq = ROWS // lc
    row = lambda i: (i, 0)
    const = lambda i: (0, 0)
    return pl.pallas_call(
        functools.partial(_ret_sample_body, lc=lc, nseq=nseq),
        grid=(t // ROWS,),
        in_specs=[pl.BlockSpec((ROWS, 4 * GROUP_W), row), pl.BlockSpec((ROWS, DH), const),
                  pl.BlockSpec((ROWS, DH), const), _layer_weight_spec((1, GROUP_W), layer),
                  _layer_state_spec((nseq, H_C, DH, DH), layer), _STACK_SPEC],
        out_specs=[pl.BlockSpec((ROWS, GROUP_W), row),
                   _layer_state_spec((nseq, H_C, DH, DH), layer)],
        out_shape=[jax.ShapeDtypeStruct((t, GROUP_W), BF16),
                   jax.ShapeDtypeStruct(s_all.shape, F32)],
        input_output_aliases={5: 1},
        scratch_shapes=[pltpu.VMEM((H_C, ROWS, DH), F32)] * 3,
        compiler_params=_mixer_params(1),
        name="ret_sample",
    )(pc, cos, sin, norm, s_all, s_prev)


HEADS_PER_GROUP = H_B // G_B
PAIR_W = 2 * P_B
GROUP_CH = HEADS_PER_GROUP * P_B


def _ssd_prompt_body(pt_ref, gt_ref, gb_ref, alog_ref, dskipb_ref, cwb_ref, cbb_ref, normb_ref,
                     out_ref, h_ref, tail_ref, xc_scr, *, nb):
    @pl.when(pl.program_id(1) == 0)
    def _init():
        h_ref[...] = jnp.zeros_like(h_ref)
        tail_ref[...] = jnp.zeros_like(tail_ref)

    causal_t, _, _ = _source_target_mask()
    lane = lax.broadcasted_iota(jnp.int32, (DH, ROWS), 1)
    rowid = lax.broadcasted_iota(jnp.int32, (GATE_ROWS, LANES), 0)
    drows = rowid >= GATE_DT
    pad = jnp.zeros((ROWS - GATE_ROWS, LANES), F32)

    dpre = (gt_ref[...] + gb_ref[...][None]).reshape(nb * GATE_ROWS, LANES)
    dt_all = jnp.maximum(dpre, 0.0) + _log1p_exp_neg_abs(dpre)
    a_all = jnp.broadcast_to(-jnp.exp(alog_ref[...])[None], (nb, GATE_ROWS, LANES)).reshape(nb * GATE_ROWS, LANES)
    acum_all = _cumsum_lanes(dt_all * a_all)
    gates = []
    for sq in range(nb):
        rows16 = slice(sq * GATE_ROWS, (sq + 1) * GATE_ROWS)
        dt = dt_all[rows16]
        acum = jnp.where(drows, acum_all[rows16], 0.0)
        alast = _last_lane(acum)
        gates.append(dict(dt=dt, acum=acum, at=jnp.concatenate([acum, pad], axis=0).T, exp_a=jnp.exp(acum),
                          wx=jnp.exp(alast - acum), g_a=jnp.exp(alast)))

    for sq in range(nb):
        for blk in range(CONV_DIM // DH):
            ch = slice(blk * DH, (blk + 1) * DH)
            new = pt_ref[sq, GROUP_W + blk * DH:GROUP_W + (blk + 1) * DH, :]
            prev = tail_ref[sq, ch, :]
            acc = new * cwb_ref[CONV_W - 1, ch, :]
            for k in range(1, CONV_W):
                shifted = pltpu.roll(jnp.where(lane >= ROWS - k, prev, new), k, axis=1)
                acc = acc + shifted * cwb_ref[CONV_W - 1 - k, ch, :]
            xc_scr[sq, ch, :] = _silu(acc + cbb_ref[ch, :])
            tail_ref[sq, ch, :] = new

    def x_rows(hd):
        return slice(hd * P_B, (hd + 1) * P_B)

    def b_rows(g):
        return slice(GROUP_W + g * N_B, GROUP_W + (g + 1) * N_B)

    def c_rows(g):
        return slice(GROUP_W + G_B * N_B + g * N_B, GROUP_W + G_B * N_B + (g + 1) * N_B)

    groups = [(sq, g) for sq in range(nb) for g in range(G_B)]
    cbs, chs, upds, xdts = {}, {}, {}, {}
    for sq, g in groups:
        gs = gates[sq]
        bg = xc_scr[sq, b_rows(g), :].T.astype(BF16)
        ct = xc_scr[sq, c_rows(g), :].astype(BF16)
        xws = []
        for r in range(HEADS_PER_GROUP):
            hd = g * HEADS_PER_GROUP + r
            c = GATE_DT + hd
            xdt = xc_scr[sq, x_rows(hd), :] * gs["dt"][c:c + 1, :]
            xdts[sq, hd] = xdt.astype(BF16)
            xws.append((xdt * gs["wx"][c:c + 1, :]).astype(BF16))
        h_old = jnp.concatenate([h_ref[sq, g * HEADS_PER_GROUP + r] for r in range(HEADS_PER_GROUP)], axis=0)
        cbs[sq, g] = _dot(bg, ct)
        chs[sq, g] = _dot(h_old.astype(BF16), ct)
        upds[sq, g] = _dot(jnp.concatenate(xws, axis=0), bg)

    ys = {}
    for sq, g in groups:
        gs = gates[sq]
        for r in range(HEADS_PER_GROUP):
            hd = g * HEADS_PER_GROUP + r
            c = GATE_DT + hd
            dec = jnp.where(causal_t, jnp.exp(gs["acum"][c:c + 1, :] - gs["at"][:, c:c + 1]), 0.0)
            ys[sq, hd] = _dot(xdts[sq, hd], (cbs[sq, g] * dec).astype(BF16))
            h_ref[sq, hd] = gs["g_a"][c:c + 1, 0:1] * h_ref[sq, hd] + upds[sq, g][r * P_B:(r + 1) * P_B, :]

    for sq, g in groups:
        gs = gates[sq]
        parts = []
        for r in range(HEADS_PER_GROUP):
            hd = g * HEADS_PER_GROUP + r
            c = GATE_DT + hd
            y = ys[sq, hd] + gs["exp_a"][c:c + 1, :] * chs[sq, g][r * P_B:(r + 1) * P_B, :]
            parts.append(y + dskipb_ref[x_rows(hd), :] * xc_scr[sq, x_rows(hd), :])
        grows = slice(g * GROUP_CH, (g + 1) * GROUP_CH)
        yz = jnp.concatenate(parts, axis=0) * _silu(pt_ref[sq, grows, :])
        yn = yz * lax.rsqrt(jnp.mean(yz * yz, axis=0, keepdims=True) + EPS) * normb_ref[grows, :]
        for j in range(GROUP_CH // DH):
            cols = slice(g * GROUP_CH + j * DH, g * GROUP_CH + (j + 1) * DH)
            out_ref[sq, :, cols] = yn[j * DH:(j + 1) * DH, :].T.astype(out_ref.dtype)


def _ssd_sample_body(*refs, lc, nseq):
    _ssd_block(*refs[:10], *refs[11:], lc=lc, nseq=nseq)


def _ssd_block(pb_ref, pg_ref, gb_ref, alog_ref, dskip_ref, cw_ref, cb_ref, norm_ref, h0_ref, hist_ref,
               out_ref, h_ref, buf_ref, ch_scr, xc_scr, xw_scr, g_scr, *, lc, nseq):
    rows = lc * nseq
    causal, _, _ = _causal_mask(rows, lc)
    lane = lax.broadcasted_iota(jnp.int32, (rows, LANES), 1)
    dcols = (lane >= GATE_DT) & (lane < GATE_DT + H_B)
    low = lax.broadcasted_iota(jnp.int32, (rows, PAIR_W), 1) < P_B

    new = pb_ref[:, GROUP_W:GROUP_W + CONV_DIM]
    tin = lax.broadcasted_iota(jnp.int32, (rows, 1), 0) & (lc - 1)
    acc = new * cw_ref[CONV_W - 1:CONV_W, :]
    for k in range(1, CONV_W):
        rolled = pltpu.roll(new, k, axis=0)
        hist = pltpu.roll(hist_ref[...], (rows + k - lc) % rows, axis=0)
        acc = acc + jnp.where(tin >= k, rolled, hist) * cw_ref[CONV_W - 1 - k:CONV_W - k, :]
    xc = _silu(acc + cb_ref[...])
    buf_ref[...] = new.reshape(nseq, lc, CONV_DIM)[:, lc - (CONV_W - 1):lc, :]

    dpre = pg_ref[...] + gb_ref[...]
    dt = jnp.maximum(dpre, 0.0) + _log1p_exp_neg_abs(dpre)
    adt = dt * (-jnp.exp(alog_ref[...]))
    acum = jnp.where(dcols, _dot_exact(jnp.where(causal, 1.0, 0.0), adt), 0.0)
    at = acum.T
    alast = _group_last(acum, lc)
    exp_a = jnp.exp(acum)
    wx = jnp.exp(alast - acum)
    g_a = jnp.exp(alast)

    def pair_bcast(slab, c0):
        return jnp.where(low, slab[:, c0:c0 + 1], slab[:, c0 + 1:c0 + 2])

    def pair_cols(g, p):
        start = g * GROUP_CH + p * PAIR_W
        return slice(start, start + PAIR_W)

    def b_cols(g):
        return slice(GROUP_W + g * N_B, GROUP_W + (g + 1) * N_B)

    def c_cols(g):
        return slice(GROUP_W + G_B * N_B + g * N_B, GROUP_W + G_B * N_B + (g + 1) * N_B)

    xw_pairs = {}
    xdt_pairs = {}
    for g in range(G_B):
        for p in range(HEADS_PER_GROUP // 2):
            c0 = GATE_DT + g * HEADS_PER_GROUP + 2 * p
            xdt = xc[:, pair_cols(g, p)] * pair_bcast(dt, c0)
            xdt_pairs[g, p] = xdt.astype(BF16)
            xw_pairs[g, p] = xdt * pair_bcast(wx, c0)
    xc_scr[...] = xc
    g_scr[...] = g_a
    for g in range(G_B):
        for p in range(HEADS_PER_GROUP // 2):
            xw_scr[:, pair_cols(g, p)] = xw_pairs[g, p]

    def seq_body(j, carry):
        r0 = pl.multiple_of(j * lc, lc)
        grow = g_scr[pl.ds(r0, 1), :]
        for g in range(G_B):
            bj = xc_scr[pl.ds(r0, lc), b_cols(g)].astype(BF16)
            cj = xc_scr[pl.ds(r0, lc), c_cols(g)].astype(BF16)
            xwj = xw_scr[pl.ds(r0, lc), g * GROUP_CH:(g + 1) * GROUP_CH].astype(BF16)
            hs = [h0_ref[j, g * HEADS_PER_GROUP + r] for r in range(HEADS_PER_GROUP)]
            h_old = jnp.concatenate(hs, axis=0)
            ch_scr[g, pl.ds(r0, lc), :] = _dot_nt(cj, h_old.astype(BF16))
            upd = _dot_tn(xwj, bj)
            for r in range(HEADS_PER_GROUP):
                hd = g * HEADS_PER_GROUP + r
                c = GATE_DT + hd
                h_ref[j, hd] = grow[:, c:c + 1] * hs[r] + upd[r * P_B:(r + 1) * P_B, :]
        return carry

    lax.fori_loop(0, nseq, seq_body, 0, unroll=SEQ_UNROLL)

    cbs = [_dot_nt(xc[:, c_cols(g)].astype(BF16), xc[:, b_cols(g)].astype(BF16)) for g in range(G_B)]
    for g in range(G_B):
        cb = cbs[g]
        ys = []
        for p in range(HEADS_PER_GROUP // 2):
            c0 = GATE_DT + g * HEADS_PER_GROUP + 2 * p
            xpair = xc[:, pair_cols(g, p)]
            xdt = xdt_pairs[g, p]
            halves = []
            for c in (c0, c0 + 1):
                dec = jnp.where(causal, jnp.exp(acum[:, c:c + 1] - at[c:c + 1, :]), 0.0)
                halves.append(_dot((cb * dec).astype(BF16), xdt))
            y = (jnp.where(low, halves[0], halves[1])
                 + pair_bcast(exp_a, c0) * ch_scr[g, :, p * PAIR_W:(p + 1) * PAIR_W])
            ys.append(y + dskip_ref[:, pair_cols(g, p)] * xpair)
        yg = jnp.concatenate(ys, axis=1)
        gcols = slice(g * GROUP_CH, (g + 1) * GROUP_CH)
        yz = yg * _silu(pb_ref[:, gcols])
        yn = yz * lax.rsqrt(jnp.mean(yz * yz, axis=1, keepdims=True) + EPS) * norm_ref[:, gcols]
        out_ref[:, gcols] = yn.astype(out_ref.dtype)


def _ssd_prompt(pt, gt, gb16, alog16, dskipb, cwb, cbb, normb, bsz, nchunk, layer):
    seq = nchunk * ROWS
    out, h1, tail = pl.pallas_call(
        functools.partial(_ssd_prompt_body, nb=SEQ_PER_STEP),
        grid=(bsz // SEQ_PER_STEP, nchunk),
        in_specs=[_seq_t_spec(GROUP_W + CONV_DIM), _seq_t_spec(GATE_ROWS),
                  _layer_weight_spec((GATE_ROWS, LANES), layer), _layer_weight_spec((GATE_ROWS, LANES), layer),
                  _layer_weight_spec((GROUP_W, LANES), layer), _layer_weight_spec((CONV_W, CONV_DIM, LANES), layer),
                  _layer_weight_spec((CONV_DIM, LANES), layer), _layer_weight_spec((GROUP_W, LANES), layer)],
        out_specs=[_seq_spec(GROUP_W), _seq_state_spec((H_B, P_B, N_B)), _seq_state_spec((CONV_DIM, ROWS))],
        out_shape=[jax.ShapeDtypeStruct((bsz, seq, GROUP_W), BF16),
                   jax.ShapeDtypeStruct((bsz, H_B, P_B, N_B), F32),
                   jax.ShapeDtypeStruct((bsz, CONV_DIM, ROWS), F32)],
        scratch_shapes=[pltpu.VMEM((SEQ_PER_STEP, CONV_DIM, ROWS), F32)],
        compiler_params=_mixer_params(2),
        name="ssd_prompt",
    )(pt.reshape(bsz, nchunk, GROUP_W + CONV_DIM, ROWS), gt.reshape(bsz, nchunk, GATE_ROWS, ROWS),
      gb16, alog16, dskipb, cwb, cbb, normb)
    return out.reshape(bsz * seq, GROUP_W), h1, tail


def _ssd_sample(pb, pg, gbias, alog, dskip, cw, cb, norm, h_all, hist, lc, layer, h_prev):
    t = pb.shape[0]
    nseq = ROWS // lc
    bsz = h_all.shape[1]
    row = lambda i: (i, 0)
    return pl.pallas_call(
        functools.partial(_ssd_sample_body, lc=lc, nseq=nseq),
        grid=(t // ROWS,),
        in_specs=[pl.BlockSpec((ROWS, GROUP_W + CONV_DIM), row), pl.BlockSpec((ROWS, LANES), row),
                  _layer_weight_spec((1, LANES), layer), _layer_weight_spec((1, LANES), layer),
                  _layer_weight_spec((1, GROUP_W), layer), _layer_weight_spec((CONV_W, CONV_DIM), layer),
                  _layer_weight_spec((1, CONV_DIM), layer), _layer_weight_spec((1, GROUP_W), layer),
                  _layer_state_spec((nseq, H_B, P_B, N_B), layer),
                  _layer_state_spec((ROWS, CONV_DIM), layer), _STACK_SPEC],
        out_specs=[pl.BlockSpec((ROWS, GROUP_W), row),
                   _layer_state_spec((nseq, H_B, P_B, N_B), layer),
                   pl.BlockSpec((nseq, CONV_W - 1, CONV_DIM), lambda i: (i, 0, 0))],
        out_shape=[jax.ShapeDtypeStruct((t, GROUP_W), BF16),
                   jax.ShapeDtypeStruct(h_all.shape, F32),
                   jax.ShapeDtypeStruct((bsz, CONV_W - 1, CONV_DIM), F32)],
        input_output_aliases={10: 1},
        scratch_shapes=[pltpu.VMEM((G_B, ROWS, GROUP_CH), F32), pltpu.VMEM((ROWS, CONV_DIM), F32),
                        pltpu.VMEM((ROWS, GROUP_W), F32), pltpu.VMEM((ROWS, LANES), F32)],
        compiler_params=_mixer_params(1),
        name="ssd_sample",
    )(pb, pg, gbias, alog, dskip, cw, cb, norm, h_all, hist, h_prev)


def _rope_tables(pos):
    half = DH // 2
    freqs = ROPE_BASE ** (-jnp.arange(half, dtype=F32) / half)
    ang = pos.astype(F32)[:, None] * freqs
    cos, sin = jnp.cos(ang), jnp.sin(ang)
    return jnp.concatenate([cos, cos], axis=-1), jnp.concatenate([-sin, sin], axis=-1)


def kernel(x_prompt, x_sample, state_mlstm_C, state_mlstm_n, state_mlstm_m, state_ssd, state_conv, state_ret,
           ffn1_norm, ffn1_w1, ffn1_w3, ffn1_w2, mix_norm, w_in, b_igate, b_fgate, mlstm_norm,
           conv_w, conv_b, dt_bias, a_log, d_skip, ssd_norm, ret_norm, w_out,
           ffn2_norm, ffn2_w1, ffn2_w3, ffn2_w2, final_norm):
    depth = w_in.shape[0]
    bsz, seq, _ = x_prompt.shape
    dbsz, dseq, _ = x_sample.shape
    assert seq % CHUNK == 0 and ROWS % dseq == 0 and (dbsz * dseq) % ROWS == 0
    assert dseq >= SUBLANES and dseq & (dseq - 1) == 0
    nchunk = seq // CHUNK
    seq_per_block = ROWS // dseq

    xp = x_prompt.reshape(bsz * seq, D_MODEL)
    xs = x_sample.reshape(dbsz * dseq, D_MODEL)

    cos_p, sin_p = _rope_tables(jnp.arange(seq))
    cos_s, sin_s = _rope_tables(PAST_LEN + jnp.arange(dseq))
    cos_s, sin_s = jnp.tile(cos_s, (seq_per_block, 1)), jnp.tile(sin_s, (seq_per_block, 1))

    a0 = 0
    a_gate = a0 + 4 * GROUP_W
    b0 = a_gate + 2 * H_A
    b_dt = b0 + GROUP_W + CONV_DIM
    c0 = b_dt + H_B

    outs_p = [[] for _ in range(6)]
    outs_s = [[] for _ in range(3)]
    c_stack = jnp.zeros(state_mlstm_C.shape, F32)
    h_stack = jnp.zeros(state_ssd.shape, F32)
    s_stack = jnp.zeros(state_ret.shape, F32)
    w1a, w3a, w2a = ffn1_w1.astype(BF16), ffn1_w3.astype(BF16), ffn1_w2.astype(BF16)
    w1b, w3b, w2b = ffn2_w1.astype(BF16), ffn2_w3.astype(BF16), ffn2_w2.astype(BF16)
    wo = w_out.astype(BF16)
    fin = final_norm[None, :]

    row3 = lambda v: v.astype(F32)[:, None, :]
    lanes_bcast = lambda v: jnp.broadcast_to(v.astype(F32)[..., None], v.shape + (LANES,))
    w_in_t = jnp.swapaxes(w_in, 1, 2).astype(BF16)
    cols = lambda lo, hi: w_in_t[:, lo:hi, :]
    qa, ka, va, ga = (cols(a0 + i * GROUP_W, a0 + (i + 1) * GROUP_W) for i in range(4))
    qc, kc, vc, gc = (cols(c0 + i * GROUP_W, c0 + (i + 1) * GROUP_W) for i in range(4))
    w_gates = jnp.concatenate([cols(a_gate, b0), cols(b_dt, c0)], axis=1)
    n_gates = w_gates.shape[1]
    zeros_dt = jnp.zeros((depth, GATE_DT), F32)
    g1, gm, g2 = row3(ffn1_norm), row3(mix_norm), row3(ffn2_norm)
    wa, wb, wc = cols(a0, a_gate), cols(b0, b_dt), cols(c0, c0 + 4 * GROUP_W)
    wg = jnp.pad(w_gates, ((0, 0), (0, LANES - n_gates), (0, 0)))
    gate_row = lambda parts: row3(jnp.pad(jnp.concatenate(parts, axis=1), ((0, 0), (0, LANES - n_gates))))
    gbias = gate_row([b_igate, b_fgate, dt_bias])
    alog = gate_row([zeros_dt, a_log])
    dskip_ch = jnp.repeat(d_skip.astype(F32), P_B, axis=1)
    dskip, na, nb, nc = row3(dskip_ch), row3(mlstm_norm), row3(ssd_norm), row3(ret_norm)
    cw, cb = conv_w.astype(F32), row3(conv_b)
    m0rows = jnp.pad(jnp.repeat(state_mlstm_m.astype(F32), dseq, axis=1),
                     ((0, 0), (0, 0), (GATE_F, LANES - GATE_F - H_A)))
    hist = jnp.pad(state_conv.astype(F32), ((0, 0), (0, 0), (dseq - (CONV_W - 1), 0), (0, 0)))
    hist = hist.reshape(depth, dbsz * dseq, CONV_DIM)
    wta = jnp.concatenate([qa, va, ga], axis=1)
    wtb = wb
    wtc = jnp.concatenate([qc, vc, gc], axis=1)
    wtg = w_gates
    wk = jnp.concatenate([ka, kc], axis=1)
    gb8 = lanes_bcast(jnp.concatenate([b_igate, b_fgate], axis=1))
    gb16 = lanes_bcast(jnp.concatenate([zeros_dt, dt_bias], axis=1))
    alog16 = lanes_bcast(jnp.concatenate([zeros_dt, a_log], axis=1))
    nab, nbb, ncb = lanes_bcast(mlstm_norm), lanes_bcast(ssd_norm), lanes_bcast(ret_norm)
    dskipb, cwb, cbb = lanes_bcast(dskip_ch), lanes_bcast(conv_w), lanes_bcast(conv_b)

    for l in range(depth):
        final = l == depth - 1

        xp = _ffn_call(xp, g1, w1a, w3a, w2a, l)
        pta, ptb, ptc, gt, pk = _inproj_prompt_call(xp, gm, wta, wtb, wtc, wtg, wk, l)
        oa, c1, n1, m1 = _mlstm_prompt(pta, pk, gt, gb8, nab, bsz, nchunk, l)
        ob, h1, tail = _ssd_prompt(ptb, gt, gb16, alog16, dskipb, cwb, cbb, nbb, bsz, nchunk, l)
        oc, s1 = _ret_prompt(ptc, pk, cos_p, sin_p, ncb, bsz, nchunk, l)
        xp = _mix_ffn_call(xp, oa, ob, oc, wo, g2, w1b, w3b, w2b, fin, final, l)
        buf1 = jnp.swapaxes(tail[:, :, ROWS - (CONV_W - 1):], 1, 2)
        for acc, v in zip(outs_p, (c1, n1, m1[:, GATE_F:GATE_F + H_A, 0], h1, buf1, s1)):
            acc.append(v)

        xs = _ffn_call(xs, g1, w1a, w3a, w2a, l)
        pa, pb, pc, pg = _inproj_call(xs, gm, wa, wb, wc, wg, l)
        oa, c_stack, n1, m1 = _mlstm_sample(pa, pg, gbias, na, state_mlstm_C, state_mlstm_n, m0rows, dseq,
                                            l, c_stack)
        ob, h_stack, buf1 = _ssd_sample(pb, pg, gbias, alog, dskip, cw, cb, nb, state_ssd, hist, dseq, l, h_stack)
        oc, s_stack = _ret_sample(pc, cos_s, sin_s, nc, state_ret, dseq, l, s_stack)
        xs = _mix_ffn_call(xs, oa, ob, oc, wo, g2, w1b, w3b, w2b, fin, final, l)
        for acc, v in zip(outs_s, (n1, m1[:, GATE_F:GATE_F + H_A], buf1)):
            acc.append(v)

    y_prompt = xp.reshape(bsz, seq, D_MODEL)
    y_sample = xs.reshape(dbsz, dseq, D_MODEL)
    s_n, s_m, s_buf = [jnp.stack(a) for a in outs_s]
    return (y_prompt, y_sample, *[jnp.stack(a) for a in outs_p], c_stack, s_n, s_m, h_stack, s_buf, s_stack)
```

```python
import functools
import math

import jax
import jax.numpy as jnp
from jax import lax
from jax.experimental import pallas as pl
from jax.experimental.pallas import tpu as pltpu

F32 = jnp.float32
BF16 = jnp.bfloat16
HIGHEST = lax.Precision.HIGHEST

D_MODEL = 1024
D_FF = 2816
GROUP_W = 512
H_A = 4
DH = 128
H_B = 8
P_B = 64
N_B = 128
G_B = 2
CONV_W = 4
CONV_DIM = GROUP_W + 2 * G_B * N_B
H_C = 4
CHUNK = 128
PAST_LEN = 16384
GATE_SOFTCAP = 15.0
ROPE_BASE = 10000.0
EPS = 1e-6
QK_SCALE = DH ** -0.5

LANES = 128
SUBLANES = 8

ROWS = 128
SEQ_PER_STEP = 8
TM_FFN = 1024
SEQ_UNROLL = 8
TM = 512
FC = 256
VMEM_LIMIT = 56 * 1024 * 1024

GATE_I = 0
GATE_F = 4
GATE_DT = 8
GATE_ROWS = 16


def _dot(a, b):
    return jnp.dot(a, b, preferred_element_type=F32)


def _dot_nt(a, b):
    return lax.dot_general(a, b, (((1,), (1,)), ((), ())), preferred_element_type=F32)


def _dot_tn(a, b):
    return lax.dot_general(a, b, (((0,), (0,)), ((), ())), preferred_element_type=F32)


def _dot_exact(a, b):
    return jnp.dot(a, b, precision=HIGHEST, preferred_element_type=F32)


def _rms(x, g):
    return x * lax.rsqrt(jnp.mean(x * x, axis=-1, keepdims=True) + EPS) * g


def _silu(x):
    return x * jax.nn.sigmoid(x)


def _log1p_exp_neg_abs(x):
    return jnp.log1p(jnp.exp(-jnp.abs(x)))


def _causal_mask(rows, lc):
    r = lax.broadcasted_iota(jnp.int32, (rows, rows), 0)
    c = lax.broadcasted_iota(jnp.int32, (rows, rows), 1)
    m = c <= r
    if lc != rows:
        shift = lc.bit_length() - 1
        m = m & ((r >> shift) == (c >> shift))
    return m, r, c


def _group_last(x, lc):
    rows, w = x.shape
    x3 = x.reshape(rows // lc, lc, w)
    return jnp.broadcast_to(x3[:, lc - 1:lc, :], x3.shape).reshape(rows, w)


def _group_first(x, lc):
    rows, w = x.shape
    return x.reshape(rows // lc, lc, w)[:, 0, :]


def _group_sum(x, lc):
    rows, w = x.shape
    return jnp.sum(x.reshape(rows // lc, lc, w), axis=1)


def _group_bcast(x, lc, rows):
    n, w = x.shape
    return jnp.broadcast_to(x[:, None, :], (n, lc, w)).reshape(rows, w)


def _ffn_core(x, g_ref, w1_ref, w3_ref, w2_ref, act_ref):
    h = _rms(x, g_ref[...]).astype(BF16)
    for c in range(D_FF // FC):
        cols = slice(c * FC, (c + 1) * FC)
        a = _dot(h, w1_ref[:, cols])
        b = _dot(h, w3_ref[:, cols])
        act_ref[:, cols] = (_silu(a) * b).astype(BF16)
    return x + 0.5 * _dot(act_ref[...], w2_ref[...])


def _ffn_body(x_ref, g_ref, w1_ref, w3_ref, w2_ref, o_ref, act_ref):
    o_ref[...] = _ffn_core(x_ref[...], g_ref, w1_ref, w3_ref, w2_ref, act_ref)


def _mix_ffn_body(x_ref, oa_ref, ob_ref, oc_ref, wo_ref, g_ref, w1_ref, w3_ref, w2_ref, fin_ref,
                  o_ref, act_ref, *, final):
    x = x_ref[...]
    x = x + (_dot(oa_ref[...], wo_ref[0:GROUP_W, :])
             + _dot(ob_ref[...], wo_ref[GROUP_W:2 * GROUP_W, :])
             + _dot(oc_ref[...], wo_ref[2 * GROUP_W:3 * GROUP_W, :]))
    y = _ffn_core(x, g_ref, w1_ref, w3_ref, w2_ref, act_ref)
    if final:
        y = _rms(y, fin_ref[...])
    o_ref[...] = y


def _ffn_two_source_body(xp_ref, xs_ref, g_ref, w1_ref, w3_ref, w2_ref, o_ref, act_ref, *, n_prompt):
    x = jnp.where(pl.program_id(0) >= n_prompt, xs_ref[...], xp_ref[...])
    o_ref[...] = _ffn_core(x, g_ref, w1_ref, w3_ref, w2_ref, act_ref)


def _mix_ffn_two_source_body(x_ref, oap_ref, obp_ref, ocp_ref, oas_ref, obs_ref, ocs_ref, wo_ref, g_ref,
                             w1_ref, w3_ref, w2_ref, o_ref, act_ref, *, n_prompt):
    sample = pl.program_id(0) >= n_prompt
    oa = jnp.where(sample, oas_ref[...], oap_ref[...])
    ob = jnp.where(sample, obs_ref[...], obp_ref[...])
    oc = jnp.where(sample, ocs_ref[...], ocp_ref[...])
    x = x_ref[...] + (_dot(oa, wo_ref[0:GROUP_W, :]) + _dot(ob, wo_ref[GROUP_W:2 * GROUP_W, :])
                      + _dot(oc, wo_ref[2 * GROUP_W:3 * GROUP_W, :]))
    o_ref[...] = _ffn_core(x, g_ref, w1_ref, w3_ref, w2_ref, act_ref)


def _inproj_body(x_ref, g_ref, wa_ref, wb_ref, wc_ref, wg_ref, pa_ref, pb_ref, pc_ref, pg_ref):
    h = _rms(x_ref[...], g_ref[...]).astype(BF16)
    pa_ref[...] = _dot_nt(h, wa_ref[...])
    pb_ref[...] = _dot_nt(h, wb_ref[...])
    pc_ref[...] = _dot_nt(h, wc_ref[...])
    pg_ref[...] = _dot_nt(h, wg_ref[...])


def _inproj_prompt_body(x_ref, g_ref, wta_ref, wtb_ref, wtc_ref, wtg_ref, wk_ref,
                        pta_ref, ptb_ref, ptc_ref, gt_ref, pk_ref):
    h = _rms(x_ref[...], g_ref[...]).astype(BF16)
    for w_ref, o_ref in ((wta_ref, pta_ref), (wtb_ref, ptb_ref), (wtc_ref, ptc_ref), (wtg_ref, gt_ref)):
        pt = _dot_nt(w_ref[...], h)
        for j in range(TM // ROWS):
            o_ref[j] = pt[:, j * ROWS:(j + 1) * ROWS]
    pk_ref[...] = _dot_nt(h, wk_ref[...])


def _const_spec(shape):
    nd = len(shape)
    return pl.BlockSpec(shape, lambda *_: (0,) * nd, pipeline_mode=pl.Buffered(1))


def _layer_weight_spec(shape, layer):
    nd = len(shape)
    return pl.BlockSpec((None,) + shape, lambda *_: (layer,) + (0,) * nd, pipeline_mode=pl.Buffered(1))


def _row_spec(rows, cols):
    return pl.BlockSpec((rows, cols), lambda i: (i, 0))


def _dense_params():
    return pltpu.CompilerParams(dimension_semantics=("arbitrary",), vmem_limit_bytes=VMEM_LIMIT)


def _ffn_call(x, g, w1, w3, w2, layer):
    t = x.shape[0]
    return pl.pallas_call(
        _ffn_body,
        grid=(t // TM_FFN,),
        in_specs=[_row_spec(TM_FFN, D_MODEL), _layer_weight_spec((1, D_MODEL), layer),
                  _layer_weight_spec((D_MODEL, D_FF), layer), _layer_weight_spec((D_MODEL, D_FF), layer),
                  _layer_weight_spec((D_FF, D_MODEL), layer)],
        out_specs=_row_spec(TM_FFN, D_MODEL),
        out_shape=jax.ShapeDtypeStruct((t, D_MODEL), F32),
        scratch_shapes=[pltpu.VMEM((TM_FFN, D_FF), BF16)],
        compiler_params=_dense_params(),
        name="ffn",
    )(x, g, w1, w3, w2)


def _two_source_specs(cols, n_prompt, tile=TM_FFN):
    return [pl.BlockSpec((tile, cols), lambda i: (jnp.minimum(i, n_prompt - 1), 0)),
            pl.BlockSpec((tile, cols), lambda i: (jnp.maximum(i - n_prompt, 0), 0))]


def _ffn_two_source_call(xp, xs, g, w1, w3, w2, layer):
    t = xp.shape[0] + xs.shape[0]
    n_prompt = xp.shape[0] // TM_FFN
    return pl.pallas_call(
        functools.partial(_ffn_two_source_body, n_prompt=n_prompt),
        grid=(t // TM_FFN,),
        in_specs=_two_source_specs(D_MODEL, n_prompt) + [
                  _layer_weight_spec((1, D_MODEL), layer),
                  _layer_weight_spec((D_MODEL, D_FF), layer), _layer_weight_spec((D_MODEL, D_FF), layer),
                  _layer_weight_spec((D_FF, D_MODEL), layer)],
        out_specs=_row_spec(TM_FFN, D_MODEL),
        out_shape=jax.ShapeDtypeStruct((t, D_MODEL), F32),
        scratch_shapes=[pltpu.VMEM((TM_FFN, D_FF), BF16)],
        compiler_params=_dense_params(),
        name="ffn_merged",
    )(xp, xs, g, w1, w3, w2)


def _mix_ffn_two_source_call(x, mix_p, mix_s, wo, g, w1, w3, w2, layer):
    t = x.shape[0]
    n_prompt = mix_p[0].shape[0] // TM
    pairs = [_two_source_specs(GROUP_W, n_prompt, TM) for _ in range(3)]
    return pl.pallas_call(
        functools.partial(_mix_ffn_two_source_body, n_prompt=n_prompt),
        grid=(t // TM,),
        in_specs=[_row_spec(TM, D_MODEL)] + [p[0] for p in pairs] + [p[1] for p in pairs] + [
                  _layer_weight_spec((3 * GROUP_W, D_MODEL), layer), _layer_weight_spec((1, D_MODEL), layer),
                  _layer_weight_spec((D_MODEL, D_FF), layer), _layer_weight_spec((D_MODEL, D_FF), layer),
                  _layer_weight_spec((D_FF, D_MODEL), layer)],
        out_specs=_row_spec(TM, D_MODEL),
        out_shape=jax.ShapeDtypeStruct((t, D_MODEL), F32),
        scratch_shapes=[pltpu.VMEM((TM, D_FF), BF16)],
        compiler_params=_dense_params(),
        name="mix_ffn_merged",
    )(x, *mix_p, *mix_s, wo, g, w1, w3, w2)


def _mix_ffn_call(x, oa, ob, oc, wo, g, w1, w3, w2, fin, final, layer, tile0=0):
    t = oa.shape[0]
    return pl.pallas_call(
        functools.partial(_mix_ffn_body, final=final),
        grid=(t // TM_FFN,),
        in_specs=[pl.BlockSpec((TM_FFN, D_MODEL), lambda i: (i + tile0, 0))] + [_row_spec(TM_FFN, GROUP_W)] * 3 + [
                  _layer_weight_spec((3 * GROUP_W, D_MODEL), layer), _layer_weight_spec((1, D_MODEL), layer),
                  _layer_weight_spec((D_MODEL, D_FF), layer), _layer_weight_spec((D_MODEL, D_FF), layer),
                  _layer_weight_spec((D_FF, D_MODEL), layer), _const_spec((1, D_MODEL))],
        out_specs=_row_spec(TM_FFN, D_MODEL),
        out_shape=jax.ShapeDtypeStruct((t, D_MODEL), F32),
        scratch_shapes=[pltpu.VMEM((TM_FFN, D_FF), BF16)],
        compiler_params=_dense_params(),
        name="mix_ffn",
    )(x, oa, ob, oc, wo, g, w1, w3, w2, fin)


def _inproj_call(x, g, wa, wb, wc, wg, layer, rows, tile0):
    t = rows
    widths = (wa.shape[1], wb.shape[1], wc.shape[1], wg.shape[1])
    return pl.pallas_call(
        _inproj_body,
        grid=(t // TM,),
        in_specs=([pl.BlockSpec((TM, D_MODEL), lambda i: (i + tile0, 0)), _layer_weight_spec((1, D_MODEL), layer)]
                  + [_layer_weight_spec((w, D_MODEL), layer) for w in widths]),
        out_specs=[_row_spec(TM, w) for w in widths],
        out_shape=[jax.ShapeDtypeStruct((t, w), F32) for w in widths],
        compiler_params=_dense_params(),
        name="inproj",
    )(x, g, wa, wb, wc, wg)


def _inproj_prompt_call(x, g, wta, wtb, wtc, wtg, wk, layer, rows):
    t = rows
    t_rows = (wta.shape[1], wtb.shape[1], wtc.shape[1], wtg.shape[1])
    n_cols = (wk.shape[1],)
    chunk_spec = lambda r: pl.BlockSpec((TM // ROWS, r, ROWS), lambda i: (i, 0, 0))
    return pl.pallas_call(
        _inproj_prompt_body,
        grid=(t // TM,),
        in_specs=([_row_spec(TM, D_MODEL), _layer_weight_spec((1, D_MODEL), layer)]
                  + [_layer_weight_spec((r, D_MODEL), layer) for r in t_rows + n_cols]),
        out_specs=[chunk_spec(r) for r in t_rows] + [_row_spec(TM, w) for w in n_cols],
        out_shape=([jax.ShapeDtypeStruct((t // ROWS, r, ROWS), F32) for r in t_rows]
                   + [jax.ShapeDtypeStruct((t, w), F32) for w in n_cols]),
        compiler_params=_dense_params(),
        name="inproj_prompt",
    )(x, g, wta, wtb, wtc, wtg, wk)


def _source_target_mask():
    src = lax.broadcasted_iota(jnp.int32, (ROWS, ROWS), 0)
    tgt = lax.broadcasted_iota(jnp.int32, (ROWS, ROWS), 1)
    return src <= tgt, src, tgt


def _last_lane(x):
    return jnp.broadcast_to(x[:, LANES - 1:LANES], x.shape)


def _cumsum_lanes(x):
    lane = lax.broadcasted_iota(jnp.int32, x.shape, 1)
    k = 1
    while k < x.shape[1]:
        x = x + jnp.where(lane >= k, pltpu.roll(x, k, axis=1), 0.0)
        k *= 2
    return x


def _mlstm_prompt_body(pt_ref, k_ref, gt_ref, gb_ref, normb_ref, out_ref, c_ref, n_ref, m_ref, ct_scr, *, nb):
    @pl.when(pl.program_id(1) == 0)
    def _init():
        ct_scr[...] = jnp.zeros_like(ct_scr)
        n_ref[...] = jnp.zeros_like(n_ref)
        m_ref[...] = jnp.zeros_like(m_ref)

    causal_t, _, _ = _source_target_mask()
    pre = (gt_ref[:, 0:SUBLANES, :] + gb_ref[...][None]).reshape(nb * SUBLANES, LANES)
    cap_all = GATE_SOFTCAP * jnp.tanh(pre / GATE_SOFTCAP)
    bcum_all = _cumsum_lanes(jnp.minimum(cap_all, 0.0) - _log1p_exp_neg_abs(cap_all))
    gates = []
    for sq in range(nb):
        rows8 = slice(sq * SUBLANES, (sq + 1) * SUBLANES)
        gates.append(_mlstm_prompt_gates(cap_all[rows8], bcum_all[rows8], m_ref.at[sq], causal_t))
    units = [(sq, h) for sq in range(nb) for h in range(H_A)]

    st_raw, cq, qn, c_upd, n_upd, p_all = {}, {}, {}, {}, {}, {}
    for sq, h in units:
        c = GATE_F + h
        gs = gates[sq]
        qt = (pt_ref[sq, h * DH:(h + 1) * DH, :] * QK_SCALE).astype(BF16)
        vt = pt_ref[sq, GROUP_W + h * DH:GROUP_W + (h + 1) * DH, :]
        kf = k_ref[sq, :, h * DH:(h + 1) * DH]
        kb = kf.astype(BF16)
        n8 = jnp.concatenate([n_ref[sq, h:h + 1, :], jnp.zeros((SUBLANES - 1, DH), F32)], axis=0)
        st_raw[sq, h] = _dot(kb, qt)
        cq[sq, h] = _dot(ct_scr[sq, h].astype(BF16), qt)
        qn[sq, h] = _dot(n8.astype(BF16), qt)[0:1, :]
        c_upd[sq, h] = _dot((vt * gs["wend"][c:c + 1, :]).astype(BF16), kb)
        n_upd[sq, h] = jnp.sum(kf * gs["wt"][:, c:c + 1], axis=0, keepdims=True)
        p_all[sq, h] = jnp.where(causal_t, jnp.exp(gs["ut"][:, c:c + 1] + gs["bm"][c:c + 1, :]), 0.0)

    num, den = {}, {}
    for sq, h in units:
        c = GATE_F + h
        gs = gates[sq]
        st = st_raw[sq, h] * p_all[sq, h]
        vt = pt_ref[sq, GROUP_W + h * DH:GROUP_W + (h + 1) * DH, :].astype(BF16)
        g = gs["gint"][c:c + 1, :]
        num[sq, h] = _dot(vt, st.astype(BF16)) + g * cq[sq, h]
        den[sq, h] = jnp.sum(st, axis=0, keepdims=True) + g * qn[sq, h]
        g1 = gs["gend"][c:c + 1, 0:1]
        ct_scr[sq, h] = g1 * ct_scr[sq, h] + c_upd[sq, h]
        n_ref[sq, h:h + 1, :] = g1 * n_ref[sq, h:h + 1, :] + n_upd[sq, h]

    for sq, h in units:
        c = GATE_F + h
        hs = slice(h * DH, (h + 1) * DH)
        ot = pt_ref[sq, 2 * GROUP_W + h * DH:2 * GROUP_W + (h + 1) * DH, :]
        hh = num[sq, h] * (1.0 / jnp.maximum(jnp.abs(den[sq, h]), gates[sq]["emt"][c:c + 1, :]))
        rs = lax.rsqrt(jnp.mean(hh * hh, axis=0, keepdims=True) + EPS)
        o = hh * rs * normb_ref[hs, :] * jax.nn.sigmoid(ot)
        out_ref[sq, :, hs] = o.T.astype(out_ref.dtype)

    @pl.when(pl.program_id(1) == pl.num_programs(1) - 1)
    def _finish():
        for sq in range(nb):
            for h in range(H_A):
                c_ref[sq, h] = ct_scr[sq, h].T


def _mlstm_prompt_gates(cap, bcum, m_ref, causal_t):
    rowid = lax.broadcasted_iota(jnp.int32, (SUBLANES, LANES), 0)
    frows = rowid >= GATE_F
    mprev = m_ref[...]
    inter = bcum + mprev
    u = pltpu.roll(cap, GATE_F - GATE_I, axis=0) - bcum
    ut = jnp.concatenate([u, jnp.zeros((ROWS - SUBLANES, LANES), F32)], axis=0).T
    mt_all = jnp.zeros((SUBLANES, LANES), F32)
    for h in range(H_A):
        c = GATE_F + h
        d = jnp.where(causal_t, ut[:, c:c + 1] + bcum[c:c + 1, :], -jnp.inf)
        mt = jnp.maximum(inter[c:c + 1, :], jnp.max(d, axis=0, keepdims=True))
        mt_all = jnp.where(rowid == c, mt, mt_all)
    blast = _last_lane(bcum)
    mend = _last_lane(mt_all)
    wend = jnp.where(frows, jnp.exp(blast + u - mend), 0.0)
    gend = jnp.where(frows, jnp.exp(blast + mprev - mend), 0.0)
    gint = jnp.where(frows, jnp.exp(inter - mt_all), 0.0)
    emt = jnp.exp(-mt_all)
    bm = bcum - mt_all
    m_ref[...] = jnp.where(frows, mend, 0.0)
    wt = jnp.concatenate([wend, jnp.zeros((ROWS - SUBLANES, LANES), F32)], axis=0).T
    return dict(ut=ut, bm=bm, wend=wend, wt=wt, gend=gend, gint=gint, emt=emt)


def _mlstm_sample_body(*refs, lc, nseq):
    _mlstm_block(*refs[:7], *refs[8:], lc=lc, nseq=nseq)


def _mlstm_block(pa_ref, pg_ref, gb_ref, norm_ref, c0_ref, n0_ref, m0_ref,
                 out_ref, c_ref, n_ref, m_ref, qc_scr, kw_scr, g_scr, *, lc, nseq):
    rows = lc * nseq
    mprev = m0_ref[...]
    causal, _, _ = _causal_mask(rows, lc)
    lane = lax.broadcasted_iota(jnp.int32, (rows, LANES), 1)
    fcols = (lane >= GATE_F) & (lane < GATE_F + H_A)

    pre = pg_ref[...] + gb_ref[...]
    cap = GATE_SOFTCAP * jnp.tanh(pre / GATE_SOFTCAP)
    logf = jnp.minimum(cap, 0.0) - _log1p_exp_neg_abs(cap)
    bcum = _dot_exact(jnp.where(causal, 1.0, 0.0), logf)
    inter = bcum + mprev
    u = pltpu.roll(cap, GATE_F - GATE_I, axis=1) - bcum
    ut = u.T
    mt_all = jnp.zeros((rows, LANES), F32)
    for h in range(H_A):
        c = GATE_F + h
        d = jnp.where(causal, bcum[:, c:c + 1] + ut[c:c + 1, :], -jnp.inf)
        mt = jnp.maximum(inter[:, c:c + 1], jnp.max(d, axis=1, keepdims=True))
        mt_all = jnp.where(lane == c, mt, mt_all)
    blast = _group_last(bcum, lc)
    mend = _group_last(mt_all, lc)
    wend = jnp.where(fcols, jnp.exp(blast + u - mend), 0.0)
    gend = jnp.where(fcols, jnp.exp(blast + mprev - mend), 0.0)
    gint = jnp.where(fcols, jnp.exp(inter - mt_all), 0.0)
    emt = jnp.exp(-mt_all)
    bm = bcum - mt_all

    def head_cols(group, h):
        return slice(group * GROUP_W + h * DH, group * GROUP_W + (h + 1) * DH)

    g_scr[...] = gend
    for h in range(H_A):
        c = GATE_F + h
        kw_scr[h] = pa_ref[:, head_cols(1, h)] * wend[:, c:c + 1]

    def seq_body(j, carry):
        r0 = pl.multiple_of(j * lc, lc)
        grow = g_scr[pl.ds(r0, 1), :]
        for h in range(H_A):
            c = GATE_F + h
            qj = (pa_ref[pl.ds(r0, lc), head_cols(0, h)] * QK_SCALE).astype(BF16)
            vj = pa_ref[pl.ds(r0, lc), head_cols(2, h)].astype(BF16)
            kwj = kw_scr[h, pl.ds(r0, lc), :].astype(BF16)
            c_old = c0_ref[j, h]
            qc_scr[h, pl.ds(r0, lc), :] = _dot(qj, c_old.astype(BF16))
            c_ref[j, h] = grow[:, c:c + 1] * c_old + _dot_tn(kwj, vj)
        return carry

    lax.fori_loop(0, nseq, seq_body, 0, unroll=SEQ_UNROLL)
    n_rows = []
    gfirst = _group_first(gend, lc)
    for h in range(H_A):
        c = GATE_F + h
        n_old = n0_ref[:, h, :]
        n_rows.append(_group_bcast(n_old, lc, rows))
        n_ref[:, h, :] = gfirst[:, c:c + 1] * n_old + _group_sum(kw_scr[h], lc)
    m_ref[...] = _group_first(mend, lc)

    scores = []
    for h in range(H_A):
        qb = (pa_ref[:, head_cols(0, h)] * QK_SCALE).astype(BF16)
        scores.append(_dot_nt(qb, pa_ref[:, head_cols(1, h)].astype(BF16)))
    for h in range(H_A):
        c = GATE_F + h
        qf = pa_ref[:, head_cols(0, h)] * QK_SCALE
        v = pa_ref[:, head_cols(2, h)].astype(BF16)
        og = pa_ref[:, head_cols(3, h)]
        p = jnp.where(causal, jnp.exp(bm[:, c:c + 1] + ut[c:c + 1, :]), 0.0)
        s = scores[h] * p
        g = gint[:, c:c + 1]
        num = _dot(s.astype(BF16), v) + g * qc_scr[h]
        qn = jnp.sum(qf * n_rows[h], axis=1, keepdims=True)
        den = jnp.sum(s, axis=1, keepdims=True) + g * qn
        hh = num / jnp.maximum(jnp.abs(den), emt[:, c:c + 1])
        hn = hh * lax.rsqrt(jnp.mean(hh * hh, axis=1, keepdims=True) + EPS) * norm_ref[:, h * DH:(h + 1) * DH]
        out_ref[:, h * DH:(h + 1) * DH] = (hn * jax.nn.sigmoid(og)).astype(out_ref.dtype)


def _mixer_params(ndims):
    return pltpu.CompilerParams(dimension_semantics=("arbitrary",) * ndims, vmem_limit_bytes=VMEM_LIMIT)


def _seq_spec(cols):
    return pl.BlockSpec((SEQ_PER_STEP, ROWS, cols), lambda b, c: (b, c, 0))


def _seq_state_spec(shape):
    nd = len(shape)
    return pl.BlockSpec((SEQ_PER_STEP,) + shape, lambda b, c: (b,) + (0,) * nd)


def _seq_t_spec(rows):
    return pl.BlockSpec((SEQ_PER_STEP, None, rows, ROWS), lambda b, c: (b, c, 0, 0))


def _seq_col_spec(cols, col_block):
    return pl.BlockSpec((SEQ_PER_STEP, ROWS, cols), lambda b, c: (b, c, col_block))


def _mlstm_prompt(pt, pk, gt, gb8, normb, bsz, nchunk, layer):
    seq = nchunk * ROWS
    out, c1, n1, m1 = pl.pallas_call(
        functools.partial(_mlstm_prompt_body, nb=SEQ_PER_STEP),
        grid=(bsz // SEQ_PER_STEP, nchunk),
        in_specs=[_seq_t_spec(3 * GROUP_W), _seq_col_spec(GROUP_W, 0), _seq_t_spec(GATE_ROWS),
                  _layer_weight_spec((SUBLANES, LANES), layer), _layer_weight_spec((GROUP_W, LANES), layer)],
        out_specs=[_seq_spec(GROUP_W), _seq_state_spec((H_A, DH, DH)), _seq_state_spec((H_A, DH)),
                   _seq_state_spec((SUBLANES, LANES))],
        out_shape=[jax.ShapeDtypeStruct((bsz, seq, GROUP_W), BF16),
                   jax.ShapeDtypeStruct((bsz, H_A, DH, DH), F32),
                   jax.ShapeDtypeStruct((bsz, H_A, DH), F32),
                   jax.ShapeDtypeStruct((bsz, SUBLANES, LANES), F32)],
        scratch_shapes=[pltpu.VMEM((SEQ_PER_STEP, H_A, DH, DH), F32)],
        compiler_params=_mixer_params(2),
        name="mlstm_prompt",
    )(pt.reshape(bsz, nchunk, 3 * GROUP_W, ROWS), pk.reshape(bsz, seq, -1),
      gt.reshape(bsz, nchunk, GATE_ROWS, ROWS), gb8, normb)
    return out.reshape(bsz * seq, GROUP_W), c1, n1, m1


def _layer_state_spec(shape, layer):
    nd = len(shape)
    return pl.BlockSpec((None,) + shape, lambda i: (layer, i) + (0,) * (nd - 1))


_STACK_SPEC = pl.BlockSpec(memory_space=pl.ANY)


def _mlstm_sample(pa, pg, gbias, norm, c_all, n_all, m0rows, lc, layer, c_prev):
    t = pa.shape[0]
    nseq = ROWS // lc
    bsz = c_all.shape[1]
    row = lambda i: (i, 0)
    return pl.pallas_call(
        functools.partial(_mlstm_sample_body, lc=lc, nseq=nseq),
        grid=(t // ROWS,),
        in_specs=[pl.BlockSpec((ROWS, 4 * GROUP_W), row), pl.BlockSpec((ROWS, LANES), row),
                  _layer_weight_spec((1, LANES), layer), _layer_weight_spec((1, GROUP_W), layer),
                  _layer_state_spec((nseq, H_A, DH, DH), layer),
                  _layer_state_spec((nseq, H_A, DH), layer),
                  _layer_state_spec((ROWS, LANES), layer), _STACK_SPEC],
        out_specs=[pl.BlockSpec((ROWS, GROUP_W), row),
                   _layer_state_spec((nseq, H_A, DH, DH), layer),
                   pl.BlockSpec((nseq, H_A, DH), lambda i: (i, 0, 0)),
                   pl.BlockSpec((nseq, LANES), row)],
        out_shape=[jax.ShapeDtypeStruct((t, GROUP_W), BF16),
                   jax.ShapeDtypeStruct(c_all.shape, F32),
                   jax.ShapeDtypeStruct((bsz, H_A, DH), F32),
                   jax.ShapeDtypeStruct((bsz, LANES), F32)],
        input_output_aliases={7: 1},
        scratch_shapes=[pltpu.VMEM((H_A, ROWS, DH), F32), pltpu.VMEM((H_A, ROWS, DH), F32),
                        pltpu.VMEM((ROWS, LANES), F32)],
        compiler_params=_mixer_params(1),
        name="mlstm_sample",
    )(pa, pg, gbias, norm, c_all, n_all, m0rows, c_prev)


LOG_GAMMA = [math.log(1.0 - 2.0 ** (-5.0 - h)) for h in range(H_C)]


def _ret_prompt_body(pt_ref, k_ref, cost_ref, sint_ref, cos_ref, sin_ref, dec_ref, normb_ref, out_ref, s_ref,
                     st_scr, *, nb):
    @pl.when(pl.program_id(1) == 0)
    def _init():
        st_scr[...] = jnp.zeros_like(st_scr)

    tin = lax.broadcasted_iota(jnp.int32, (1, ROWS), 1).astype(F32)
    cost, sint = cost_ref[...], sint_ref[...]
    cos, sin = cos_ref[...], sin_ref[...]
    units = [(sq, h) for sq in range(nb) for h in range(H_C)]

    st_raw, qs, s_upd = {}, {}, {}
    for sq, h in units:
        lg = LOG_GAMMA[h]
        qt = pt_ref[sq, h * DH:(h + 1) * DH, :]
        vt = pt_ref[sq, GROUP_W + h * DH:GROUP_W + (h + 1) * DH, :]
        kf = k_ref[sq, :, h * DH:(h + 1) * DH]
        qr = (qt * cost + pltpu.roll(qt, DH // 2, axis=0) * sint).astype(BF16)
        kr = ((kf * cos + pltpu.roll(kf, DH // 2, axis=1) * sin) * QK_SCALE).astype(BF16)
        ve = (vt * jnp.exp((ROWS - 1.0 - tin) * lg)).astype(BF16)
        st_raw[sq, h] = _dot(kr, qr)
        qs[sq, h] = _dot(st_scr[sq, h].astype(BF16), qr)
        s_upd[sq, h] = _dot(ve, kr)

    o_all = {}
    for sq, h in units:
        lg = LOG_GAMMA[h]
        vt = pt_ref[sq, GROUP_W + h * DH:GROUP_W + (h + 1) * DH, :].astype(BF16)
        st = (st_raw[sq, h] * dec_ref[h]).astype(BF16)
        o_all[sq, h] = _dot(vt, st) + jnp.exp((tin + 1.0) * lg) * qs[sq, h]
        st_scr[sq, h] = math.exp(ROWS * lg) * st_scr[sq, h] + s_upd[sq, h]

    for sq, h in units:
        hs = slice(h * DH, (h + 1) * DH)
        gt = pt_ref[sq, 2 * GROUP_W + h * DH:2 * GROUP_W + (h + 1) * DH, :]
        o = o_all[sq, h]
        rs = lax.rsqrt(jnp.mean(o * o, axis=0, keepdims=True) + EPS)
        on = o * rs * normb_ref[hs, :] * _silu(gt)
        out_ref[sq, :, hs] = on.T.astype(out_ref.dtype)

    @pl.when(pl.program_id(1) == pl.num_programs(1) - 1)
    def _finish():
        for sq in range(nb):
            for h in range(H_C):
                s_ref[sq, h] = st_scr[sq, h].T


def _ret_sample_body(*refs, lc, nseq):
    _ret_block(*refs[:5], *refs[6:], lc=lc, nseq=nseq)


def _ret_block(pc_ref, cos_ref, sin_ref, norm_ref, s0_ref, out_ref, s_ref, qs_scr, qr_scr, ke_scr, *, lc, nseq):
    rows = lc * nseq
    causal, r, c = _causal_mask(rows, lc)
    diff = (r - c).astype(F32)
    tin = (lax.broadcasted_iota(jnp.int32, (rows, 1), 0) & (lc - 1)).astype(F32)
    cos = cos_ref[...]
    sin = sin_ref[...]

    def head_cols(group, h):
        return slice(group * GROUP_W + h * DH, group * GROUP_W + (h + 1) * DH)

    def rot(x):
        return x * cos + pltpu.roll(x, DH // 2, axis=1) * sin

    for h in range(H_C):
        qr_scr[h] = rot(pc_ref[:, head_cols(0, h)])
        ke_scr[h] = rot(pc_ref[:, head_cols(1, h)]) * QK_SCALE * jnp.exp((lc - 1.0 - tin) * LOG_GAMMA[h])

    def seq_body(j, carry):
        r0 = pl.multiple_of(j * lc, lc)
        for h in range(H_C):
            qj = qr_scr[h, pl.ds(r0, lc), :].astype(BF16)
            kj = ke_scr[h, pl.ds(r0, lc), :].astype(BF16)
            vj = pc_ref[pl.ds(r0, lc), head_cols(2, h)].astype(BF16)
            s_old = s0_ref[j, h]
            qs_scr[h, pl.ds(r0, lc), :] = _dot(qj, s_old.astype(BF16))
            s_ref[j, h] = math.exp(lc * LOG_GAMMA[h]) * s_old + _dot_tn(kj, vj)
        return carry

    lax.fori_loop(0, nseq, seq_body, 0, unroll=SEQ_UNROLL)

    scores = []
    for h in range(H_C):
        kr = (rot(pc_ref[:, head_cols(1, h)]) * QK_SCALE).astype(BF16)
        scores.append(_dot_nt(qr_scr[h].astype(BF16), kr))
    for h in range(H_C):
        lg = LOG_GAMMA[h]
        decay = jnp.where(causal, jnp.exp(diff * lg), 0.0)
        v = pc_ref[:, head_cols(2, h)].astype(BF16)
        gate = pc_ref[:, head_cols(3, h)]
        o = _dot((scores[h] * decay).astype(BF16), v) + jnp.exp((tin + 1.0) * lg) * qs_scr[h]
        on = o * lax.rsqrt(jnp.mean(o * o, axis=1, keepdims=True) + EPS) * norm_ref[:, h * DH:(h + 1) * DH]
        out_ref[:, h * DH:(h + 1) * DH] = (on * _silu(gate)).astype(out_ref.dtype)


def _ret_prompt(pt, pk, cos, sin, normb, bsz, nchunk, layer):
    seq = nchunk * ROWS
    chunk = lambda b, c: (c, 0)
    chunk_t = lambda b, c: (0, c)
    idx = jnp.arange(ROWS, dtype=F32)
    diff = idx[None, :] - idx[:, None]
    log_gamma = jnp.asarray(LOG_GAMMA, F32)[:, None, None]
    dec = jnp.where(diff >= 0, jnp.exp(diff * log_gamma), 0.0)
    out, s1 = pl.pallas_call(
        functools.partial(_ret_prompt_body, nb=SEQ_PER_STEP),
        grid=(bsz // SEQ_PER_STEP, nchunk),
        in_specs=[_seq_t_spec(3 * GROUP_W), _seq_col_spec(GROUP_W, 1),
                  pl.BlockSpec((DH, ROWS), chunk_t), pl.BlockSpec((DH, ROWS), chunk_t),
                  pl.BlockSpec((ROWS, DH), chunk), pl.BlockSpec((ROWS, DH), chunk),
                  pl.BlockSpec((H_C, ROWS, ROWS), lambda b, c: (0, 0, 0)),
                  _layer_weight_spec((GROUP_W, LANES), layer)],
        out_specs=[_seq_spec(GROUP_W), _seq_state_spec((H_C, DH, DH))],
        out_shape=[jax.ShapeDtypeStruct((bsz, seq, GROUP_W), BF16),
                   jax.ShapeDtypeStruct((bsz, H_C, DH, DH), F32)],
        scratch_shapes=[pltpu.VMEM((SEQ_PER_STEP, H_C, DH, DH), F32)],
        compiler_params=_mixer_params(2),
        name="ret_prompt",
    )(pt.reshape(bsz, nchunk, 3 * GROUP_W, ROWS), pk.reshape(bsz, seq, -1), cos.T, sin.T, cos, sin, dec, normb)
    return out.reshape(bsz * seq, GROUP_W), s1


def _ret_sample(pc, cos, sin, norm, s_all, lc, layer, s_prev):
    t = pc.shape[0]
    nseq = ROWS // lc
    row = lambda i: (i, 0)
    const = lambda i: (0, 0)
    return pl.pallas_call(
        functools.partial(_ret_sample_body, lc=lc, nseq=nseq),
        grid=(t // ROWS,),
        in_specs=[pl.BlockSpec((ROWS, 4 * GROUP_W), row), pl.BlockSpec((ROWS, DH), const),
                  pl.BlockSpec((ROWS, DH), const), _layer_weight_spec((1, GROUP_W), layer),
                  _layer_state_spec((nseq, H_C, DH, DH), layer), _STACK_SPEC],
        out_specs=[pl.BlockSpec((ROWS, GROUP_W), row),
                   _layer_state_spec((nseq, H_C, DH, DH), layer)],
        out_shape=[jax.ShapeDtypeStruct((t, GROUP_W), BF16),
                   jax.ShapeDtypeStruct(s_all.shape, F32)],
        input_output_aliases={5: 1},
        scratch_shapes=[pltpu.VMEM((H_C, ROWS, DH), F32)] * 3,
        compiler_params=_mixer_params(1),
        name="ret_sample",
    )(pc, cos, sin, norm, s_all, s_prev)


HEADS_PER_GROUP = H_B // G_B
PAIR_W = 2 * P_B
GROUP_CH = HEADS_PER_GROUP * P_B


def _ssd_prompt_body(pt_ref, gt_ref, gb_ref, alog_ref, dskipb_ref, cwb_ref, cbb_ref, normb_ref,
                     out_ref, h_ref, tail_ref, xc_scr, *, nb):
    @pl.when(pl.program_id(1) == 0)
    def _init():
        h_ref[...] = jnp.zeros_like(h_ref)
        tail_ref[...] = jnp.zeros_like(tail_ref)

    causal_t, _, _ = _source_target_mask()
    lane = lax.broadcasted_iota(jnp.int32, (DH, ROWS), 1)
    rowid = lax.broadcasted_iota(jnp.int32, (GATE_ROWS, LANES), 0)
    drows = rowid >= GATE_DT
    pad = jnp.zeros((ROWS - GATE_ROWS, LANES), F32)

    dpre = (gt_ref[...] + gb_ref[...][None]).reshape(nb * GATE_ROWS, LANES)
    dt_all = jnp.maximum(dpre, 0.0) + _log1p_exp_neg_abs(dpre)
    a_all = jnp.broadcast_to(-jnp.exp(alog_ref[...])[None], (nb, GATE_ROWS, LANES)).reshape(nb * GATE_ROWS, LANES)
    acum_all = _cumsum_lanes(dt_all * a_all)
    gates = []
    for sq in range(nb):
        rows16 = slice(sq * GATE_ROWS, (sq + 1) * GATE_ROWS)
        dt = dt_all[rows16]
        acum = jnp.where(drows, acum_all[rows16], 0.0)
        alast = _last_lane(acum)
        gates.append(dict(dt=dt, acum=acum, at=jnp.concatenate([acum, pad], axis=0).T, exp_a=jnp.exp(acum),
                          wx=jnp.exp(alast - acum), g_a=jnp.exp(alast)))

    for sq in range(nb):
        for blk in range(CONV_DIM // DH):
            ch = slice(blk * DH, (blk + 1) * DH)
            new = pt_ref[sq, GROUP_W + blk * DH:GROUP_W + (blk + 1) * DH, :]
            prev = tail_ref[sq, ch, :]
            acc = new * cwb_ref[CONV_W - 1, ch, :]
            for k in range(1, CONV_W):
                shifted = pltpu.roll(jnp.where(lane >= ROWS - k, prev, new), k, axis=1)
                acc = acc + shifted * cwb_ref[CONV_W - 1 - k, ch, :]
            xc_scr[sq, ch, :] = _silu(acc + cbb_ref[ch, :])
            tail_ref[sq, ch, :] = new

    def x_rows(hd):
        return slice(hd * P_B, (hd + 1) * P_B)

    def b_rows(g):
        return slice(GROUP_W + g * N_B, GROUP_W + (g + 1) * N_B)

    def c_rows(g):
        return slice(GROUP_W + G_B * N_B + g * N_B, GROUP_W + G_B * N_B + (g + 1) * N_B)

    groups = [(sq, g) for sq in range(nb) for g in range(G_B)]
    cbs, chs, upds, xdts = {}, {}, {}, {}
    for sq, g in groups:
        gs = gates[sq]
        bg = xc_scr[sq, b_rows(g), :].T.astype(BF16)
        ct = xc_scr[sq, c_rows(g), :].astype(BF16)
        xws = []
        for r in range(HEADS_PER_GROUP):
            hd = g * HEADS_PER_GROUP + r
            c = GATE_DT + hd
            xdt = xc_scr[sq, x_rows(hd), :] * gs["dt"][c:c + 1, :]
            xdts[sq, hd] = xdt.astype(BF16)
            xws.append((xdt * gs["wx"][c:c + 1, :]).astype(BF16))
        h_old = jnp.concatenate([h_ref[sq, g * HEADS_PER_GROUP + r] for r in range(HEADS_PER_GROUP)], axis=0)
        cbs[sq, g] = _dot(bg, ct)
        chs[sq, g] = _dot(h_old.astype(BF16), ct)
        upds[sq, g] = _dot(jnp.concatenate(xws, axis=0), bg)

    ys = {}
    for sq, g in groups:
        gs = gates[sq]
        for r in range(HEADS_PER_GROUP):
            hd = g * HEADS_PER_GROUP + r
            c = GATE_DT + hd
            dec = jnp.where(causal_t, jnp.exp(gs["acum"][c:c + 1, :] - gs["at"][:, c:c + 1]), 0.0)
            ys[sq, hd] = _dot(xdts[sq, hd], (cbs[sq, g] * dec).astype(BF16))
            h_ref[sq, hd] = gs["g_a"][c:c + 1, 0:1] * h_ref[sq, hd] + upds[sq, g][r * P_B:(r + 1) * P_B, :]

    for sq, g in groups:
        gs = gates[sq]
        parts = []
        for r in range(HEADS_PER_GROUP):
            hd = g * HEADS_PER_GROUP + r
            c = GATE_DT + hd
            y = ys[sq, hd] + gs["exp_a"][c:c + 1, :] * chs[sq, g][r * P_B:(r + 1) * P_B, :]
            parts.append(y + dskipb_ref[x_rows(hd), :] * xc_scr[sq, x_rows(hd), :])
        grows = slice(g * GROUP_CH, (g + 1) * GROUP_CH)
        yz = jnp.concatenate(parts, axis=0) * _silu(pt_ref[sq, grows, :])
        yn = yz * lax.rsqrt(jnp.mean(yz * yz, axis=0, keepdims=True) + EPS) * normb_ref[grows, :]
        for j in range(GROUP_CH // DH):
            cols = slice(g * GROUP_CH + j * DH, g * GROUP_CH + (j + 1) * DH)
            out_ref[sq, :, cols] = yn[j * DH:(j + 1) * DH, :].T.astype(out_ref.dtype)


def _ssd_sample_body(*refs, lc, nseq):
    _ssd_block(*refs[:10], *refs[11:], lc=lc, nseq=nseq)


def _ssd_block(pb_ref, pg_ref, gb_ref, alog_ref, dskip_ref, cw_ref, cb_ref, norm_ref, h0_ref, hist_ref,
               out_ref, h_ref, buf_ref, ch_scr, xc_scr, xw_scr, g_scr, *, lc, nseq):
    rows = lc * nseq
    causal, _, _ = _causal_mask(rows, lc)
    lane = lax.broadcasted_iota(jnp.int32, (rows, LANES), 1)
    dcols = (lane >= GATE_DT) & (lane < GATE_DT + H_B)
    low = lax.broadcasted_iota(jnp.int32, (rows, PAIR_W), 1) < P_B

    new = pb_ref[:, GROUP_W:GROUP_W + CONV_DIM]
    tin = lax.broadcasted_iota(jnp.int32, (rows, 1), 0) & (lc - 1)
    acc = new * cw_ref[CONV_W - 1:CONV_W, :]
    for k in range(1, CONV_W):
        rolled = pltpu.roll(new, k, axis=0)
        hist = pltpu.roll(hist_ref[...], (rows + k - lc) % rows, axis=0)
        acc = acc + jnp.where(tin >= k, rolled, hist) * cw_ref[CONV_W - 1 - k:CONV_W - k, :]
    xc = _silu(acc + cb_ref[...])
    buf_ref[...] = new.reshape(nseq, lc, CONV_DIM)[:, lc - (CONV_W - 1):lc, :]

    dpre = pg_ref[...] + gb_ref[...]
    dt = jnp.maximum(dpre, 0.0) + _log1p_exp_neg_abs(dpre)
    adt = dt * (-jnp.exp(alog_ref[...]))
    acum = jnp.where(dcols, _dot_exact(jnp.where(causal, 1.0, 0.0), adt), 0.0)
    at = acum.T
    alast = _group_last(acum, lc)
    exp_a = jnp.exp(acum)
    wx = jnp.exp(alast - acum)
    g_a = jnp.exp(alast)

    def pair_bcast(slab, c0):
        return jnp.where(low, slab[:, c0:c0 + 1], slab[:, c0 + 1:c0 + 2])

    def pair_cols(g, p):
        start = g * GROUP_CH + p * PAIR_W
        return slice(start, start + PAIR_W)

    def b_cols(g):
        return slice(GROUP_W + g * N_B, GROUP_W + (g + 1) * N_B)

    def c_cols(g):
        return slice(GROUP_W + G_B * N_B + g * N_B, GROUP_W + G_B * N_B + (g + 1) * N_B)

    xw_pairs = {}
    xdt_pairs = {}
    for g in range(G_B):
        for p in range(HEADS_PER_GROUP // 2):
            c0 = GATE_DT + g * HEADS_PER_GROUP + 2 * p
            xdt = xc[:, pair_cols(g, p)] * pair_bcast(dt, c0)
            xdt_pairs[g, p] = xdt.astype(BF16)
            xw_pairs[g, p] = xdt * pair_bcast(wx, c0)
    xc_scr[...] = xc
    g_scr[...] = g_a
    for g in range(G_B):
        for p in range(HEADS_PER_GROUP // 2):
            xw_scr[:, pair_cols(g, p)] = xw_pairs[g, p]

    def seq_body(j, carry):
        r0 = pl.multiple_of(j * lc, lc)
        grow = g_scr[pl.ds(r0, 1), :]
        for g in range(G_B):
            bj = xc_scr[pl.ds(r0, lc), b_cols(g)].astype(BF16)
            cj = xc_scr[pl.ds(r0, lc), c_cols(g)].astype(BF16)
            xwj = xw_scr[pl.ds(r0, lc), g * GROUP_CH:(g + 1) * GROUP_CH].astype(BF16)
            hs = [h0_ref[j, g * HEADS_PER_GROUP + r] for r in range(HEADS_PER_GROUP)]
            h_old = jnp.concatenate(hs, axis=0)
            ch_scr[g, pl.ds(r0, lc), :] = _dot_nt(cj, h_old.astype(BF16))
            upd = _dot_tn(xwj, bj)
            for r in range(HEADS_PER_GROUP):
                hd = g * HEADS_PER_GROUP + r
                c = GATE_DT + hd
                h_ref[j, hd] = grow[:, c:c + 1] * hs[r] + upd[r * P_B:(r + 1) * P_B, :]
        return carry

    lax.fori_loop(0, nseq, seq_body, 0, unroll=SEQ_UNROLL)

    cbs = [_dot_nt(xc[:, c_cols(g)].astype(BF16), xc[:, b_cols(g)].astype(BF16)) for g in range(G_B)]
    for g in range(G_B):
        cb = cbs[g]
        ys = []
        for p in range(HEADS_PER_GROUP // 2):
            c0 = GATE_DT + g * HEADS_PER_GROUP + 2 * p
            xpair = xc[:, pair_cols(g, p)]
            xdt = xdt_pairs[g, p]
            halves = []
            for c in (c0, c0 + 1):
                dec = jnp.where(causal, jnp.exp(acum[:, c:c + 1] - at[c:c + 1, :]), 0.0)
                halves.append(_dot((cb * dec).astype(BF16), xdt))
            y = (jnp.where(low, halves[0], halves[1])
                 + pair_bcast(exp_a, c0) * ch_scr[g, :, p * PAIR_W:(p + 1) * PAIR_W])
            ys.append(y + dskip_ref[:, pair_cols(g, p)] * xpair)
        yg = jnp.concatenate(ys, axis=1)
        gcols = slice(g * GROUP_CH, (g + 1) * GROUP_CH)
        yz = yg * _silu(pb_ref[:, gcols])
        yn = yz * lax.rsqrt(jnp.mean(yz * yz, axis=1, keepdims=True) + EPS) * norm_ref[:, gcols]
        out_ref[:, gcols] = yn.astype(out_ref.dtype)


def _ssd_prompt(pt, gt, gb16, alog16, dskipb, cwb, cbb, normb, bsz, nchunk, layer):
    seq = nchunk * ROWS
    out, h1, tail = pl.pallas_call(
        functools.partial(_ssd_prompt_body, nb=SEQ_PER_STEP),
        grid=(bsz // SEQ_PER_STEP, nchunk),
        in_specs=[_seq_t_spec(GROUP_W + CONV_DIM), _seq_t_spec(GATE_ROWS),
                  _layer_weight_spec((GATE_ROWS, LANES), layer), _layer_weight_spec((GATE_ROWS, LANES), layer),
                  _layer_weight_spec((GROUP_W, LANES), layer), _layer_weight_spec((CONV_W, CONV_DIM, LANES), layer),
                  _layer_weight_spec((CONV_DIM, LANES), layer), _layer_weight_spec((GROUP_W, LANES), layer)],
        out_specs=[_seq_spec(GROUP_W), _seq_state_spec((H_B, P_B, N_B)), _seq_state_spec((CONV_DIM, ROWS))],
        out_shape=[jax.ShapeDtypeStruct((bsz, seq, GROUP_W), BF16),
                   jax.ShapeDtypeStruct((bsz, H_B, P_B, N_B), F32),
                   jax.ShapeDtypeStruct((bsz, CONV_DIM, ROWS), F32)],
        scratch_shapes=[pltpu.VMEM((SEQ_PER_STEP, CONV_DIM, ROWS), F32)],
        compiler_params=_mixer_params(2),
        name="ssd_prompt",
    )(pt.reshape(bsz, nchunk, GROUP_W + CONV_DIM, ROWS), gt.reshape(bsz, nchunk, GATE_ROWS, ROWS),
      gb16, alog16, dskipb, cwb, cbb, normb)
    return out.reshape(bsz * seq, GROUP_W), h1, tail


def _ssd_sample(pb, pg, gbias, alog, dskip, cw, cb, norm, h_all, hist, lc, layer, h_prev):
    t = pb.shape[0]
    nseq = ROWS // lc
    bsz = h_all.shape[1]
    row = lambda i: (i, 0)
    return pl.pallas_call(
        functools.partial(_ssd_sample_body, lc=lc, nseq=nseq),
        grid=(t // ROWS,),
        in_specs=[pl.BlockSpec((ROWS, GROUP_W + CONV_DIM), row), pl.BlockSpec((ROWS, LANES), row),
                  _layer_weight_spec((1, LANES), layer), _layer_weight_spec((1, LANES), layer),
                  _layer_weight_spec((1, GROUP_W), layer), _layer_weight_spec((CONV_W, CONV_DIM), layer),
                  _layer_weight_spec((1, CONV_DIM), layer), _layer_weight_spec((1, GROUP_W), layer),
                  _layer_state_spec((nseq, H_B, P_B, N_B), layer),
                  _layer_state_spec((ROWS, CONV_DIM), layer), _STACK_SPEC],
        out_specs=[pl.BlockSpec((ROWS, GROUP_W), row),
                   _layer_state_spec((nseq, H_B, P_B, N_B), layer),
                   pl.BlockSpec((nseq, CONV_W - 1, CONV_DIM), lambda i: (i, 0, 0))],
        out_shape=[jax.ShapeDtypeStruct((t, GROUP_W), BF16),
                   jax.ShapeDtypeStruct(h_all.shape, F32),
                   jax.ShapeDtypeStruct((bsz, CONV_W - 1, CONV_DIM), F32)],
        input_output_aliases={10: 1},
        scratch_shapes=[pltpu.VMEM((G_B, ROWS, GROUP_CH), F32), pltpu.VMEM((ROWS, CONV_DIM), F32),
                        pltpu.VMEM((ROWS, GROUP_W), F32), pltpu.VMEM((ROWS, LANES), F32)],
        compiler_params=_mixer_params(1),
        name="ssd_sample",
    )(pb, pg, gbias, alog, dskip, cw, cb, norm, h_all, hist, h_prev)


def _rope_tables(pos):
    half = DH // 2
    freqs = ROPE_BASE ** (-jnp.arange(half, dtype=F32) / half)
    ang = pos.astype(F32)[:, None] * freqs
    cos, sin = jnp.cos(ang), jnp.sin(ang)
    return jnp.concatenate([cos, cos], axis=-1), jnp.concatenate([-sin, sin], axis=-1)


def kernel(x_prompt, x_sample, state_mlstm_C, state_mlstm_n, state_mlstm_m, state_ssd, state_conv, state_ret,
           ffn1_norm, ffn1_w1, ffn1_w3, ffn1_w2, mix_norm, w_in, b_igate, b_fgate, mlstm_norm,
           conv_w, conv_b, dt_bias, a_log, d_skip, ssd_norm, ret_norm, w_out,
           ffn2_norm, ffn2_w1, ffn2_w3, ffn2_w2, final_norm):
    depth = w_in.shape[0]
    bsz, seq, _ = x_prompt.shape
    dbsz, dseq, _ = x_sample.shape
    assert seq % CHUNK == 0 and ROWS % dseq == 0 and (dbsz * dseq) % ROWS == 0
    assert dseq >= SUBLANES and dseq & (dseq - 1) == 0
    nchunk = seq // CHUNK
    seq_per_block = ROWS // dseq

    xp = x_prompt.reshape(bsz * seq, D_MODEL)
    xs = x_sample.reshape(dbsz * dseq, D_MODEL)
    t_prompt, t_sample = bsz * seq, dbsz * dseq
    assert t_prompt % TM_FFN == 0 and t_sample % TM_FFN == 0

    cos_p, sin_p = _rope_tables(jnp.arange(seq))
    cos_s, sin_s = _rope_tables(PAST_LEN + jnp.arange(dseq))
    cos_s, sin_s = jnp.tile(cos_s, (seq_per_block, 1)), jnp.tile(sin_s, (seq_per_block, 1))

    a0 = 0
    a_gate = a0 + 4 * GROUP_W
    b0 = a_gate + 2 * H_A
    b_dt = b0 + GROUP_W + CONV_DIM
    c0 = b_dt + H_B

    outs_p = [[] for _ in range(6)]
    outs_s = [[] for _ in range(3)]
    c_stack = jnp.zeros(state_mlstm_C.shape, F32)
    h_stack = jnp.zeros(state_ssd.shape, F32)
    s_stack = jnp.zeros(state_ret.shape, F32)
    w1a, w3a, w2a = ffn1_w1.astype(BF16), ffn1_w3.astype(BF16), ffn1_w2.astype(BF16)
    w1b, w3b, w2b = ffn2_w1.astype(BF16), ffn2_w3.astype(BF16), ffn2_w2.astype(BF16)
    wo = w_out.astype(BF16)
    fin = final_norm[None, :]

    row3 = lambda v: v.astype(F32)[:, None, :]
    lanes_bcast = lambda v: jnp.broadcast_to(v.astype(F32)[..., None], v.shape + (LANES,))
    w_in_t = jnp.swapaxes(w_in, 1, 2).astype(BF16)
    cols = lambda lo, hi: w_in_t[:, lo:hi, :]
    qa, ka, va, ga = (cols(a0 + i * GROUP_W, a0 + (i + 1) * GROUP_W) for i in range(4))
    qc, kc, vc, gc = (cols(c0 + i * GROUP_W, c0 + (i + 1) * GROUP_W) for i in range(4))
    w_gates = jnp.concatenate([cols(a_gate, b0), cols(b_dt, c0)], axis=1)
    n_gates = w_gates.shape[1]
    zeros_dt = jnp.zeros((depth, GATE_DT), F32)
    g1, gm, g2 = row3(ffn1_norm), row3(mix_norm), row3(ffn2_norm)
    wa, wb, wc = cols(a0, a_gate), cols(b0, b_dt), cols(c0, c0 + 4 * GROUP_W)
    wg = jnp.pad(w_gates, ((0, 0), (0, LANES - n_gates), (0, 0)))
    gate_row = lambda parts: row3(jnp.pad(jnp.concatenate(parts, axis=1), ((0, 0), (0, LANES - n_gates))))
    gbias = gate_row([b_igate, b_fgate, dt_bias])
    alog = gate_row([zeros_dt, a_log])
    dskip_ch = jnp.repeat(d_skip.astype(F32), P_B, axis=1)
    dskip, na, nb, nc = row3(dskip_ch), row3(mlstm_norm), row3(ssd_norm), row3(ret_norm)
    cw, cb = conv_w.astype(F32), row3(conv_b)
    m0rows = jnp.pad(jnp.repeat(state_mlstm_m.astype(F32), dseq, axis=1),
                     ((0, 0), (0, 0), (GATE_F, LANES - GATE_F - H_A)))
    hist = jnp.pad(state_conv.astype(F32), ((0, 0), (0, 0), (dseq - (CONV_W - 1), 0), (0, 0)))
    hist = hist.reshape(depth, dbsz * dseq, CONV_DIM)
    wta = jnp.concatenate([qa, va, ga], axis=1)
    wtb = wb
    wtc = jnp.concatenate([qc, vc, gc], axis=1)
    wtg = w_gates
    wk = jnp.concatenate([ka, kc], axis=1)
    gb8 = lanes_bcast(jnp.concatenate([b_igate, b_fgate], axis=1))
    gb16 = lanes_bcast(jnp.concatenate([zeros_dt, dt_bias], axis=1))
    alog16 = lanes_bcast(jnp.concatenate([zeros_dt, a_log], axis=1))
    nab, nbb, ncb = lanes_bcast(mlstm_norm), lanes_bcast(ssd_norm), lanes_bcast(ret_norm)
    dskipb, cwb, cbb = lanes_bcast(dskip_ch), lanes_bcast(conv_w), lanes_bcast(conv_b)

    for l in range(depth):
        final = l == depth - 1

        if l == 0:
            xa = _ffn_two_source_call(xp, xs, g1, w1a, w3a, w2a, l)
        else:
            xa = _ffn_call(xa, g1, w1a, w3a, w2a, l)

        pta, ptb, ptc, gt, pk = _inproj_prompt_call(xa, gm, wta, wtb, wtc, wtg, wk, l, t_prompt)
        oa, c1, n1, m1 = _mlstm_prompt(pta, pk, gt, gb8, nab, bsz, nchunk, l)
        ob, h1, tail = _ssd_prompt(ptb, gt, gb16, alog16, dskipb, cwb, cbb, nbb, bsz, nchunk, l)
        oc, s1 = _ret_prompt(ptc, pk, cos_p, sin_p, ncb, bsz, nchunk, l)
        mix_p = (oa, ob, oc)
        buf1 = jnp.swapaxes(tail[:, :, ROWS - (CONV_W - 1):], 1, 2)
        for acc, v in zip(outs_p, (c1, n1, m1[:, GATE_F:GATE_F + H_A, 0], h1, buf1, s1)):
            acc.append(v)

        pa, pb, pc, pg = _inproj_call(xa, gm, wa, wb, wc, wg, l, t_sample, t_prompt // TM)
        oa, c_stack, n1, m1 = _mlstm_sample(pa, pg, gbias, na, state_mlstm_C, state_mlstm_n, m0rows, dseq,
                                            l, c_stack)
        ob, h_stack, buf1 = _ssd_sample(pb, pg, gbias, alog, dskip, cw, cb, nb, state_ssd, hist, dseq, l, h_stack)
        oc, s_stack = _ret_sample(pc, cos_s, sin_s, nc, state_ret, dseq, l, s_stack)
        mix_s = (oa, ob, oc)
        for acc, v in zip(outs_s, (n1, m1[:, GATE_F:GATE_F + H_A], buf1)):
            acc.append(v)

        if final:
            xp = _mix_ffn_call(xa, *mix_p, wo, g2, w1b, w3b, w2b, fin, True, l)
            xs = _mix_ffn_call(xa, *mix_s, wo, g2, w1b, w3b, w2b, fin, True, l, tile0=t_prompt // TM_FFN)
        else:
            xa = _mix_ffn_two_source_call(xa, mix_p, mix_s, wo, g2, w1b, w3b, w2b, l)

    y_prompt = xp.reshape(bsz, seq, D_MODEL)
    y_sample = xs.reshape(dbsz, dseq, D_MODEL)
    s_n, s_m, s_buf = [jnp.stack(a) for a in outs_s]
    return (y_prompt, y_sample, *[jnp.stack(a) for a in outs_p], c_stack, s_n, s_m, h_stack, s_buf, s_stack)
```
